```python
import jax
import jax.numpy as jnp
from jax import lax
import numpy as np

D_MODEL = 4096
BATCH = 1
SEQ = 16384
DEPTH = 2

CHUNK = 64
N_META = 16
N_PAD = CHUNK - N_META

RET_HEADS = 12
RET_DK = 128
RET_DV = 128
RET_W = RET_HEADS * RET_DV
ROPE_BASE = 10000.0

GLA_HEADS = 6
GLA_DK = 128
GLA_DV = 256
GLA_WK = GLA_HEADS * GLA_DK
GLA_WV = GLA_HEADS * GLA_DV
GLA_RANK = 16
GLA_TAU = 16.0

RWKV_HEADS = 16
RWKV_DH = 64
RWKV_W = RWKV_HEADS * RWKV_DH
RWKV_RANK_W = 64
RWKV_RANK_A = 64
RWKV_RANK_G = 128

N_BRANCH = 3
RET_SIZES = (RET_W, RET_W, RET_W, RET_W)
GLA_SIZES = (GLA_WK, GLA_WK, GLA_WV, GLA_WV, GLA_RANK)
RWKV_SIZES = (RWKV_W, RWKV_W, RWKV_W, RWKV_RANK_W, RWKV_RANK_A, RWKV_RANK_G)
GATE_SIZES = (D_MODEL,) * N_BRANCH
BLOCK_SIZES = (sum(RET_SIZES), sum(GLA_SIZES), sum(RWKV_SIZES), sum(GATE_SIZES))
D_IN = sum(BLOCK_SIZES)

N_EXPERTS = 32
N_GROUPS = 4
EXPERTS_PER_GROUP = N_EXPERTS // N_GROUPS
TOP_K = 2
D_FF = 704

DN_ALPHA = (2 * DEPTH) ** 0.25
DN_BETA = (8 * DEPTH) ** -0.25
LN_EPS = 1e-5
GN_EPS = 1e-5
RWKV_GN_EPS = 64e-5
NORM_EPS = 1e-12

kernel_name = 'hybrid_ret_gla_rwkv7_grouped_moe_deepnorm'


def _split(t, sizes):
    return jnp.split(t, np.cumsum(sizes)[:-1].tolist(), axis=-1)


def _layernorm(x, g, b):
    xf = x.astype(jnp.float32)
    mu = xf.mean(-1, keepdims=True)
    var = jnp.square(xf - mu).mean(-1, keepdims=True)
    y = (xf - mu) * lax.rsqrt(var + LN_EPS) * g.astype(jnp.float32) + b.astype(jnp.float32)
    return y.astype(x.dtype)


def _groupnorm(o, g, b, eps=GN_EPS):
    o = o.astype(jnp.float32)
    mu = o.mean(-1, keepdims=True)
    var = jnp.square(o - mu).mean(-1, keepdims=True)
    o = ((o - mu) * lax.rsqrt(var + eps)).reshape(*o.shape[:-2], -1)
    return o * g.astype(jnp.float32) + b.astype(jnp.float32)


def _to_chunks(t):
    B, L = t.shape[:2]
    return jnp.moveaxis(t.reshape(B, L // CHUNK, CHUNK, *t.shape[2:]), 1, 0)


def _from_chunks(t):
    t = jnp.moveaxis(t, 0, 1)
    return t.reshape(t.shape[0], -1, *t.shape[3:])


def _rotary(t, pos):
    half = t.shape[-1] // 2
    inv = ROPE_BASE ** (-jnp.arange(half, dtype=jnp.float32) / half)
    ang = pos.astype(jnp.float32)[:, None] * inv[None, :]
    cos = jnp.cos(ang)[None, :, None, :]
    sin = jnp.sin(ang)[None, :, None, :]
    t1, t2 = t[..., :half], t[..., half:]
    return jnp.concatenate([t1 * cos - t2 * sin, t1 * sin + t2 * cos], axis=-1)


def _retention(q, k, v, g, pos, gn_g, gn_b):
    B, L, _ = q.shape
    dt = q.dtype
    q = _rotary(q.astype(jnp.float32).reshape(B, L, RET_HEADS, RET_DK), pos) * RET_DK ** -0.5
    k = _rotary(k.astype(jnp.float32).reshape(B, L, RET_HEADS, RET_DK), pos)
    v = v.astype(jnp.float32).reshape(B, L, RET_HEADS, RET_DV)
    lg = jnp.log1p(-jnp.exp2(-5.0 - jnp.arange(RET_HEADS, dtype=jnp.float32)))
    i = jnp.arange(CHUNK, dtype=jnp.float32)
    intra = jnp.exp(lg[:, None, None] * jnp.abs(i[:, None] - i[None, :]))
    q_dec = jnp.exp(lg[None, :] * (i[:, None] + 1.0))[:, :, None]
    k_dec = jnp.exp(lg[None, :] * (CHUNK - 1.0 - i[:, None]))[:, :, None]
    c_dec = jnp.exp(lg * CHUNK)[:, None, None]

    def step(R, xs):
        qc, kc, vc = xs
        s = jnp.einsum('bihd,bjhd->bhij', qc, kc) * intra
        o = jnp.einsum('bhij,bjhe->bihe', s, vc) + jnp.einsum('bihd,bhde->bihe', qc * q_dec, R)
        R = R * c_dec + jnp.einsum('bjhd,bjhe->bhde', kc * k_dec, vc)
        return R, o

    R0 = jnp.zeros((B, RET_HEADS, RET_DK, RET_DV), jnp.float32)
    _, o = lax.scan(step, R0, (_to_chunks(q), _to_chunks(k), _to_chunks(v)))
    o = _groupnorm(_from_chunks(o), gn_g, gn_b)
    return (o * jax.nn.silu(g.astype(jnp.float32))).astype(dt)


def _gla(q, k, v, g, a_low, a_up, a_bias, gn_g, gn_b):
    B, L, _ = q.shape
    dt = q.dtype
    log_a = jax.nn.log_sigmoid((a_low @ a_up).astype(jnp.float32) + a_bias.astype(jnp.float32)) / GLA_TAU
    q = q.astype(jnp.float32).reshape(B, L, GLA_HEADS, GLA_DK) * GLA_DK ** -0.5
    k = k.astype(jnp.float32).reshape(B, L, GLA_HEADS, GLA_DK)
    v = v.astype(jnp.float32).reshape(B, L, GLA_HEADS, GLA_DV)
    cum = jnp.cumsum(_to_chunks(log_a.reshape(B, L, GLA_HEADS, GLA_DK)), axis=2)
    last = cum[:, :, -1:]
    k_c = _to_chunks(k) * jnp.exp(last - cum)
    dec = jnp.exp(last[:, :, 0])

    def step(S, xs):
        qc, kc, vc, dc = xs
        S = S * dc[..., None] + jnp.einsum('bjhk,bjhv->bhkv', kc, vc)
        return S, jnp.einsum('bihk,bhkv->bihv', qc, S)

    S0 = jnp.zeros((B, GLA_HEADS, GLA_DK, GLA_DV), jnp.float32)
    _, o = lax.scan(step, S0, (_to_chunks(q), k_c, _to_chunks(v), dec))
    o = _groupnorm(_from_chunks(o), gn_g, gn_b)
    return (o * jax.nn.silu(g.astype(jnp.float32))).astype(dt)


def _rwkv7(z, w_up, w0, a_up, a0, g_up, k_k, k_a, r_k, gn_g, gn_b):
    B, L, _ = z.shape
    dt = z.dtype
    f32 = lambda t: t.astype(jnp.float32)
    r, k, v, wl, al, gl = _split(f32(z), RWKV_SIZES)
    w = -jax.nn.softplus(-(f32(w0) + jnp.tanh(wl) @ f32(w_up))) - 0.5
    a = jax.nn.sigmoid(f32(a0) + al @ f32(a_up))
    g = jax.nn.sigmoid(gl) @ f32(g_up)
    hd = lambda t: t.reshape(B, L, RWKV_HEADS, RWKV_DH)
    kk = hd(k * f32(k_k))
    kk = kk / jnp.maximum(jnp.sqrt(jnp.sum(kk * kk, -1, keepdims=True)), NORM_EPS)
    k = hd(k * (1.0 + (a - 1.0) * f32(k_a)))
    r, v, a, decay = hd(r), hd(v), hd(a), hd(jnp.exp(-jnp.exp(w)))
    b = kk * a

    def frame(S, xs):
        d_t, kk_t, b_t, k_t, v_t = xs
        Sa = jnp.einsum('bhvk,bhk->bhv', S, kk_t)
        S = S * d_t[:, :, None, :] - Sa[..., None] * b_t[:, :, None, :] + v_t[..., None] * k_t[:, :, None, :]
        return S, None

    def chunk(S, xs):
        S, _ = lax.scan(frame, S, xs)
        return S, S

    frames = lambda t: jnp.moveaxis(_to_chunks(t), 2, 1)
    S0 = jnp.zeros((B, RWKV_HEADS, RWKV_DH, RWKV_DH), jnp.float32)
    _, S_end = lax.scan(chunk, S0, (frames(decay), frames(kk), frames(b), frames(k), frames(v)))
    o = jnp.einsum('cbhvk,cbihk->cbihv', S_end, _to_chunks(r))
    o = _groupnorm(_from_chunks(o), gn_g, gn_b, RWKV_GN_EPS)
    bonus = (jnp.sum(r * k * f32(r_k), -1, keepdims=True) * v).reshape(B, L, RWKV_W)
    return ((o + bonus) * g).astype(dt)


def _mixer(h, valid, pos, w_in, ret_gn_g, ret_gn_b, gla_a_up, gla_a_bias, gla_gn_g, gla_gn_b,
           rw_mix, rw_w_up, rw_w0, rw_a_up, rw_a0, rw_g_up, rw_k_k, rw_k_a, rw_r_k, rw_gn_g, rw_gn_b,
           w_br_ret, w_br_gla, w_br_rw, w_out):
    proj = (h * valid[None, :, None].astype(h.dtype)) @ w_in
    ret, gla, rw, gates = _split(proj, BLOCK_SIZES)
    rq, rk, rv, rg = _split(ret, RET_SIZES)
    gq, gk, gv, gg, ga = _split(gla, GLA_SIZES)
    rw_prev = jnp.pad(rw, ((0, 0), (1, 0), (0, 0)))[:, :-1]
    rw = rw + (rw_prev - rw) * rw_mix.astype(rw.dtype)
    o_ret = _retention(rq, rk, rv, rg, pos, ret_gn_g, ret_gn_b)
    o_gla = _gla(gq, gk, gv, gg, ga, gla_a_up, gla_a_bias, gla_gn_g, gla_gn_b)
    o_rw = _rwkv7(rw, rw_w_up, rw_w0, rw_a_up, rw_a0, rw_g_up, rw_k_k, rw_k_a, rw_r_k, rw_gn_g, rw_gn_b)
    g_ret, g_gla, g_rw = _split(gates, GATE_SIZES)
    merged = (jax.nn.sigmoid(g_ret) * (o_ret @ w_br_ret)
              + jax.nn.sigmoid(g_gla) * (o_gla @ w_br_gla)
              + jax.nn.sigmoid(g_rw) * (o_rw @ w_br_rw))
    return merged @ w_out


def _moe(h, router_w, router_bias, w1, w3, w2):
    B, L, D = h.shape
    t = h.reshape(B * L, D)
    s = jax.nn.sigmoid((t @ router_w).astype(jnp.float32))
    sel = (s + router_bias.astype(jnp.float32)).reshape(-1, N_GROUPS, EXPERTS_PER_GROUP)
    group_score = lax.top_k(sel, TOP_K)[0].sum(-1)
    best = jnp.argmax(group_score, axis=-1)
    keep = (jnp.arange(N_GROUPS)[None, :] == best[:, None])[..., None]
    sel = jnp.where(keep, sel, -jnp.inf).reshape(-1, N_EXPERTS)
    idx = lax.top_k(sel, TOP_K)[1]
    wts = jnp.take_along_axis(s, idx, axis=-1)
    wts = wts / wts.sum(-1, keepdims=True)
    combine = jnp.einsum('tk,tke->te', wts, jax.nn.one_hot(idx, N_EXPERTS, dtype=jnp.float32)).astype(t.dtype)
    out = jnp.zeros_like(t)
    for e in range(N_EXPERTS):
        he = jax.nn.silu(t @ w1[e]) * (t @ w3[e])
        out = out + combine[:, e:e + 1] * (he @ w2[e])
    return out.reshape(B, L, D)


def setup_inputs(seed: int = 0) -> dict:
    key = jax.random.key(seed)
    ks = iter(jax.random.split(key, 48))
    nrm = lambda shape, scale: jax.random.normal(next(ks), shape, jnp.float32) * scale
    gain = lambda shape: 1.0 + nrm(shape, 0.02)
    return {
        'x': nrm((BATCH, SEQ, D_MODEL), 1.0),
        'meta': nrm((N_META, D_MODEL), 1.0),
        'ln_in_g': gain((D_MODEL,)),
        'ln_in_b': nrm((D_MODEL,), 0.02),
        'w_in': nrm((DEPTH, D_MODEL, D_IN), D_MODEL ** -0.5),
        'ret_gn_g': gain((DEPTH, RET_W)),
        'ret_gn_b': nrm((DEPTH, RET_W), 0.02),
        'gla_a_up': nrm((DEPTH, GLA_RANK, GLA_WK), GLA_RANK ** -0.5),
        'gla_a_bias': nrm((DEPTH, GLA_WK), 0.1),
        'gla_gn_g': gain((DEPTH, GLA_WV)),
        'gla_gn_b': nrm((DEPTH, GLA_WV), 0.02),
        'rw_mix': jax.random.uniform(next(ks), (DEPTH, sum(RWKV_SIZES)), jnp.float32),
        'rw_w_up': nrm((DEPTH, RWKV_RANK_W, RWKV_W), 0.1),
        'rw_w0': jax.random.uniform(next(ks), (DEPTH, RWKV_W), jnp.float32, -6.0, -1.0),
        'rw_a_up': nrm((DEPTH, RWKV_RANK_A, RWKV_W), 0.1),
        'rw_a0': nrm((DEPTH, RWKV_W), 0.1),
        'rw_g_up': nrm((DEPTH, RWKV_RANK_G, RWKV_W), RWKV_RANK_G ** -0.5),
        'rw_k_k': 0.85 + nrm((DEPTH, RWKV_W), 0.02),
        'rw_k_a': 1.0 + nrm((DEPTH, RWKV_W), 0.02),
        'rw_r_k': nrm((DEPTH, RWKV_HEADS, RWKV_DH), 0.1),
        'rw_gn_g': gain((DEPTH, RWKV_W)),
        'rw_gn_b': nrm((DEPTH, RWKV_W), 0.02),
        'w_br_ret': nrm((DEPTH, RET_W, D_MODEL), RET_W ** -0.5),
        'w_br_gla': nrm((DEPTH, GLA_WV, D_MODEL), GLA_WV ** -0.5),
        'w_br_rw': nrm((DEPTH, RWKV_W, D_MODEL), RWKV_W ** -0.5),
        'w_out': nrm((DEPTH, D_MODEL, D_MODEL), D_MODEL ** -0.5 * DN_BETA),
        'ln1_g': gain((DEPTH, D_MODEL)),
        'ln1_b': nrm((DEPTH, D_MODEL), 0.02),
        'router_w': nrm((D_MODEL, N_EXPERTS), D_MODEL ** -0.5),
        'router_bias': nrm((N_EXPERTS,), 0.01),
        'exp_w1': nrm((DEPTH, N_EXPERTS, D_MODEL, D_FF), D_MODEL ** -0.5),
        'exp_w3': nrm((DEPTH, N_EXPERTS, D_MODEL, D_FF), D_MODEL ** -0.5),
        'exp_w2': nrm((DEPTH, N_EXPERTS, D_FF, D_MODEL), D_FF ** -0.5 * DN_BETA),
        'ln2_g': gain((DEPTH, D_MODEL)),
        'ln2_b': nrm((DEPTH, D_MODEL), 0.02),
    }


def reference(x, meta, ln_in_g, ln_in_b, w_in, ret_gn_g, ret_gn_b, gla_a_up, gla_a_bias, gla_gn_g, gla_gn_b,
              rw_mix, rw_w_up, rw_w0, rw_a_up, rw_a0, rw_g_up, rw_k_k, rw_k_a, rw_r_k, rw_gn_g, rw_gn_b,
              w_br_ret, w_br_gla, w_br_rw, w_out, ln1_g, ln1_b, router_w, router_bias,
              exp_w1, exp_w3, exp_w2, ln2_g, ln2_b):
    B = x.shape[0]
    pad = jnp.zeros((B, N_PAD, D_MODEL), x.dtype)
    h = jnp.concatenate([pad, jnp.broadcast_to(meta.astype(x.dtype), (B, N_META, D_MODEL)), x], axis=1)
    L = h.shape[1]
    pos = jnp.arange(L, dtype=jnp.int32) - N_PAD
    valid = pos >= 0
    h = _layernorm(h, ln_in_g, ln_in_b)
    for l in range(DEPTH):
        mix = _mixer(h, valid, pos, w_in[l], ret_gn_g[l], ret_gn_b[l], gla_a_up[l], gla_a_bias[l],
                     gla_gn_g[l], gla_gn_b[l], rw_mix[l], rw_w_up[l], rw_w0[l], rw_a_up[l], rw_a0[l],
                     rw_g_up[l], rw_k_k[l], rw_k_a[l], rw_r_k[l], rw_gn_g[l], rw_gn_b[l],
                     w_br_ret[l], w_br_gla[l], w_br_rw[l], w_out[l])
        h = _layernorm(DN_ALPHA * h + mix, ln1_g[l], ln1_b[l])
        ffn = _moe(h, router_w, router_bias, exp_w1[l], exp_w3[l], exp_w2[l])
        h = _layernorm(DN_ALPHA * h + ffn, ln2_g[l], ln2_b[l])
    return h[:, N_PAD + N_META:]
```

```python
import functools

import jax
import jax.numpy as jnp
from jax import lax
from jax.experimental import pallas as pl
from jax.experimental.pallas import tpu as pltpu

F32 = jnp.float32
BF16 = jnp.bfloat16

D_MODEL = 4096
DEPTH = 2
CHUNK = 64
N_META = 16
N_PAD = CHUNK - N_META

RET_HEADS, RET_DK, RET_DV = 12, 128, 128
RET_W = RET_HEADS * RET_DV
ROPE_BASE = 10000.0

GLA_HEADS, GLA_DK, GLA_DV = 6, 128, 256
GLA_WK = GLA_HEADS * GLA_DK
GLA_WV = GLA_HEADS * GLA_DV
GLA_RANK = 16
GLA_TAU = 16.0

RWKV_HEADS, RWKV_DH = 16, 64
RWKV_W = RWKV_HEADS * RWKV_DH
RWKV_RANK_W, RWKV_RANK_A, RWKV_RANK_G = 64, 64, 128

N_EXPERTS = 32
N_GROUPS = 4
EXPERTS_PER_GROUP = N_EXPERTS // N_GROUPS
D_FF = 704

DN_ALPHA = (2 * DEPTH) ** 0.25
LN_EPS = 1e-5
GN_EPS = 1e-5
RWKV_GN_EPS = 64e-5
NORM_EPS = 1e-12

LANES = 128
SUBLANES = 8
VMEM_LIMIT_BYTES = 56 * 1024 * 1024

RW_R0, RW_K0, RW_V0 = 0, RWKV_W, 2 * RWKV_W
RET_Q0 = 3 * RWKV_W
RET_K0, RET_V0, RET_G0 = RET_Q0 + RET_W, RET_Q0 + 2 * RET_W, RET_Q0 + 3 * RET_W
GLA_Q0 = RET_Q0 + 4 * RET_W
GLA_K0 = GLA_Q0 + GLA_WK
GLA_V0 = GLA_K0 + GLA_WK
GLA_G0 = GLA_V0 + GLA_WV
GATE0 = GLA_G0 + GLA_WV
N_MAIN = GATE0 + 3 * D_MODEL
N_SMALL = 3 * LANES
D_FF_PAD = 768

ROW_TILE = 256
MOE_TM = 256


def _cparams(sem):
    return pltpu.CompilerParams(dimension_semantics=sem, vmem_limit_bytes=VMEM_LIMIT_BYTES)


def _pick(n, cands):
    for c in cands:
        if n % c == 0:
            return c
    raise ValueError(f"no tile in {cands} divides {n}")


def _dot(a, b):
    return jnp.dot(a, b, preferred_element_type=F32)


def _dot_hi(a, b):
    return jnp.dot(a, b, preferred_element_type=F32, precision=lax.Precision.HIGHEST)


def _dot_nt(a, b):
    return lax.dot_general(a, b, (((1,), (1,)), ((), ())), preferred_element_type=F32)


def _dot_tn(a, b):
    return lax.dot_general(a, b, (((0,), (0,)), ((), ())), preferred_element_type=F32)


def _sigmoid(x):
    return 1.0 / (1.0 + jnp.exp(-x))


def _softplus(x):
    return jnp.maximum(x, 0.0) + jnp.log(1.0 + jnp.exp(-jnp.abs(x)))


def _mm_kernel(x_ref, w_ref, o_ref):
    o_ref[...] = _dot(x_ref[...], w_ref[...]).astype(o_ref.dtype)


def _matmul(x, w, out_dtype, tn):
    m, k = x.shape
    n = w.shape[1]
    tm = _pick(m, (1280, 640, 256, 128, 64))
    return pl.pallas_call(
        _mm_kernel,
        grid=(m // tm, n // tn),
        in_specs=[pl.BlockSpec((tm, k), lambda i, j: (i, 0)), pl.BlockSpec((k, tn), lambda i, j: (0, j))],
        out_specs=pl.BlockSpec((tm, tn), lambda i, j: (i, j)),
        out_shape=jax.ShapeDtypeStruct((m, n), out_dtype),
        compiler_params=_cparams(("parallel", "parallel")),
        name="dense_matmul",
    )(x, w)


def _layernorm_rows(x, g, b):
    mu = jnp.mean(x, axis=-1, keepdims=True)
    xc = x - mu
    var = jnp.mean(xc * xc, axis=-1, keepdims=True)
    return xc * lax.rsqrt(var + LN_EPS) * g + b


def _emit_ln(y, o_ref, ob_ref, n_mask, tm):
    o_ref[...] = y
    yb = y.astype(BF16)
    if n_mask:
        row = pl.program_id(0) * tm + lax.broadcasted_iota(jnp.int32, (tm, 1), 0)
        yb = jnp.where(row >= n_mask, yb, jnp.zeros_like(yb))
    ob_ref[...] = yb


def _res_ln_kernel(h_ref, f_ref, g_ref, b_ref, o_ref, ob_ref, *, alpha, n_mask, tm):
    x = h_ref[...] * alpha + f_ref[...]
    _emit_ln(_layernorm_rows(x, g_ref[...], b_ref[...]), o_ref, ob_ref, n_mask, tm)


def _ln_kernel(h_ref, g_ref, b_ref, o_ref, ob_ref, *, n_mask, tm):
    _emit_ln(_layernorm_rows(h_ref[...], g_ref[...], b_ref[...]), o_ref, ob_ref, n_mask, tm)


def _layer_norm(h, f, g, b, n_mask):
    m, d = h.shape
    tm = _pick(m, (256, 128, 64))
    row = pl.BlockSpec((tm, d), lambda i: (i, 0))
    vec = pl.BlockSpec((1, d), lambda i: (0, 0))
    out_shape = (jax.ShapeDtypeStruct((m, d), F32), jax.ShapeDtypeStruct((m, d), BF16))
    if f is None:
        kern = functools.partial(_ln_kernel, n_mask=n_mask, tm=tm)
        args, specs = (h, g[None], b[None]), [row, vec, vec]
    else:
        kern = functools.partial(_res_ln_kernel, alpha=DN_ALPHA, n_mask=n_mask, tm=tm)
        args, specs = (h, f, g[None], b[None]), [row, row, vec, vec]
    return pl.pallas_call(
        kern, grid=(m // tm,), in_specs=specs, out_specs=(row, row), out_shape=out_shape,
        compiler_params=_cparams(("parallel",)), name="res_layernorm",
    )(*args)


def _groupnorm_lanes(o, g, b, eps):
    mu = jnp.mean(o, axis=-1, keepdims=True)
    oc = o - mu
    var = jnp.mean(oc * oc, axis=-1, keepdims=True)
    return oc * lax.rsqrt(var + eps) * g + b


def _chunks_per_step(length, cands):
    return _pick(length // CHUNK, cands)


def _ret_kernel(q_ref, k_ref, v_ref, g_ref, cos_ref, sin_ref, intra_ref, qdec_ref, kdec_ref, cdec_ref,
                gng_ref, gnb_ref, o_ref, state_ref, *, nc):
    @pl.when(pl.program_id(1) == 0)
    def _():
        state_ref[...] = jnp.zeros_like(state_ref)

    cos, sin = cos_ref[...], sin_ref[...]
    q = q_ref[...].astype(F32)
    k = k_ref[...].astype(F32)
    half = RET_DK // 2
    q = (q * cos + pltpu.roll(q, half, 1) * sin) * (RET_DK ** -0.5)
    k = k * cos + pltpu.roll(k, half, 1) * sin
    v = v_ref[...]
    intra, qdec, kdec = intra_ref[0], qdec_ref[0], kdec_ref[0]
    cdec = cdec_ref[0][0:1]
    state = state_ref[...]
    outs = []
    for j in range(nc):
        sl = slice(j * CHUNK, (j + 1) * CHUNK)
        qc, kc, vc = q[sl], k[sl], v[sl]
        s = _dot_nt(qc.astype(BF16), kc.astype(BF16)) * intra
        outs.append(_dot(s.astype(BF16), vc) + _dot((qc * qdec).astype(BF16), state.astype(BF16)))
        state = state * cdec + _dot_tn((kc * kdec).astype(BF16), vc)
    state_ref[...] = state
    y = _groupnorm_lanes(jnp.concatenate(outs, axis=0), gng_ref[...], gnb_ref[...], GN_EPS)
    gate = g_ref[...].astype(F32)
    o_ref[...] = (y * (gate * _sigmoid(gate))).astype(o_ref.dtype)


def _retention(proj, cos, sin, gn_g, gn_b):
    length = proj.shape[0]
    nc = _chunks_per_step(length, (10, 5, 4, 2, 1))
    t = nc * CHUNK
    lg = jnp.log1p(-jnp.exp2(-5.0 - jnp.arange(RET_HEADS, dtype=F32)))
    i = jnp.arange(CHUNK, dtype=F32)
    intra = jnp.exp(lg[:, None, None] * jnp.abs(i[:, None] - i[None, :]))
    qdec = jnp.broadcast_to(jnp.exp(lg[:, None] * (i[None, :] + 1.0))[:, :, None], (RET_HEADS, CHUNK, RET_DK))
    kdec = jnp.broadcast_to(jnp.exp(lg[:, None] * (CHUNK - 1.0 - i[None, :]))[:, :, None], (RET_HEADS, CHUNK, RET_DK))
    cdec = jnp.broadcast_to(jnp.exp(lg * CHUNK)[:, None, None], (RET_HEADS, SUBLANES, RET_DV))

    def col(c0):
        return pl.BlockSpec((t, RET_DK), lambda h, c: (c, c0 // RET_DK + h))

    tab = pl.BlockSpec((t, RET_DK), lambda h, c: (c, 0))
    per_head3 = lambda s1, s2: pl.BlockSpec((1, s1, s2), lambda h, c: (h, 0, 0))
    vec = pl.BlockSpec((1, RET_DV), lambda h, c: (0, h))
    return pl.pallas_call(
        functools.partial(_ret_kernel, nc=nc),
        grid=(RET_HEADS, length // t),
        in_specs=[col(RET_Q0), col(RET_K0), col(RET_V0), col(RET_G0), tab, tab,
                  per_head3(CHUNK, CHUNK), per_head3(CHUNK, RET_DK), per_head3(CHUNK, RET_DK),
                  per_head3(SUBLANES, RET_DV), vec, vec],
        out_specs=pl.BlockSpec((t, RET_DV), lambda h, c: (c, h)),
        out_shape=jax.ShapeDtypeStruct((length, RET_W), BF16),
        scratch_shapes=[pltpu.VMEM((RET_DK, RET_DV), F32)],
        compiler_params=_cparams(("parallel", "arbitrary")),
        name="retention",
    )(proj, proj, proj, proj, cos, sin, intra, qdec, kdec, cdec, gn_g[None], gn_b[None])


def _gla_kernel(q_ref, k_ref, v_ref, g_ref, ga_ref, aup_ref, ab_ref, gng_ref, gnb_ref, o_ref, state_ref, *, nc):
    @pl.when(pl.program_id(1) == 0)
    def _():
        state_ref[...] = jnp.zeros_like(state_ref)

    z = _dot_hi(ga_ref[...], aup_ref[...]) + ab_ref[...]
    log_a = -_softplus(-z) / GLA_TAU
    q = (q_ref[...].astype(F32) * (GLA_DK ** -0.5)).astype(BF16)
    k = k_ref[...].astype(F32)
    v = v_ref[...]
    row = lax.broadcasted_iota(jnp.int32, (CHUNK, CHUNK), 0)
    colm = lax.broadcasted_iota(jnp.int32, (CHUNK, CHUNK), 1)
    later = (colm > row).astype(F32)
    state = state_ref[...]
    outs = []
    for j in range(nc):
        sl = slice(j * CHUNK, (j + 1) * CHUNK)
        la = log_a[sl]
        to_end = _dot_hi(later, la)
        dec = jnp.exp(to_end[0:1] + la[0:1])
        kc = (k[sl] * jnp.exp(to_end)).astype(BF16)
        state = state * dec + _dot_tn(v[sl], kc)
        outs.append(_dot_nt(q[sl], state.astype(BF16)))
    state_ref[...] = state
    y = _groupnorm_lanes(jnp.concatenate(outs, axis=0), gng_ref[...], gnb_ref[...], GN_EPS)
    gate = g_ref[...].astype(F32)
    o_ref[...] = (y * (gate * _sigmoid(gate))).astype(o_ref.dtype)


def _gla(proj, small, a_up, a_bias, gn_g, gn_b):
    length = proj.shape[0]
    nc = _chunks_per_step(length, (10, 5, 4, 2, 1))
    t = nc * CHUNK
    a_up_pad = jnp.zeros((LANES, GLA_WK), F32).at[:GLA_RANK].set(a_up)
    kcol = lambda c0: pl.BlockSpec((t, GLA_DK), lambda h, c: (c, c0 // GLA_DK + h))
    vcol = lambda c0: pl.BlockSpec((t, GLA_DV), lambda h, c: (c, c0 // GLA_DV + h))
    return pl.pallas_call(
        functools.partial(_gla_kernel, nc=nc),
        grid=(GLA_HEADS, length // t),
        in_specs=[kcol(GLA_Q0), kcol(GLA_K0), vcol(GLA_V0), vcol(GLA_G0),
                  pl.BlockSpec((t, LANES), lambda h, c: (c, 2)),
                  pl.BlockSpec((LANES, GLA_DK), lambda h, c: (0, h)),
                  pl.BlockSpec((1, GLA_DK), lambda h, c: (0, h)),
                  pl.BlockSpec((1, GLA_DV), lambda h, c: (0, h)),
                  pl.BlockSpec((1, GLA_DV), lambda h, c: (0, h))],
        out_specs=pl.BlockSpec((t, GLA_DV), lambda h, c: (c, h)),
        out_shape=jax.ShapeDtypeStruct((length, GLA_WV), BF16),
        scratch_shapes=[pltpu.VMEM((GLA_DV, GLA_DK), F32)],
        compiler_params=_cparams(("parallel", "arbitrary")),
        name="gla",
    )(proj, proj, proj, proj, small, a_up_pad, a_bias[None], gn_g[None], gn_b[None])


RW_PAIRS = RWKV_HEADS // 2
RW_GROUP = 8


def _head_sum(x, ones_blk):
    parts = [_dot_hi(x[:, p * LANES:(p + 1) * LANES], ones_blk) for p in range(RW_PAIRS)]
    return jnp.concatenate(parts, axis=1)


def _rwkv_kernel(r_ref, k_ref, v_ref, sm_ref, mixr_ref, mixk_ref, mixv_ref, mixs_ref, wup_ref, w0_ref,
                 aup_ref, a0_ref, gup_ref, kk_ref, ka_ref, rk_ref, gng_ref, gnb_ref, o_ref,
                 state_ref, cr_ref, ck_ref, cv_ref, cs_ref, d_s, kkn_s, b_s, km_s, v_s, r_s, o_s, g_s, bonus_s, *, nc):
    t = nc * CHUNK

    @pl.when(pl.program_id(0) == 0)
    def _():
        state_ref[...] = jnp.zeros_like(state_ref)
        cr_ref[...] = jnp.zeros_like(cr_ref)
        ck_ref[...] = jnp.zeros_like(ck_ref)
        cv_ref[...] = jnp.zeros_like(cv_ref)
        cs_ref[...] = jnp.zeros_like(cs_ref)

    rowi = lax.broadcasted_iota(jnp.int32, (t, 1), 0)

    def shifted(x, carry_ref, mix):
        prev = jnp.where(rowi == 0, carry_ref[SUBLANES - 1:SUBLANES], pltpu.roll(x, 1, 0))
        carry_ref[...] = x[t - SUBLANES:t]
        return x + (prev - x) * mix

    r = shifted(r_ref[...].astype(F32), cr_ref, mixr_ref[...])
    k = shifted(k_ref[...].astype(F32), ck_ref, mixk_ref[...])
    v = shifted(v_ref[...].astype(F32), cv_ref, mixv_ref[...])
    sm = shifted(sm_ref[...], cs_ref, mixs_ref[...])
    wa, gl = sm[:, :LANES], sm[:, LANES:]

    l0 = lax.broadcasted_iota(jnp.int32, (LANES, LANES), 0)
    l1 = lax.broadcasted_iota(jnp.int32, (LANES, LANES), 1)
    same_head = (l0 // RWKV_DH) == (l1 // RWKV_DH)
    ones_blk = same_head.astype(F32)

    w = -_softplus(-(w0_ref[...] + _dot_hi(jnp.tanh(wa), wup_ref[...]))) - 0.5
    a = _sigmoid(a0_ref[...] + _dot_hi(wa, aup_ref[...]))
    g_s[...] = _dot_hi(_sigmoid(gl), gup_ref[...])
    kk = k * kk_ref[...]
    kkn = kk / jnp.maximum(jnp.sqrt(_head_sum(kk * kk, ones_blk)), NORM_EPS)
    km = k * (1.0 + (a - 1.0) * ka_ref[...])
    bonus_s[...] = _head_sum(r * km * rk_ref[...], ones_blk) * v

    d_s[...] = jnp.exp(-jnp.exp(w))
    kkn_s[...] = kkn
    b_s[...] = kkn * a
    km_s[...] = km
    v_s[...] = v
    r_s[...] = r

    lane = lax.broadcasted_iota(jnp.int32, (RWKV_DH, LANES), 1)
    sub = lax.broadcasted_iota(jnp.int32, (RWKV_DH, LANES), 0)
    lo = lane < RWKV_DH
    diag = (lane % RWKV_DH) == sub
    lo_row = lax.broadcasted_iota(jnp.int32, (1, LANES), 1) < RWKV_DH
    ones_bf = same_head.astype(BF16)

    for j in range(nc):
        def group(gi, carry, j=j):
            t0 = pl.multiple_of(j * CHUNK + gi * RW_GROUP, RW_GROUP)
            for p in range(RW_PAIRS):
                cs = slice(p * LANES, (p + 1) * LANES)
                frames = pl.ds(t0, RW_GROUP)
                v8, d8, b8, km8 = v_s[frames, cs], d_s[frames, cs], b_s[frames, cs], km_s[frames, cs]
                kk8 = kkn_s[frames, cs]
                kk8_lo, kk8_hi = jnp.where(lo_row, kk8, 0.0), jnp.where(lo_row, 0.0, kk8)
                diag_v = jnp.concatenate([jnp.where(diag, v8[u:u + 1], 0.0) for u in range(RW_GROUP)], axis=0)
                v_cols = _dot(diag_v.astype(BF16), ones_bf)
                s = state_ref[p]
                for u in range(RW_GROUP):
                    sa_lo = jnp.sum(s * kk8_lo[u:u + 1], axis=1, keepdims=True)
                    sa_hi = jnp.sum(s * kk8_hi[u:u + 1], axis=1, keepdims=True)
                    sa = jnp.where(lo, sa_lo, sa_hi)
                    s = (s * d8[u:u + 1] - sa * b8[u:u + 1]
                         + v_cols[u * RWKV_DH:(u + 1) * RWKV_DH] * km8[u:u + 1])
                state_ref[p] = s
            return carry

        lax.fori_loop(0, CHUNK // RW_GROUP, group, 0)
        rows = slice(j * CHUNK, (j + 1) * CHUNK)
        for p in range(RW_PAIRS):
            cs = slice(p * LANES, (p + 1) * LANES)
            s_b = state_ref[p].astype(BF16)
            rc = r_s[rows, cs]
            o_lo = _dot_nt(jnp.where(lo, rc, 0.0).astype(BF16), s_b)
            o_hi = _dot_nt(jnp.where(lo, 0.0, rc).astype(BF16), s_b)
            o_s[rows, cs] = jnp.concatenate([o_lo, o_hi], axis=1)

    o = o_s[...]
    mu = _head_sum(o, ones_blk) * (1.0 / RWKV_DH)
    oc = o - mu
    var = _head_sum(oc * oc, ones_blk) * (1.0 / RWKV_DH)
    y = oc * lax.rsqrt(var + RWKV_GN_EPS) * gng_ref[...] + gnb_ref[...]
    o_ref[...] = ((y + bonus_s[...]) * g_s[...]).astype(o_ref.dtype)


def _rwkv(proj, small, mix, w_up, w0, a_up, a0, g_up, k_k, k_a, r_k, gn_g, gn_b):
    length = proj.shape[0]
    nc = _chunks_per_step(length, (5, 4, 2, 1))
    t = nc * CHUNK
    w = RWKV_W
    mix_s = mix[3 * w:][None]
    wup_pad = jnp.zeros((LANES, w), F32).at[:RWKV_RANK_W].set(w_up)
    aup_pad = jnp.zeros((LANES, w), F32).at[RWKV_RANK_W:].set(a_up)
    col = lambda c0: pl.BlockSpec((t, w), lambda c: (c, c0 // w))
    vec = pl.BlockSpec((1, w), lambda c: (0, 0))
    mat = pl.BlockSpec((LANES, w), lambda c: (0, 0))
    wide = lambda: pltpu.VMEM((t, w), F32)
    carry = lambda n: pltpu.VMEM((SUBLANES, n), F32)
    return pl.pallas_call(
        functools.partial(_rwkv_kernel, nc=nc),
        grid=(length // t,),
        in_specs=[col(RW_R0), col(RW_K0), col(RW_V0), pl.BlockSpec((t, 2 * LANES), lambda c: (c, 0)),
                  vec, vec, vec, pl.BlockSpec((1, 2 * LANES), lambda c: (0, 0)),
                  mat, vec, mat, vec, mat, vec, vec, vec, vec, vec],
        out_specs=pl.BlockSpec((t, w), lambda c: (c, 0)),
        out_shape=jax.ShapeDtypeStruct((length, w), BF16),
        scratch_shapes=[pltpu.VMEM((RW_PAIRS, RWKV_DH, LANES), F32),
                        carry(w), carry(w), carry(w), carry(2 * LANES),
                        wide(), wide(), wide(), wide(), wide(), wide(), wide(), wide(), wide()],
        compiler_params=_cparams(("arbitrary",)),
        name="rwkv7",
    )(proj, proj, proj, small, mix[None, :w], mix[None, w:2 * w], mix[None, 2 * w:3 * w], mix_s,
      wup_pad, w0[None], aup_pad, a0[None], g_up, k_k[None], k_a[None], r_k.reshape(1, w),
      gn_g[None], gn_b[None])


def _merge_kernel(oret_ref, ogla_ref, orw_ref, wret_ref, wgla_ref, wrw_ref, gret_ref, ggla_ref, grw_ref, o_ref):
    acc = _sigmoid(gret_ref[...].astype(F32)) * _dot(oret_ref[...], wret_ref[...])
    acc = acc + _sigmoid(ggla_ref[...].astype(F32)) * _dot(ogla_ref[...], wgla_ref[...])
    acc = acc + _sigmoid(grw_ref[...].astype(F32)) * _dot(orw_ref[...], wrw_ref[...])
    o_ref[...] = acc.astype(o_ref.dtype)


def _merge(o_ret, o_gla, o_rw, w_ret, w_gla, w_rw, proj):
    m = o_ret.shape[0]
    tm = _pick(m, (640, 256, 128, 64))
    tn = 512
    act = lambda n: pl.BlockSpec((tm, n), lambda i, j: (i, 0))
    wgt = lambda n: pl.BlockSpec((n, tn), lambda i, j: (0, j))
    gate = lambda b: pl.BlockSpec((tm, tn), lambda i, j: (i, (GATE0 + b * D_MODEL) // tn + j))
    return pl.pallas_call(
        _merge_kernel,
        grid=(m // tm, D_MODEL // tn),
        in_specs=[act(RET_W), act(GLA_WV), act(RWKV_W), wgt(RET_W), wgt(GLA_WV), wgt(RWKV_W),
                  gate(0), gate(1), gate(2)],
        out_specs=pl.BlockSpec((tm, tn), lambda i, j: (i, j)),
        out_shape=jax.ShapeDtypeStruct((m, D_MODEL), BF16),
        compiler_params=_cparams(("parallel", "parallel")),
        name="branch_merge",
    )(o_ret, o_gla, o_rw, w_ret, w_gla, w_rw, proj, proj, proj)


def _router_kernel(h_ref, rw_ref, bias_ref, eidx_ref, ewts_ref):
    tm = h_ref.shape[0]
    logits = _dot_hi(h_ref[...], rw_ref[...])
    s_all = _sigmoid(logits.T[0:N_EXPERTS])
    sel_all = s_all + bias_ref[...]
    ids = lax.broadcasted_iota(jnp.int32, (EXPERTS_PER_GROUP, tm), 0)
    neg = jnp.float32(-jnp.inf)
    best_score = best_e1 = best_e2 = best_w1 = best_w2 = None
    for gidx in range(N_GROUPS):
        rows = slice(gidx * EXPERTS_PER_GROUP, (gidx + 1) * EXPERTS_PER_GROUP)
        sel, s = sel_all[rows], s_all[rows]
        m1 = jnp.max(sel, axis=0, keepdims=True)
        i1 = jnp.min(jnp.where(sel == m1, ids, EXPERTS_PER_GROUP), axis=0, keepdims=True)
        rest = jnp.where(ids == i1, neg, sel)
        m2 = jnp.max(rest, axis=0, keepdims=True)
        i2 = jnp.min(jnp.where(rest == m2, ids, EXPERTS_PER_GROUP), axis=0, keepdims=True)
        score = m1 + m2
        w1 = jnp.sum(jnp.where(ids == i1, s, 0.0), axis=0, keepdims=True)
        w2 = jnp.sum(jnp.where(ids == i2, s, 0.0), axis=0, keepdims=True)
        e1 = i1 + gidx * EXPERTS_PER_GROUP
        e2 = i2 + gidx * EXPERTS_PER_GROUP
        if gidx == 0:
            best_score, best_e1, best_e2, best_w1, best_w2 = score, e1, e2, w1, w2
        else:
            take = score > best_score
            best_score = jnp.where(take, score, best_score)
            best_e1, best_e2 = jnp.where(take, e1, best_e1), jnp.where(take, e2, best_e2)
            best_w1, best_w2 = jnp.where(take, w1, best_w1), jnp.where(take, w2, best_w2)
    total = best_w1 + best_w2
    slot = lax.broadcasted_iota(jnp.int32, (SUBLANES, tm), 0)
    eidx_ref[...] = jnp.where(slot == 0, best_e1, jnp.where(slot == 1, best_e2, 0))
    ewts_ref[...] = jnp.where(slot == 0, best_w1 / total, jnp.where(slot == 1, best_w2 / total, 0.0))


def _router(h, router_w, router_bias):
    m = h.shape[0]
    tm = _pick(m, (256, 128))
    rw_pad = jnp.zeros((D_MODEL, LANES), F32).at[:, :N_EXPERTS].set(router_w)
    return pl.pallas_call(
        _router_kernel,
        grid=(m // tm,),
        in_specs=[pl.BlockSpec((tm, D_MODEL), lambda i: (i, 0)),
                  pl.BlockSpec((D_MODEL, LANES), lambda i: (0, 0)),
                  pl.BlockSpec((N_EXPERTS, 1), lambda i: (0, 0))],
        out_specs=(pl.BlockSpec((SUBLANES, tm), lambda i: (0, i)), pl.BlockSpec((SUBLANES, tm), lambda i: (0, i))),
        out_shape=(jax.ShapeDtypeStruct((SUBLANES, m), jnp.int32), jax.ShapeDtypeStruct((SUBLANES, m), F32)),
        compiler_params=_cparams(("parallel",)),
        name="router",
    )(h, rw_pad, router_bias[:, None])


def _row_copy(src_hbm, row, dst_buf, sem):
    return pltpu.make_async_copy(src_hbm.at[pl.ds(row, 1)], dst_buf, sem)


def _ffn_kernel(te_ref, rt_ref, na_ref, h_hbm, w1_ref, w3_ref, w2_ref, y_ref, xbuf, sem, *, tm):
    del te_ref
    i = pl.program_id(0)
    n_active = na_ref[0]
    slot = i % 2

    def start_tile(tile, dst_slot):
        def body(r, c):
            _row_copy(h_hbm, rt_ref[tile * tm + r], xbuf.at[dst_slot, pl.ds(r, 1)], sem.at[dst_slot]).start()
            return c
        lax.fori_loop(0, tm, body, 0)

    def wait_tile(dst_slot):
        def body(r, c):
            _row_copy(h_hbm, 0, xbuf.at[dst_slot, pl.ds(r, 1)], sem.at[dst_slot]).wait()
            return c
        lax.fori_loop(0, tm, body, 0)

    @pl.when(jnp.logical_and(i == 0, n_active > 0))
    def _():
        start_tile(0, 0)

    @pl.when(i + 1 < n_active)
    def _():
        start_tile(i + 1, 1 - slot)

    @pl.when(i < n_active)
    def _():
        wait_tile(slot)
        x = xbuf[slot].astype(BF16)
        h1 = _dot(x, w1_ref[...])
        h3 = _dot(x, w3_ref[...])
        he = (h1 * _sigmoid(h1)) * h3
        y_ref[...] = _dot(he.astype(BF16), w2_ref[...])

    @pl.when(i >= n_active)
    def _():
        y_ref[...] = jnp.zeros_like(y_ref)


def _expert_ffn(h, tile_expert, row_token, n_active, w1, w3, w2):
    d = h.shape[1]
    tm = MOE_TM
    n_tiles = tile_expert.shape[0]
    wspec = lambda shape: pl.BlockSpec((None,) + shape, lambda i, te, rt, na: (te[i], 0, 0),
                                       pipeline_mode=pl.Buffered(1))
    return pl.pallas_call(
        functools.partial(_ffn_kernel, tm=tm),
        grid_spec=pltpu.PrefetchScalarGridSpec(
            num_scalar_prefetch=3,
            grid=(n_tiles,),
            in_specs=[pl.BlockSpec(memory_space=pl.ANY), wspec((d, D_FF_PAD)), wspec((d, D_FF_PAD)),
                      wspec((D_FF_PAD, d))],
            out_specs=pl.BlockSpec((tm, d), lambda i, te, rt, na: (i, 0)),
            scratch_shapes=[pltpu.VMEM((2, tm, d), F32), pltpu.SemaphoreType.DMA((2,))]),
        out_shape=jax.ShapeDtypeStruct((n_tiles * tm, d), F32),
        compiler_params=_cparams(("arbitrary",)),
        name="expert_ffn",
    )(tile_expert, row_token, n_active, h, w1, w3, w2)


def _combine_ln_kernel(p1_ref, p2_ref, h_ref, y_hbm, w_ref, g_ref, b_ref, o_ref, ob_ref, ybuf, sem, *, tm, n_mask):
    i = pl.program_id(0)
    n = pl.num_programs(0)
    slot = i % 2

    def start_tile(tile, dst_slot):
        def body(r, c):
            _row_copy(y_hbm, p1_ref[tile * tm + r], ybuf.at[dst_slot, 0, pl.ds(r, 1)], sem.at[dst_slot]).start()
            _row_copy(y_hbm, p2_ref[tile * tm + r], ybuf.at[dst_slot, 1, pl.ds(r, 1)], sem.at[dst_slot]).start()
            return c
        lax.fori_loop(0, tm, body, 0)

    def wait_tile(dst_slot):
        def body(r, c):
            _row_copy(y_hbm, 0, ybuf.at[dst_slot, 0, pl.ds(r, 1)], sem.at[dst_slot]).wait()
            _row_copy(y_hbm, 0, ybuf.at[dst_slot, 1, pl.ds(r, 1)], sem.at[dst_slot]).wait()
            return c
        lax.fori_loop(0, tm, body, 0)

    @pl.when(i == 0)
    def _():
        start_tile(0, 0)

    @pl.when(i + 1 < n)
    def _():
        start_tile(i + 1, 1 - slot)

    wait_tile(slot)
    w = w_ref[...]
    ffn = w[:, 0:1] * ybuf[slot, 0] + w[:, 1:2] * ybuf[slot, 1]
    x = h_ref[...] * DN_ALPHA + ffn
    _emit_ln(_layernorm_rows(x, g_ref[...], b_ref[...]), o_ref, ob_ref, n_mask, tm)


def _combine_ln(h, y, pos1, pos2, wts, g, b, n_mask):
    m, d = h.shape
    tm = _pick(m, (256, 128, 64))
    row = pl.BlockSpec((tm, d), lambda i, p1, p2: (i, 0))
    vec = pl.BlockSpec((1, d), lambda i, p1, p2: (0, 0))
    return pl.pallas_call(
        functools.partial(_combine_ln_kernel, tm=tm, n_mask=n_mask),
        grid_spec=pltpu.PrefetchScalarGridSpec(
            num_scalar_prefetch=2,
            grid=(m // tm,),
            in_specs=[row, pl.BlockSpec(memory_space=pl.ANY),
                      pl.BlockSpec((tm, 2), lambda i, p1, p2: (i, 0)), vec, vec],
            out_specs=(row, row),
            scratch_shapes=[pltpu.VMEM((2, 2, tm, d), F32), pltpu.SemaphoreType.DMA((2,))]),
        out_shape=(jax.ShapeDtypeStruct((m, d), F32), jax.ShapeDtypeStruct((m, d), BF16)),
        compiler_params=_cparams(("arbitrary",)),
        name="moe_combine_layernorm",
    )(pos1, pos2, h, y, wts, g[None], b[None])


def _dispatch_plan(eidx, tm):
    m = eidx.shape[1]
    flat_e = jnp.concatenate([eidx[0], eidx[1]])
    tok = jnp.arange(m, dtype=jnp.int32)
    flat_t = jnp.concatenate([tok, tok])
    onehot = (flat_e[:, None] == jnp.arange(N_EXPERTS, dtype=jnp.int32)[None, :]).astype(jnp.int32)
    rank = jnp.take_along_axis(jnp.cumsum(onehot, axis=0) - onehot, flat_e[:, None], axis=1)[:, 0]
    counts = jnp.sum(onehot, axis=0)
    padded = ((counts + tm - 1) // tm) * tm
    ends = jnp.cumsum(padded)
    offs = ends - padded
    pos = (offs[flat_e] + rank).astype(jnp.int32)
    n_tiles = (2 * m) // tm + N_EXPERTS
    row_token = jnp.zeros((n_tiles * tm,), jnp.int32).at[pos].set(flat_t)
    n_active = (ends[-1] // tm).astype(jnp.int32)
    tile_id = jnp.arange(n_tiles, dtype=jnp.int32)
    tile_id = jnp.minimum(tile_id, n_active - 1)
    tile_expert = (jnp.sum(tile_id[:, None] >= (offs // tm)[None, :], axis=1) - 1).astype(jnp.int32)
    return tile_expert, row_token, n_active.reshape(1), pos[:m], pos[m:]


def _moe_ln(h, router_w, router_bias, w1, w3, w2, g, b, n_mask):
    eidx, ewts = _router(h, router_w, router_bias)
    tile_expert, row_token, n_active, pos1, pos2 = _dispatch_plan(eidx, MOE_TM)
    y = _expert_ffn(h, tile_expert, row_token, n_active, w1, w3, w2)
    return _combine_ln(h, y, pos1, pos2, ewts[0:2].T, g, b, n_mask)


def _prep_w_in(w):
    ret = RET_W * 4
    gla0 = ret
    gla_main = 2 * GLA_WK + 2 * GLA_WV
    rw0 = gla0 + gla_main + GLA_RANK
    rw_main = 3 * RWKV_W
    sm0 = rw0 + rw_main
    gates0 = sm0 + RWKV_RANK_W + RWKV_RANK_A + RWKV_RANK_G
    main = jnp.concatenate([w[:, rw0:rw0 + rw_main], w[:, :ret], w[:, gla0:gla0 + gla_main], w[:, gates0:]], axis=1)
    ga = w[:, gla0 + gla_main:rw0]
    small = jnp.concatenate([w[:, sm0:gates0], ga, jnp.zeros((w.shape[0], LANES - GLA_RANK), w.dtype)], axis=1)
    return main.astype(BF16), small.astype(BF16)


def _prep_experts(w1, w3, w2):
    pad = D_FF_PAD - D_FF
    w1 = jnp.pad(w1, ((0, 0), (0, 0), (0, pad))).astype(BF16)
    w3 = jnp.pad(w3, ((0, 0), (0, 0), (0, pad))).astype(BF16)
    w2 = jnp.pad(w2, ((0, 0), (0, pad), (0, 0))).astype(BF16)
    return w1, w3, w2


def _rotary_tables(length):
    half = RET_DK // 2
    pos = jnp.arange(length, dtype=jnp.int32) - N_PAD
    inv = ROPE_BASE ** (-jnp.arange(half, dtype=F32) / half)
    ang = pos.astype(F32)[:, None] * inv[None, :]
    cos, sin = jnp.cos(ang), jnp.sin(ang)
    return jnp.concatenate([cos, cos], axis=1), jnp.concatenate([-sin, sin], axis=1)


def _mixer(hb, cos, sin, w_in, ret_gn_g, ret_gn_b, gla_a_up, gla_a_bias, gla_gn_g, gla_gn_b, rw_mix, rw_w_up,
           rw_w0, rw_a_up, rw_a0, rw_g_up, rw_k_k, rw_k_a, rw_r_k, rw_gn_g, rw_gn_b, w_br_ret, w_br_gla,
           w_br_rw, w_out):
    w_main, w_small = _prep_w_in(w_in)
    proj = _matmul(hb, w_main, BF16, 512)
    small = _matmul(hb, w_small, F32, N_SMALL)
    o_ret = _retention(proj, cos, sin, ret_gn_g, ret_gn_b)
    o_gla = _gla(proj, small, gla_a_up, gla_a_bias, gla_gn_g, gla_gn_b)
    o_rw = _rwkv(proj, small, rw_mix, rw_w_up, rw_w0, rw_a_up, rw_a0, rw_g_up, rw_k_k, rw_k_a, rw_r_k,
                 rw_gn_g, rw_gn_b)
    merged = _merge(o_ret, o_gla, o_rw, w_br_ret.astype(BF16), w_br_gla.astype(BF16), w_br_rw.astype(BF16), proj)
    return _matmul(merged, w_out.astype(BF16), F32, 512)


def kernel(x, meta, ln_in_g, ln_in_b, w_in, ret_gn_g, ret_gn_b, gla_a_up, gla_a_bias, gla_gn_g, gla_gn_b, rw_mix, rw_w_up, rw_w0, rw_a_up, rw_a0, rw_g_up, rw_k_k, rw_k_a, rw_r_k, rw_gn_g, rw_gn_b, w_br_ret, w_br_gla, w_br_rw, w_out, ln1_g, ln1_b, router_w, router_bias, exp_w1, exp_w3, exp_w2, ln2_g, ln2_b):
    batch, seq, d = x.shape
    assert batch == 1 and d == D_MODEL
    real = N_PAD + N_META + seq
    length = -(-real // ROW_TILE) * ROW_TILE
    h = jnp.concatenate([jnp.zeros((N_PAD, d), x.dtype), meta.astype(x.dtype), x[0],
                         jnp.zeros((length - real, d), x.dtype)], axis=0)
    cos, sin = _rotary_tables(length)
    h, hb = _layer_norm(h, None, ln_in_g, ln_in_b, N_PAD)
    for l in range(DEPTH):
        mix = _mixer(hb, cos, sin, w_in[l], ret_gn_g[l], ret_gn_b[l], gla_a_up[l], gla_a_bias[l], gla_gn_g[l],
                     gla_gn_b[l], rw_mix[l], rw_w_up[l], rw_w0[l], rw_a_up[l], rw_a0[l], rw_g_up[l], rw_k_k[l],
                     rw_k_a[l], rw_r_k[l], rw_gn_g[l], rw_gn_b[l], w_br_ret[l], w_br_gla[l], w_br_rw[l], w_out[l])
        h, _ = _layer_norm(h, mix, ln1_g[l], ln1_b[l], 0)
        w1, w3, w2 = _prep_experts(exp_w1[l], exp_w3[l], exp_w2[l])
        h, hb = _moe_ln(h, router_w, router_bias, w1, w3, w2, ln2_g[l], ln2_b[l], N_PAD)
    return h[N_PAD + N_META:real][None]
```

```python
import functools

import jax
import jax.numpy as jnp
from jax import lax
from jax.experimental import pallas as pl
from jax.experimental.pallas import tpu as pltpu

F32 = jnp.float32
BF16 = jnp.bfloat16

D_MODEL = 4096
DEPTH = 2
CHUNK = 64
N_META = 16
N_PAD = CHUNK - N_META

RET_HEADS, RET_DK, RET_DV = 12, 128, 128
RET_W = RET_HEADS * RET_DV
ROPE_BASE = 10000.0

GLA_HEADS, GLA_DK, GLA_DV = 6, 128, 256
GLA_WK = GLA_HEADS * GLA_DK
GLA_WV = GLA_HEADS * GLA_DV
GLA_RANK = 16
GLA_TAU = 16.0

RWKV_HEADS, RWKV_DH = 16, 64
RWKV_W = RWKV_HEADS * RWKV_DH
RWKV_RANK_W, RWKV_RANK_A, RWKV_RANK_G = 64, 64, 128

N_EXPERTS = 32
N_GROUPS = 4
EXPERTS_PER_GROUP = N_EXPERTS // N_GROUPS
D_FF = 704

DN_ALPHA = (2 * DEPTH) ** 0.25
LN_EPS = 1e-5
GN_EPS = 1e-5
RWKV_GN_EPS = 64e-5
NORM_EPS = 1e-12

LANES = 128
SUBLANES = 8
VMEM_LIMIT_BYTES = 56 * 1024 * 1024

RET_Q0, RET_K0, RET_V0, RET_G0 = 0, RET_W, 2 * RET_W, 3 * RET_W
GLA_Q0 = 4 * RET_W
GLA_K0 = GLA_Q0 + GLA_WK
GLA_V0 = GLA_K0 + GLA_WK
GLA_G0 = GLA_V0 + GLA_WV
N_HEAD = GLA_G0 + GLA_WV
W_IN_GA0 = N_HEAD
W_IN_RW0 = W_IN_GA0 + GLA_RANK
W_IN_SM0 = W_IN_RW0 + 3 * RWKV_W
W_IN_GATE0 = W_IN_SM0 + RWKV_RANK_W + RWKV_RANK_A + RWKV_RANK_G
RW_R0, RW_K0, RW_V0 = 0, RWKV_W, 2 * RWKV_W
N_SMALL = 3 * LANES
D_FF_PAD = 768

ROW_TILE = 256
MOE_TM = 256


def _cparams(sem):
    return pltpu.CompilerParams(dimension_semantics=sem, vmem_limit_bytes=VMEM_LIMIT_BYTES)


def _pick(n, cands):
    for c in cands:
        if n % c == 0:
            return c
    raise ValueError(f"no tile in {cands} divides {n}")


def _dot(a, b):
    return jnp.dot(a, b, preferred_element_type=F32)


def _split(x):
    hi = x.astype(BF16)
    return hi, (x - hi.astype(F32)).astype(BF16)


def _split3(x):
    hi = x.astype(BF16)
    rest = x - hi.astype(F32)
    mid = rest.astype(BF16)
    return hi, mid, (rest - mid.astype(F32)).astype(BF16)


def _dot_hi(a, b):
    (ah, al), (bh, bl) = _split(a), _split(b)
    return _dot(jnp.concatenate([ah, ah, al], axis=1), jnp.concatenate([bh, bl, bh], axis=0))


def _dot_sel_rhs(a, sel):
    return _dot(jnp.concatenate(_split3(a), axis=1), jnp.concatenate([sel, sel, sel], axis=0))


def _dot_sel_lhs(sel, b):
    return _dot(jnp.concatenate([sel, sel, sel], axis=1), jnp.concatenate(_split3(b), axis=0))


def _dot_nt(a, b):
    return lax.dot_general(a, b, (((1,), (1,)), ((), ())), preferred_element_type=F32)


def _dot_tn(a, b):
    return lax.dot_general(a, b, (((0,), (0,)), ((), ())), preferred_element_type=F32)


def _sigmoid(x):
    return 1.0 / (1.0 + jnp.exp(-x))


def _softplus(x):
    return jnp.maximum(x, 0.0) + jnp.log(1.0 + jnp.exp(-jnp.abs(x)))


def _mm_kernel(x_ref, w_ref, o_ref):
    o_ref[...] = _dot(x_ref[...], w_ref[...]).astype(o_ref.dtype)


def _matmul(x, w, out_dtype, tn, layer=None, n=None):
    m, k = x.shape
    n = w.shape[-1] if n is None else n
    assert n % tn == 0
    tm = _pick(m, (1280, 640, 256, 128, 64))
    if w.ndim == 3:
        w_spec = pl.BlockSpec((None, k, tn), lambda i, j: (layer, 0, j))
    else:
        w_spec = pl.BlockSpec((k, tn), lambda i, j: (0, j))
    return pl.pallas_call(
        _mm_kernel,
        grid=(m // tm, n // tn),
        in_specs=[pl.BlockSpec((tm, k), lambda i, j: (i, 0)), w_spec],
        out_specs=pl.BlockSpec((tm, tn), lambda i, j: (i, j)),
        out_shape=jax.ShapeDtypeStruct((m, n), out_dtype),
        compiler_params=_cparams(("parallel", "parallel")),
        name="dense_matmul",
    )(x, w)


def _layernorm_rows(x, g, b):
    mu = jnp.mean(x, axis=-1, keepdims=True)
    xc = x - mu
    var = jnp.mean(xc * xc, axis=-1, keepdims=True)
    return xc * lax.rsqrt(var + LN_EPS) * g + b


def _emit_ln(y, o_ref, ob_ref, n_mask, tm):
    o_ref[...] = y
    yb = y.astype(BF16)
    if n_mask:
        row = pl.program_id(0) * tm + lax.broadcasted_iota(jnp.int32, (tm, 1), 0)
        yb = jnp.where(row >= n_mask, yb, jnp.zeros_like(yb))
    ob_ref[...] = yb


def _res_ln_kernel(h_ref, f_ref, g_ref, b_ref, o_ref, ob_ref, *, alpha, n_mask, tm):
    x = h_ref[...] * alpha + f_ref[...]
    _emit_ln(_layernorm_rows(x, g_ref[...], b_ref[...]), o_ref, ob_ref, n_mask, tm)


def _ln_kernel(h_ref, g_ref, b_ref, o_ref, ob_ref, *, n_mask, tm):
    _emit_ln(_layernorm_rows(h_ref[...], g_ref[...], b_ref[...]), o_ref, ob_ref, n_mask, tm)


def _layer_norm(h, f, g, b, n_mask):
    m, d = h.shape
    tm = _pick(m, (256, 128, 64))
    row = pl.BlockSpec((tm, d), lambda i: (i, 0))
    vec = pl.BlockSpec((1, d), lambda i: (0, 0))
    out_shape = (jax.ShapeDtypeStruct((m, d), F32), jax.ShapeDtypeStruct((m, d), BF16))
    if f is None:
        kern = functools.partial(_ln_kernel, n_mask=n_mask, tm=tm)
        args, specs = (h, g[None], b[None]), [row, vec, vec]
    else:
        kern = functools.partial(_res_ln_kernel, alpha=DN_ALPHA, n_mask=n_mask, tm=tm)
        args, specs = (h, f, g[None], b[None]), [row, row, vec, vec]
    return pl.pallas_call(
        kern, grid=(m // tm,), in_specs=specs, out_specs=(row, row), out_shape=out_shape,
        compiler_params=_cparams(("parallel",)), name="res_layernorm",
    )(*args)


def _groupnorm_lanes(o, g, b, eps):
    mu = jnp.mean(o, axis=-1, keepdims=True)
    oc = o - mu
    var = jnp.mean(oc * oc, axis=-1, keepdims=True)
    return oc * lax.rsqrt(var + eps) * g + b


def _chunks_per_step(length, cands):
    return _pick(length // CHUNK, cands)


def _ret_kernel(q_ref, k_ref, v_ref, g_ref, cos_ref, sin_ref, intra_ref, qdec_ref, kdec_ref, cdec_ref,
                gng_ref, gnb_ref, o_ref, state_ref, *, nc):
    @pl.when(pl.program_id(1) == 0)
    def _():
        state_ref[...] = jnp.zeros_like(state_ref)

    cos, sin = cos_ref[...], sin_ref[...]
    q = q_ref[...].astype(F32)
    k = k_ref[...].astype(F32)
    half = RET_DK // 2
    q = (q * cos + pltpu.roll(q, half, 1) * sin) * (RET_DK ** -0.5)
    k = k * cos + pltpu.roll(k, half, 1) * sin
    v = v_ref[...]
    intra, qdec, kdec = intra_ref[0], qdec_ref[0], kdec_ref[0]
    cdec = cdec_ref[0][0:1]
    state = state_ref[...]
    outs = []
    for j in range(nc):
        sl = slice(j * CHUNK, (j + 1) * CHUNK)
        qc, kc, vc = q[sl], k[sl], v[sl]
        s = _dot_nt(qc.astype(BF16), kc.astype(BF16)) * intra
        outs.append(_dot(s.astype(BF16), vc) + _dot((qc * qdec).astype(BF16), state.astype(BF16)))
        state = state * cdec + _dot_tn((kc * kdec).astype(BF16), vc)
    state_ref[...] = state
    y = _groupnorm_lanes(jnp.concatenate(outs, axis=0), gng_ref[...], gnb_ref[...], GN_EPS)
    gate = g_ref[...].astype(F32)
    o_ref[...] = (y * (gate * _sigmoid(gate))).astype(o_ref.dtype)


def _retention(proj, cos, sin, gn_g, gn_b):
    length = proj.shape[0]
    nc = _chunks_per_step(length, (10, 5, 4, 2, 1))
    t = nc * CHUNK
    lg = jnp.log1p(-jnp.exp2(-5.0 - jnp.arange(RET_HEADS, dtype=F32)))
    i = jnp.arange(CHUNK, dtype=F32)
    intra = jnp.exp(lg[:, None, None] * jnp.abs(i[:, None] - i[None, :]))
    qdec = jnp.broadcast_to(jnp.exp(lg[:, None] * (i[None, :] + 1.0))[:, :, None], (RET_HEADS, CHUNK, RET_DK))
    kdec = jnp.broadcast_to(jnp.exp(lg[:, None] * (CHUNK - 1.0 - i[None, :]))[:, :, None], (RET_HEADS, CHUNK, RET_DK))
    cdec = jnp.broadcast_to(jnp.exp(lg * CHUNK)[:, None, None], (RET_HEADS, SUBLANES, RET_DV))

    def col(c0):
        return pl.BlockSpec((t, RET_DK), lambda h, c: (c, c0 // RET_DK + h))

    tab = pl.BlockSpec((t, RET_DK), lambda h, c: (c, 0))
    per_head3 = lambda s1, s2: pl.BlockSpec((1, s1, s2), lambda h, c: (h, 0, 0))
    vec = pl.BlockSpec((1, RET_DV), lambda h, c: (0, h))
    return pl.pallas_call(
        functools.partial(_ret_kernel, nc=nc),
        grid=(RET_HEADS, length // t),
        in_specs=[col(RET_Q0), col(RET_K0), col(RET_V0), col(RET_G0), tab, tab,
                  per_head3(CHUNK, CHUNK), per_head3(CHUNK, RET_DK), per_head3(CHUNK, RET_DK),
                  per_head3(SUBLANES, RET_DV), vec, vec],
        out_specs=pl.BlockSpec((t, RET_DV), lambda h, c: (c, h)),
        out_shape=jax.ShapeDtypeStruct((length, RET_W), BF16),
        scratch_shapes=[pltpu.VMEM((RET_DK, RET_DV), F32)],
        compiler_params=_cparams(("parallel", "arbitrary")),
        name="retention",
    )(proj, proj, proj, proj, cos, sin, intra, qdec, kdec, cdec, gn_g[None], gn_b[None])


def _gla_kernel(q_ref, k_ref, v_ref, g_ref, ga_ref, aup_ref, ab_ref, gng_ref, gnb_ref, o_ref, state_ref, *, nc):
    @pl.when(pl.program_id(1) == 0)
    def _():
        state_ref[...] = jnp.zeros_like(state_ref)

    z = _dot_hi(ga_ref[...], aup_ref[...]) + ab_ref[...]
    log_a = -_softplus(-z) / GLA_TAU
    q = (q_ref[...].astype(F32) * (GLA_DK ** -0.5)).astype(BF16)
    k = k_ref[...].astype(F32)
    v = v_ref[...]
    row = lax.broadcasted_iota(jnp.int32, (CHUNK, CHUNK), 0)
    colm = lax.broadcasted_iota(jnp.int32, (CHUNK, CHUNK), 1)
    later = (colm > row).astype(BF16)
    state = state_ref[...]
    outs = []
    for j in range(nc):
        sl = slice(j * CHUNK, (j + 1) * CHUNK)
        la = log_a[sl]
        to_end = _dot_sel_lhs(later, la)
        dec = jnp.exp(to_end[0:1] + la[0:1])
        kc = (k[sl] * jnp.exp(to_end)).astype(BF16)
        state = state * dec + _dot_tn(v[sl], kc)
        outs.append(_dot_nt(q[sl], state.astype(BF16)))
    state_ref[...] = state
    y = _groupnorm_lanes(jnp.concatenate(outs, axis=0), gng_ref[...], gnb_ref[...], GN_EPS)
    gate = g_ref[...].astype(F32)
    o_ref[...] = (y * (gate * _sigmoid(gate))).astype(o_ref.dtype)


def _gla(proj, small, a_up, a_bias, gn_g, gn_b):
    length = proj.shape[0]
    nc = _chunks_per_step(length, (10, 5, 4, 2, 1))
    t = nc * CHUNK
    a_up_pad = jnp.zeros((LANES, GLA_WK), F32).at[:GLA_RANK].set(a_up)
    kcol = lambda c0: pl.BlockSpec((t, GLA_DK), lambda h, c: (c, c0 // GLA_DK + h))
    vcol = lambda c0: pl.BlockSpec((t, GLA_DV), lambda h, c: (c, c0 // GLA_DV + h))
    return pl.pallas_call(
        functools.partial(_gla_kernel, nc=nc),
        grid=(GLA_HEADS, length // t),
        in_specs=[kcol(GLA_Q0), kcol(GLA_K0), vcol(GLA_V0), vcol(GLA_G0),
                  pl.BlockSpec((t, LANES), lambda h, c: (c, 2)),
                  pl.BlockSpec((LANES, GLA_DK), lambda h, c: (0, h)),
                  pl.BlockSpec((1, GLA_DK), lambda h, c: (0, h)),
                  pl.BlockSpec((1, GLA_DV), lambda h, c: (0, h)),
                  pl.BlockSpec((1, GLA_DV), lambda h, c: (0, h))],
        out_specs=pl.BlockSpec((t, GLA_DV), lambda h, c: (c, h)),
        out_shape=jax.ShapeDtypeStruct((length, GLA_WV), BF16),
        scratch_shapes=[pltpu.VMEM((GLA_DV, GLA_DK), F32)],
        compiler_params=_cparams(("parallel", "arbitrary")),
        name="gla",
    )(proj, proj, proj, proj, small, a_up_pad, a_bias[None], gn_g[None], gn_b[None])


RW_PAIRS = RWKV_HEADS // 2


def _head_sum(x, ones_blk):
    parts = [_dot_sel_rhs(x[:, p * LANES:(p + 1) * LANES], ones_blk) for p in range(RW_PAIRS)]
    return jnp.concatenate(parts, axis=1)


def _bmm(a, b):
    return jnp.einsum('pik,pkj->pij', a, b, preferred_element_type=F32)


def _bmm_nt(a, b):
    return jnp.einsum('pik,pjk->pij', a, b, preferred_element_type=F32)


def _bmm_tn(a, b):
    return jnp.einsum('pki,pkj->pij', a, b, preferred_element_type=F32)


def _bmm3(a, b):
    (ah, al), (bh, bl) = _split(a), _split(b)
    return _bmm(jnp.concatenate([ah, ah, al], axis=2), jnp.concatenate([bh, bl, bh], axis=1))


def _bmm3_tn(a, b):
    (ah, al), (bh, bl) = _split(a), _split(b)
    return _bmm_tn(jnp.concatenate([ah, ah, al], axis=1), jnp.concatenate([bh, bl, bh], axis=1))


def _unit_lower_inverse_minus_eye(m, i0, i1):
    def joins(s):
        return jnp.logical_and(i0 // (2 * s) == i1 // (2 * s),
                               jnp.logical_and((i0 // s) % 2 == 1, (i1 // s) % 2 == 0))

    xp = -jnp.where(joins(1), m, 0.0)
    s = 2
    while s < CHUNK:
        c = jnp.where(joins(s), m, 0.0)
        y = c + _bmm3(xp, c)
        xp = xp - (y + _bmm3(y, xp))
        s *= 2
    return xp


def _rwkv_kernel(r_ref, k_ref, v_ref, sm_ref, mixr_ref, mixk_ref, mixv_ref, mixs_ref, wup_ref, w0_ref,
                 aup_ref, a0_ref, gup_ref, kk_ref, ka_ref, rk_ref, gng_ref, gnb_ref, o_ref,
                 state_ref, cr_ref, ck_ref, cv_ref, cs_ref, a_s, bt_s, kt_s, bp_s, kp_s, v_s, r_s, gam_s,
                 o_s, g_s, bonus_s, y_s, u_s, *, nc):
    t = nc * CHUNK

    @pl.when(pl.program_id(0) == 0)
    def _():
        state_ref[...] = jnp.zeros_like(state_ref)
        cr_ref[...] = jnp.zeros_like(cr_ref)
        ck_ref[...] = jnp.zeros_like(ck_ref)
        cv_ref[...] = jnp.zeros_like(cv_ref)
        cs_ref[...] = jnp.zeros_like(cs_ref)

    rowi = lax.broadcasted_iota(jnp.int32, (t, 1), 0)

    def shifted(x, carry_ref, mix):
        prev = jnp.where(rowi == 0, carry_ref[SUBLANES - 1:SUBLANES], pltpu.roll(x, 1, 0))
        carry_ref[...] = x[t - SUBLANES:t]
        return x + (prev - x) * mix

    r = shifted(r_ref[...].astype(F32), cr_ref, mixr_ref[...])
    k = shifted(k_ref[...].astype(F32), ck_ref, mixk_ref[...])
    v = shifted(v_ref[...].astype(F32), cv_ref, mixv_ref[...])
    sm = shifted(sm_ref[...], cs_ref, mixs_ref[...])
    wa, gl = sm[:, :LANES], sm[:, LANES:]

    i0 = lax.broadcasted_iota(jnp.int32, (LANES, LANES), 0)
    i1 = lax.broadcasted_iota(jnp.int32, (LANES, LANES), 1)
    same_head = (i0 // RWKV_DH) == (i1 // RWKV_DH)
    ones_blk = same_head.astype(BF16)

    w = -_softplus(-(w0_ref[...] + _dot_hi(jnp.tanh(wa), wup_ref[...]))) - 0.5
    a = _sigmoid(a0_ref[...] + _dot_hi(wa, aup_ref[...]))
    g_s[...] = _dot_hi(_sigmoid(gl), gup_ref[...])
    kk = k * kk_ref[...]
    kkn = kk / jnp.maximum(jnp.sqrt(_head_sum(kk * kk, ones_blk)), NORM_EPS)
    km = k * (1.0 + (a - 1.0) * ka_ref[...])
    b = kkn * a
    bonus_s[...] = _head_sum(r * km * rk_ref[...], ones_blk) * v
    v_s[...] = v
    r_s[...] = r

    log_d = -jnp.exp(w)
    c0 = lax.broadcasted_iota(jnp.int32, (CHUNK, CHUNK), 0)
    c1 = lax.broadcasted_iota(jnp.int32, (CHUNK, CHUNK), 1)
    upto = (c1 <= c0).astype(BF16)
    for j in range(nc):
        rows = slice(j * CHUNK, (j + 1) * CHUNK)
        ld = log_d[rows]
        c = _dot_sel_lhs(upto, ld)
        c_end = c[CHUNK - 1:CHUNK]
        grow = jnp.exp(-c)
        to_end = jnp.exp(c_end - c)
        a_s[rows] = kkn[rows] * jnp.exp(c - ld)
        bt_s[rows] = b[rows] * grow
        kt_s[rows] = km[rows] * grow
        bp_s[rows] = b[rows] * to_end
        kp_s[rows] = km[rows] * to_end
        gam_s[j * SUBLANES:(j + 1) * SUBLANES] = jnp.broadcast_to(jnp.exp(c_end), (SUBLANES, RWKV_W))

    lo = lax.broadcasted_iota(jnp.int32, (CHUNK, LANES), 1) < RWKV_DH
    strict_lower = jnp.logical_and(same_head, i1 < i0)

    def by_pair(x):
        return jnp.stack([x[:, p * LANES:(p + 1) * LANES] for p in range(RW_PAIRS)], axis=0)

    def stacked(ref, rows):
        x = by_pair(ref[rows, :])
        return jnp.concatenate([jnp.where(lo, x, 0.0), jnp.where(lo, 0.0, x)], axis=1)

    def chunk_local(j, carry):
        rows = pl.ds(pl.multiple_of(j * CHUNK, CHUNK), CHUNK)
        a2, bp2 = stacked(a_s, rows), stacked(bp_s, rows)
        m_all = _bmm_nt(a2.astype(BF16),
                        jnp.concatenate([stacked(bt_s, rows), stacked(kt_s, rows)], axis=1).astype(BF16))
        m_b = jnp.where(strict_lower, m_all[:, :, :LANES], 0.0)
        m_k = jnp.where(strict_lower, m_all[:, :, LANES:], 0.0)
        t_mi = _unit_lower_inverse_minus_eye(m_b, i0, i1)
        z2 = bp2 + _bmm3_tn(t_mi, bp2)
        yk = _bmm3_tn(jnp.concatenate([a2, m_k], axis=2), z2)
        kpp = stacked(kp_s, rows) - yk[:, LANES:]
        uu = _bmm_tn(stacked(v_s, rows).astype(BF16), kpp.astype(BF16))
        y_s[j] = yk[:, :LANES].astype(BF16)
        u_s[j] = uu[:, :RWKV_DH] + uu[:, RWKV_DH:]
        return carry

    lax.fori_loop(0, nc, chunk_local, 0)

    def chunk_advance(j, carry):
        rows = pl.ds(pl.multiple_of(j * CHUNK, CHUNK), CHUNK)
        gam = by_pair(gam_s[pl.ds(pl.multiple_of(j * SUBLANES, SUBLANES), SUBLANES)])[:, 0:1]
        s = state_ref[...]
        s_hi, s_lo = _split(s)
        y = y_s[j]
        s = s * gam - _bmm(jnp.concatenate([s_hi, s_lo], axis=2), jnp.concatenate([y, y], axis=1)) + u_s[j]
        state_ref[...] = s
        o2 = _bmm_nt(stacked(r_s, rows).astype(BF16), s.astype(BF16))
        o_pair = jnp.concatenate([o2[:, :CHUNK], o2[:, CHUNK:]], axis=2)
        o_s[rows, :] = jnp.concatenate([o_pair[p] for p in range(RW_PAIRS)], axis=1)
        return carry

    lax.fori_loop(0, nc, chunk_advance, 0)

    o = o_s[...]
    mu = _head_sum(o, ones_blk) * (1.0 / RWKV_DH)
    oc = o - mu
    var = _head_sum(oc * oc, ones_blk) * (1.0 / RWKV_DH)
    y = oc * lax.rsqrt(var + RWKV_GN_EPS) * gng_ref[...] + gnb_ref[...]
    o_ref[...] = ((y + bonus_s[...]) * g_s[...]).astype(o_ref.dtype)


def _rwkv(proj, small, mix, w_up, w0, a_up, a0, g_up, k_k, k_a, r_k, gn_g, gn_b):
    length = proj.shape[0]
    nc = _chunks_per_step(length, (5, 4, 2, 1))
    t = nc * CHUNK
    w = RWKV_W
    mix_s = mix[3 * w:][None]
    wup_pad = jnp.zeros((LANES, w), F32).at[:RWKV_RANK_W].set(w_up)
    aup_pad = jnp.zeros((LANES, w), F32).at[RWKV_RANK_W:].set(a_up)
    col = lambda c0: pl.BlockSpec((t, w), lambda c: (c, c0 // w))
    vec = pl.BlockSpec((1, w), lambda c: (0, 0))
    mat = pl.BlockSpec((LANES, w), lambda c: (0, 0))
    wide = lambda: pltpu.VMEM((t, w), F32)
    carry = lambda n: pltpu.VMEM((SUBLANES, n), F32)
    return pl.pallas_call(
        functools.partial(_rwkv_kernel, nc=nc),
        grid=(length // t,),
        in_specs=[col(RW_R0), col(RW_K0), col(RW_V0), pl.BlockSpec((t, 2 * LANES), lambda c: (c, 0)),
                  vec, vec, vec, pl.BlockSpec((1, 2 * LANES), lambda c: (0, 0)),
                  mat, vec, mat, vec, mat, vec, vec, vec, vec, vec],
        out_specs=pl.BlockSpec((t, w), lambda c: (c, 0)),
        out_shape=jax.ShapeDtypeStruct((length, w), BF16),
        scratch_shapes=[pltpu.VMEM((RW_PAIRS, RWKV_DH, LANES), F32),
                        carry(w), carry(w), carry(w), carry(2 * LANES),
                        wide(), wide(), wide(), wide(), wide(), wide(), wide(),
                        pltpu.VMEM((nc * SUBLANES, w), F32), wide(), wide(), wide(),
                        pltpu.VMEM((nc, RW_PAIRS, LANES, LANES), BF16),
                        pltpu.VMEM((nc, RW_PAIRS, RWKV_DH, LANES), F32)],
        compiler_params=_cparams(("arbitrary",)),
        name="rwkv7",
    )(proj, proj, proj, small, mix[None, :w], mix[None, w:2 * w], mix[None, 2 * w:3 * w], mix_s,
      wup_pad, w0[None], aup_pad, a0[None], g_up, k_k[None], k_a[None], r_k.reshape(1, w),
      gn_g[None], gn_b[None])


def _merge_kernel(oret_ref, ogla_ref, orw_ref, wret_ref, wgla_ref, wrw_ref, gret_ref, ggla_ref, grw_ref, o_ref):
    acc = _sigmoid(gret_ref[...].astype(F32)) * _dot(oret_ref[...], wret_ref[...])
    acc = acc + _sigmoid(ggla_ref[...].astype(F32)) * _dot(ogla_ref[...], wgla_ref[...])
    acc = acc + _sigmoid(grw_ref[...].astype(F32)) * _dot(orw_ref[...], wrw_ref[...])
    o_ref[...] = acc.astype(o_ref.dtype)


def _merge(o_ret, o_gla, o_rw, w_ret, w_gla, w_rw, layer, gates):
    m = o_ret.shape[0]
    tm = _pick(m, (640, 256, 128, 64))
    tn = 512
    act = lambda n: pl.BlockSpec((tm, n), lambda i, j: (i, 0))
    wgt = lambda n: pl.BlockSpec((None, n, tn), lambda i, j: (layer, 0, j))
    gate = lambda b: pl.BlockSpec((tm, tn), lambda i, j: (i, b * (D_MODEL // tn) + j))
    return pl.pallas_call(
        _merge_kernel,
        grid=(m // tm, D_MODEL // tn),
        in_specs=[act(RET_W), act(GLA_WV), act(RWKV_W), wgt(RET_W), wgt(GLA_WV), wgt(RWKV_W),
                  gate(0), gate(1), gate(2)],
        out_specs=pl.BlockSpec((tm, tn), lambda i, j: (i, j)),
        out_shape=jax.ShapeDtypeStruct((m, D_MODEL), BF16),
        compiler_params=_cparams(("parallel", "parallel")),
        name="branch_merge",
    )(o_ret, o_gla, o_rw, w_ret, w_gla, w_rw, gates, gates, gates)


def _router_kernel(h_ref, rw_ref, bias_ref, eidx_ref, ewts_ref):
    tm = h_ref.shape[0]
    logits = jnp.dot(h_ref[...], rw_ref[...], preferred_element_type=F32, precision=lax.Precision.HIGHEST)
    s_all = _sigmoid(logits.T[0:N_EXPERTS])
    sel_all = s_all + bias_ref[...]
    ids = lax.broadcasted_iota(jnp.int32, (EXPERTS_PER_GROUP, tm), 0)
    neg = jnp.float32(-jnp.inf)
    best_score = best_e1 = best_e2 = best_w1 = best_w2 = None
    for gidx in range(N_GROUPS):
        rows = slice(gidx * EXPERTS_PER_GROUP, (gidx + 1) * EXPERTS_PER_GROUP)
        sel, s = sel_all[rows], s_all[rows]
        m1 = jnp.max(sel, axis=0, keepdims=True)
        i1 = jnp.min(jnp.where(sel == m1, ids, EXPERTS_PER_GROUP), axis=0, keepdims=True)
        rest = jnp.where(ids == i1, neg, sel)
        m2 = jnp.max(rest, axis=0, keepdims=True)
        i2 = jnp.min(jnp.where(rest == m2, ids, EXPERTS_PER_GROUP), axis=0, keepdims=True)
        score = m1 + m2
        w1 = jnp.sum(jnp.where(ids == i1, s, 0.0), axis=0, keepdims=True)
        w2 = jnp.sum(jnp.where(ids == i2, s, 0.0), axis=0, keepdims=True)
        e1 = i1 + gidx * EXPERTS_PER_GROUP
        e2 = i2 + gidx * EXPERTS_PER_GROUP
        if gidx == 0:
            best_score, best_e1, best_e2, best_w1, best_w2 = score, e1, e2, w1, w2
        else:
            take = score > best_score
            best_score = jnp.where(take, score, best_score)
            best_e1, best_e2 = jnp.where(take, e1, best_e1), jnp.where(take, e2, best_e2)
            best_w1, best_w2 = jnp.where(take, w1, best_w1), jnp.where(take, w2, best_w2)
    total = best_w1 + best_w2
    slot = lax.broadcasted_iota(jnp.int32, (SUBLANES, tm), 0)
    eidx_ref[...] = jnp.where(slot == 0, best_e1, jnp.where(slot == 1, best_e2, 0))
    ewts_ref[...] = jnp.where(slot == 0, best_w1 / total, jnp.where(slot == 1, best_w2 / total, 0.0))


def _router(h, router_w, router_bias):
    m = h.shape[0]
    tm = _pick(m, (256, 128))
    rw_pad = jnp.zeros((D_MODEL, LANES), F32).at[:, :N_EXPERTS].set(router_w)
    return pl.pallas_call(
        _router_kernel,
        grid=(m // tm,),
        in_specs=[pl.BlockSpec((tm, D_MODEL), lambda i: (i, 0)),
                  pl.BlockSpec((D_MODEL, LANES), lambda i: (0, 0)),
                  pl.BlockSpec((N_EXPERTS, 1), lambda i: (0, 0))],
        out_specs=(pl.BlockSpec((SUBLANES, tm), lambda i: (0, i)), pl.BlockSpec((SUBLANES, tm), lambda i: (0, i))),
        out_shape=(jax.ShapeDtypeStruct((SUBLANES, m), jnp.int32), jax.ShapeDtypeStruct((SUBLANES, m), F32)),
        compiler_params=_cparams(("parallel",)),
        name="router",
    )(h, rw_pad, router_bias[:, None])


def _row_copy(src_hbm, row, dst_buf, sem):
    return pltpu.make_async_copy(src_hbm.at[pl.ds(row, 1)], dst_buf, sem)


def _ffn_kernel(te_ref, rt_ref, na_ref, h_hbm, w1_ref, w3_ref, w2_ref, y_ref, xbuf, sem, *, tm):
    del te_ref
    i = pl.program_id(0)
    n_active = na_ref[0]
    slot = i % 2

    def start_tile(tile, dst_slot):
        def body(r, c):
            _row_copy(h_hbm, rt_ref[tile * tm + r], xbuf.at[dst_slot, pl.ds(r, 1)], sem.at[dst_slot]).start()
            return c
        lax.fori_loop(0, tm, body, 0)

    def wait_tile(dst_slot):
        def body(r, c):
            _row_copy(h_hbm, 0, xbuf.at[dst_slot, pl.ds(r, 1)], sem.at[dst_slot]).wait()
            return c
        lax.fori_loop(0, tm, body, 0)

    @pl.when(jnp.logical_and(i == 0, n_active > 0))
    def _():
        start_tile(0, 0)

    @pl.when(i + 1 < n_active)
    def _():
        start_tile(i + 1, 1 - slot)

    @pl.when(i < n_active)
    def _():
        wait_tile(slot)
        x = xbuf[slot].astype(BF16)
        h1 = _dot(x, w1_ref[...])
        h3 = _dot(x, w3_ref[...])
        he = (h1 * _sigmoid(h1)) * h3
        y_ref[...] = _dot(he.astype(BF16), w2_ref[...])

    @pl.when(i >= n_active)
    def _():
        y_ref[...] = jnp.zeros_like(y_ref)


def _expert_ffn(h, tile_expert, row_token, n_active, layer, w1, w3, w2):
    d = h.shape[1]
    tm = MOE_TM
    n_tiles = tile_expert.shape[0]
    wspec = lambda shape: pl.BlockSpec((None, None) + shape, lambda i, te, rt, na: (layer, te[i], 0, 0),
                                       pipeline_mode=pl.Buffered(1))
    return pl.pallas_call(
        functools.partial(_ffn_kernel, tm=tm),
        grid_spec=pltpu.PrefetchScalarGridSpec(
            num_scalar_prefetch=3,
            grid=(n_tiles,),
            in_specs=[pl.BlockSpec(memory_space=pl.ANY), wspec((d, D_FF_PAD)), wspec((d, D_FF_PAD)),
                      wspec((D_FF_PAD, d))],
            out_specs=pl.BlockSpec((tm, d), lambda i, te, rt, na: (i, 0)),
            scratch_shapes=[pltpu.VMEM((2, tm, d), F32), pltpu.SemaphoreType.DMA((2,))]),
        out_shape=jax.ShapeDtypeStruct((n_tiles * tm, d), F32),
        compiler_params=_cparams(("arbitrary",)),
        name="expert_ffn",
    )(tile_expert, row_token, n_active, h, w1, w3, w2)


def _combine_ln_kernel(p1_ref, p2_ref, h_ref, y_hbm, w_ref, g_ref, b_ref, o_ref, ob_ref, ybuf, sem, *, tm, n_mask):
    i = pl.program_id(0)
    n = pl.num_programs(0)
    slot = i % 2

    def start_tile(tile, dst_slot):
        def body(r, c):
            _row_copy(y_hbm, p1_ref[tile * tm + r], ybuf.at[dst_slot, 0, pl.ds(r, 1)], sem.at[dst_slot]).start()
            _row_copy(y_hbm, p2_ref[tile * tm + r], ybuf.at[dst_slot, 1, pl.ds(r, 1)], sem.at[dst_slot]).start()
            return c
        lax.fori_loop(0, tm, body, 0)

    def wait_tile(dst_slot):
        def body(r, c):
            _row_copy(y_hbm, 0, ybuf.at[dst_slot, 0, pl.ds(r, 1)], sem.at[dst_slot]).wait()
            _row_copy(y_hbm, 0, ybuf.at[dst_slot, 1, pl.ds(r, 1)], sem.at[dst_slot]).wait()
            return c
        lax.fori_loop(0, tm, body, 0)

    @pl.when(i == 0)
    def _():
        start_tile(0, 0)

    @pl.when(i + 1 < n)
    def _():
        start_tile(i + 1, 1 - slot)

    wait_tile(slot)
    w = w_ref[...]
    ffn = w[:, 0:1] * ybuf[slot, 0] + w[:, 1:2] * ybuf[slot, 1]
    x = h_ref[...] * DN_ALPHA + ffn
    _emit_ln(_layernorm_rows(x, g_ref[...], b_ref[...]), o_ref, ob_ref, n_mask, tm)


def _combine_ln(h, y, pos1, pos2, wts, g, b, n_mask):
    m, d = h.shape
    tm = _pick(m, (256, 128, 64))
    row = pl.BlockSpec((tm, d), lambda i, p1, p2: (i, 0))
    vec = pl.BlockSpec((1, d), lambda i, p1, p2: (0, 0))
    return pl.pallas_call(
        functools.partial(_combine_ln_kernel, tm=tm, n_mask=n_mask),
        grid_spec=pltpu.PrefetchScalarGridSpec(
            num_scalar_prefetch=2,
            grid=(m // tm,),
            in_specs=[row, pl.BlockSpec(memory_space=pl.ANY),
                      pl.BlockSpec((tm, 2), lambda i, p1, p2: (i, 0)), vec, vec],
            out_specs=(row, row),
            scratch_shapes=[pltpu.VMEM((2, 2, tm, d), F32), pltpu.SemaphoreType.DMA((2,))]),
        out_shape=(jax.ShapeDtypeStruct((m, d), F32), jax.ShapeDtypeStruct((m, d), BF16)),
        compiler_params=_cparams(("arbitrary",)),
        name="moe_combine_layernorm",
    )(pos1, pos2, h, y, wts, g[None], b[None])


def _dispatch_plan(eidx, tm):
    m = eidx.shape[1]
    flat_e = jnp.concatenate([eidx[0], eidx[1]])
    tok = jnp.arange(m, dtype=jnp.int32)
    flat_t = jnp.concatenate([tok, tok])
    onehot = (flat_e[:, None] == jnp.arange(N_EXPERTS, dtype=jnp.int32)[None, :]).astype(jnp.int32)
    rank = jnp.take_along_axis(jnp.cumsum(onehot, axis=0) - onehot, flat_e[:, None], axis=1)[:, 0]
    counts = jnp.sum(onehot, axis=0)
    padded = ((counts + tm - 1) // tm) * tm
    ends = jnp.cumsum(padded)
    offs = ends - padded
    pos = (offs[flat_e] + rank).astype(jnp.int32)
    n_tiles = (2 * m) // tm + N_EXPERTS
    row_token = jnp.zeros((n_tiles * tm,), jnp.int32).at[pos].set(flat_t)
    n_active = (ends[-1] // tm).astype(jnp.int32)
    tile_id = jnp.arange(n_tiles, dtype=jnp.int32)
    tile_id = jnp.minimum(tile_id, n_active - 1)
    tile_expert = (jnp.sum(tile_id[:, None] >= (offs // tm)[None, :], axis=1) - 1).astype(jnp.int32)
    return tile_expert, row_token, n_active.reshape(1), pos[:m], pos[m:]


def _moe_ln(h, router_w, router_bias, layer, w1, w3, w2, g, b, n_mask):
    eidx, ewts = _router(h, router_w, router_bias)
    tile_expert, row_token, n_active, pos1, pos2 = _dispatch_plan(eidx, MOE_TM)
    y = _expert_ffn(h, tile_expert, row_token, n_active, layer, w1, w3, w2)
    return _combine_ln(h, y, pos1, pos2, ewts[0:2].T, g, b, n_mask)


def _cast_pad_kernel(x_ref, o_ref):
    rows, cols = x_ref.shape
    o_ref[...] = jnp.zeros(o_ref.shape, o_ref.dtype)
    o_ref[0:rows, 0:cols] = x_ref[...].astype(o_ref.dtype)


def _cast_pad(w, rows_out, cols_out, block_rows, block_cols):
    n_l, n_e, r, c = w.shape
    in_rows = r if rows_out != r else block_rows
    in_cols = c if cols_out != c else block_cols
    out_rows = rows_out if rows_out != r else block_rows
    out_cols = cols_out if cols_out != c else block_cols
    grid = (n_l * n_e, rows_out // out_rows, cols_out // out_cols)
    idx = lambda e, i, j: (e // n_e, e % n_e, i, j)
    return pl.pallas_call(
        _cast_pad_kernel,
        grid=grid,
        in_specs=[pl.BlockSpec((None, None, in_rows, in_cols), idx)],
        out_specs=pl.BlockSpec((None, None, out_rows, out_cols), idx),
        out_shape=jax.ShapeDtypeStruct((n_l, n_e, rows_out, cols_out), BF16),
        compiler_params=_cparams(("parallel", "parallel", "parallel")),
        name="cast_pad_weights",
    )(w)


def _prep_experts(w1, w3, w2):
    d = w1.shape[2]
    w1 = _cast_pad(w1, d, D_FF_PAD, 1024, D_FF_PAD)
    w3 = _cast_pad(w3, d, D_FF_PAD, 1024, D_FF_PAD)
    w2 = _cast_pad(w2, D_FF_PAD, d, D_FF_PAD, 2048)
    return w1, w3, w2


def _rotary_tables(length):
    half = RET_DK // 2
    pos = jnp.arange(length, dtype=jnp.int32) - N_PAD
    inv = ROPE_BASE ** (-jnp.arange(half, dtype=F32) / half)
    ang = pos.astype(F32)[:, None] * inv[None, :]
    cos, sin = jnp.cos(ang), jnp.sin(ang)
    return jnp.concatenate([cos, cos], axis=1), jnp.concatenate([-sin, sin], axis=1)


def _mixer(hb, cos, sin, layer, w_in, ret_gn_g, ret_gn_b, gla_a_up, gla_a_bias, gla_gn_g, gla_gn_b, rw_mix, rw_w_up,
           rw_w0, rw_a_up, rw_a0, rw_g_up, rw_k_k, rw_k_a, rw_r_k, rw_gn_g, rw_gn_b, w_br_ret, w_br_gla,
           w_br_rw, w_out):
    w_l = w_in[layer]
    w_small = jnp.concatenate([w_l[:, W_IN_SM0:W_IN_GATE0], w_l[:, W_IN_GA0:W_IN_RW0],
                               jnp.zeros((D_MODEL, LANES - GLA_RANK), BF16)], axis=1)
    proj = _matmul(hb, w_in, BF16, 512, layer=layer, n=N_HEAD)
    proj_rw = _matmul(hb, w_l[:, W_IN_RW0:W_IN_SM0], BF16, 512)
    gates = _matmul(hb, w_l[:, W_IN_GATE0:], BF16, 512)
    small = _matmul(hb, w_small, F32, N_SMALL)
    o_ret = _retention(proj, cos, sin, ret_gn_g, ret_gn_b)
    o_gla = _gla(proj, small, gla_a_up, gla_a_bias, gla_gn_g, gla_gn_b)
    o_rw = _rwkv(proj_rw, small, rw_mix, rw_w_up, rw_w0, rw_a_up, rw_a0, rw_g_up, rw_k_k, rw_k_a, rw_r_k,
                 rw_gn_g, rw_gn_b)
    merged = _merge(o_ret, o_gla, o_rw, w_br_ret, w_br_gla, w_br_rw, layer, gates)
    return _matmul(merged, w_out, F32, 512, layer=layer)


def kernel(x, meta, ln_in_g, ln_in_b, w_in, ret_gn_g, ret_gn_b, gla_a_up, gla_a_bias, gla_gn_g, gla_gn_b, rw_mix, rw_w_up, rw_w0, rw_a_up, rw_a0, rw_g_up, rw_k_k, rw_k_a, rw_r_k, rw_gn_g, rw_gn_b, w_br_ret, w_br_gla, w_br_rw, w_out, ln1_g, ln1_b, router_w, router_bias, exp_w1, exp_w3, exp_w2, ln2_g, ln2_b):
    batch, seq, d = x.shape
    assert batch == 1 and d == D_MODEL
    real = N_PAD + N_META + seq
    length = -(-real // ROW_TILE) * ROW_TILE
    h = jnp.concatenate([jnp.zeros((N_PAD, d), x.dtype), meta.astype(x.dtype), x[0],
                         jnp.zeros((length - real, d), x.dtype)], axis=0)
    cos, sin = _rotary_tables(length)
    h, hb = _layer_norm(h, None, ln_in_g, ln_in_b, N_PAD)
    w_in, w_out = w_in.astype(BF16), w_out.astype(BF16)
    w_br_ret, w_br_gla, w_br_rw = w_br_ret.astype(BF16), w_br_gla.astype(BF16), w_br_rw.astype(BF16)
    w1, w3, w2 = _prep_experts(exp_w1, exp_w3, exp_w2)
    for l in range(DEPTH):
        mix = _mixer(hb, cos, sin, l, w_in, ret_gn_g[l], ret_gn_b[l], gla_a_up[l], gla_a_bias[l], gla_gn_g[l],
                     gla_gn_b[l], rw_mix[l], rw_w_up[l], rw_w0[l], rw_a_up[l], rw_a0[l], rw_g_up[l], rw_k_k[l],
                     rw_k_a[l], rw_r_k[l], rw_gn_g[l], rw_gn_b[l], w_br_ret, w_br_gla, w_br_rw, w_out)
        h, _ = _layer_norm(h, mix, ln1_g[l], ln1_b[l], 0)
        h, hb = _moe_ln(h, router_w, router_bias, l, w1, w3, w2, ln2_g[l], ln2_b[l], N_PAD)
    return h[N_PAD + N_META:real][None]
```

```python
import functools

import jax
import jax.numpy as jnp
from jax import lax
from jax.experimental import pallas as pl
from jax.experimental.pallas import tpu as pltpu

F32 = jnp.float32
BF16 = jnp.bfloat16

D_MODEL = 4096
DEPTH = 2
CHUNK = 64
N_META = 16
N_PAD = CHUNK - N_META

RET_HEADS, RET_DK, RET_DV = 12, 128, 128
RET_W = RET_HEADS * RET_DV
ROPE_BASE = 10000.0

GLA_HEADS, GLA_DK, GLA_DV = 6, 128, 256
GLA_WK = GLA_HEADS * GLA_DK
GLA_WV = GLA_HEADS * GLA_DV
GLA_RANK = 16
GLA_TAU = 16.0

RWKV_HEADS, RWKV_DH = 16, 64
RWKV_W = RWKV_HEADS * RWKV_DH
RWKV_RANK_W, RWKV_RANK_A, RWKV_RANK_G = 64, 64, 128

N_EXPERTS = 32
N_GROUPS = 4
EXPERTS_PER_GROUP = N_EXPERTS // N_GROUPS
D_FF = 704

DN_ALPHA = (2 * DEPTH) ** 0.25
LN_EPS = 1e-5
GN_EPS = 1e-5
RWKV_GN_EPS = 64e-5
NORM_EPS = 1e-12

LANES = 128
SUBLANES = 8
VMEM_LIMIT_BYTES = 56 * 1024 * 1024

RET_Q0, RET_K0, RET_V0, RET_G0 = 0, RET_W, 2 * RET_W, 3 * RET_W
GLA_Q0 = 4 * RET_W
GLA_K0 = GLA_Q0 + GLA_WK
GLA_V0 = GLA_K0 + GLA_WK
GLA_G0 = GLA_V0 + GLA_WV
N_HEAD = GLA_G0 + GLA_WV
W_IN_GA0 = N_HEAD
W_IN_RW0 = W_IN_GA0 + GLA_RANK
W_IN_SM0 = W_IN_RW0 + 3 * RWKV_W
W_IN_GATE0 = W_IN_SM0 + RWKV_RANK_W + RWKV_RANK_A + RWKV_RANK_G
RW_R0, RW_K0, RW_V0 = 0, RWKV_W, 2 * RWKV_W
N_SMALL = 3 * LANES
D_FF_PAD = 768

ROW_TILE = 256
MOE_TM = 256
DMA_UNROLL = 8


def _cparams(sem):
    return pltpu.CompilerParams(dimension_semantics=sem, vmem_limit_bytes=VMEM_LIMIT_BYTES)


def _pick(n, cands):
    for c in cands:
        if n % c == 0:
            return c
    raise ValueError(f"no tile in {cands} divides {n}")


def _dot(a, b):
    return jnp.dot(a, b, preferred_element_type=F32)


def _split(x):
    hi = x.astype(BF16)
    return hi, (x - hi.astype(F32)).astype(BF16)


def _split3(x):
    hi = x.astype(BF16)
    rest = x - hi.astype(F32)
    mid = rest.astype(BF16)
    return hi, mid, (rest - mid.astype(F32)).astype(BF16)


def _dot_hi(a, b):
    (ah, al), (bh, bl) = _split(a), _split(b)
    return _dot(jnp.concatenate([ah, ah, al], axis=1), jnp.concatenate([bh, bl, bh], axis=0))


def _dot_sel_rhs(a, sel):
    return _dot(jnp.concatenate(_split3(a), axis=1), jnp.concatenate([sel, sel, sel], axis=0))


def _dot_sel_lhs(sel, b):
    return _dot(jnp.concatenate([sel, sel, sel], axis=1), jnp.concatenate(_split3(b), axis=0))


def _dot_nt(a, b):
    return lax.dot_general(a, b, (((1,), (1,)), ((), ())), preferred_element_type=F32)


def _dot_tn(a, b):
    return lax.dot_general(a, b, (((0,), (0,)), ((), ())), preferred_element_type=F32)


def _sigmoid(x):
    return 1.0 / (1.0 + jnp.exp(-x))


def _softplus(x):
    return jnp.maximum(x, 0.0) + jnp.log(1.0 + jnp.exp(-jnp.abs(x)))


def _mm_kernel(x_ref, w_ref, o_ref):
    o_ref[...] = _dot(x_ref[...], w_ref[...]).astype(o_ref.dtype)


def _matmul(x, w, out_dtype, tn, layer=None, n=None):
    m, k = x.shape
    n = w.shape[-1] if n is None else n
    assert n % tn == 0
    tm = _pick(m, (1280, 640, 256, 128, 64))
    if w.ndim == 3:
        w_spec = pl.BlockSpec((None, k, tn), lambda i, j: (layer, 0, j))
    else:
        w_spec = pl.BlockSpec((k, tn), lambda i, j: (0, j))
    return pl.pallas_call(
        _mm_kernel,
        grid=(m // tm, n // tn),
        in_specs=[pl.BlockSpec((tm, k), lambda i, j: (i, 0)), w_spec],
        out_specs=pl.BlockSpec((tm, tn), lambda i, j: (i, j)),
        out_shape=jax.ShapeDtypeStruct((m, n), out_dtype),
        compiler_params=_cparams(("parallel", "parallel")),
        name="dense_matmul",
    )(x, w)


def _layernorm_rows(x, g, b):
    mu = jnp.mean(x, axis=-1, keepdims=True)
    xc = x - mu
    var = jnp.mean(xc * xc, axis=-1, keepdims=True)
    return xc * lax.rsqrt(var + LN_EPS) * g + b


def _emit_ln(y, o_ref, ob_ref, n_mask, tm):
    o_ref[...] = y
    yb = y.astype(BF16)
    if n_mask:
        row = pl.program_id(0) * tm + lax.broadcasted_iota(jnp.int32, (tm, 1), 0)
        yb = jnp.where(row >= n_mask, yb, jnp.zeros_like(yb))
    ob_ref[...] = yb


def _res_ln_kernel(h_ref, f_ref, g_ref, b_ref, o_ref, ob_ref, *, alpha, n_mask, tm):
    x = h_ref[...] * alpha + f_ref[...]
    _emit_ln(_layernorm_rows(x, g_ref[...], b_ref[...]), o_ref, ob_ref, n_mask, tm)


def _ln_kernel(h_ref, g_ref, b_ref, o_ref, ob_ref, *, n_mask, tm):
    _emit_ln(_layernorm_rows(h_ref[...], g_ref[...], b_ref[...]), o_ref, ob_ref, n_mask, tm)


def _layer_norm(h, f, g, b, n_mask):
    m, d = h.shape
    tm = _pick(m, (256, 128, 64))
    row = pl.BlockSpec((tm, d), lambda i: (i, 0))
    vec = pl.BlockSpec((1, d), lambda i: (0, 0))
    out_shape = (jax.ShapeDtypeStruct((m, d), F32), jax.ShapeDtypeStruct((m, d), BF16))
    if f is None:
        kern = functools.partial(_ln_kernel, n_mask=n_mask, tm=tm)
        args, specs = (h, g[None], b[None]), [row, vec, vec]
    else:
        kern = functools.partial(_res_ln_kernel, alpha=DN_ALPHA, n_mask=n_mask, tm=tm)
        args, specs = (h, f, g[None], b[None]), [row, row, vec, vec]
    return pl.pallas_call(
        kern, grid=(m // tm,), in_specs=specs, out_specs=(row, row), out_shape=out_shape,
        compiler_params=_cparams(("parallel",)), name="res_layernorm",
    )(*args)


def _groupnorm_lanes(o, g, b, eps):
    mu = jnp.mean(o, axis=-1, keepdims=True)
    oc = o - mu
    var = jnp.mean(oc * oc, axis=-1, keepdims=True)
    return oc * lax.rsqrt(var + eps) * g + b


def _chunks_per_step(length, cands):
    return _pick(length // CHUNK, cands)


def _by_head(x, n_heads, width):
    return jnp.stack([x[:, h * width:(h + 1) * width] for h in range(n_heads)], axis=0)


def _heads_to_lanes(x):
    return jnp.concatenate([x[h] for h in range(x.shape[0])], axis=1)


def _ret_kernel(q_ref, k_ref, v_ref, g_ref, cos_ref, sin_ref, intra_ref, qdec_ref, kdec_ref, cdec_ref,
                gng_ref, gnb_ref, o_ref, state_ref, *, nc):
    @pl.when(pl.program_id(0) == 0)
    def _():
        state_ref[...] = jnp.zeros_like(state_ref)

    cos, sin = cos_ref[...], sin_ref[...]
    half = RET_DK // 2

    def rotated(ref):
        x = ref[...].astype(F32)
        return jnp.stack([xh * cos + pltpu.roll(xh, half, 1) * sin
                          for xh in (x[:, h * RET_DK:(h + 1) * RET_DK] for h in range(RET_HEADS))], axis=0)

    q = rotated(q_ref) * (RET_DK ** -0.5)
    k = rotated(k_ref)
    v = _by_head(v_ref[...], RET_HEADS, RET_DV)
    intra, qdec, kdec = intra_ref[...], qdec_ref[...], kdec_ref[...]
    cdec = cdec_ref[...][:, 0:1]
    state = state_ref[...]
    outs = []
    for j in range(nc):
        sl = slice(j * CHUNK, (j + 1) * CHUNK)
        qc, kc, vc = q[:, sl], k[:, sl], v[:, sl]
        s = _bmm_nt(qc.astype(BF16), kc.astype(BF16)) * intra
        outs.append(_bmm(s.astype(BF16), vc) + _bmm((qc * qdec).astype(BF16), state.astype(BF16)))
        state = state * cdec + _bmm_tn((kc * kdec).astype(BF16), vc)
    state_ref[...] = state
    y = _groupnorm_lanes(jnp.concatenate(outs, axis=1), gng_ref[...], gnb_ref[...], GN_EPS)
    gate = _by_head(g_ref[...].astype(F32), RET_HEADS, RET_DV)
    o_ref[...] = _heads_to_lanes(y * (gate * _sigmoid(gate))).astype(o_ref.dtype)


def _retention(proj, cos, sin, gn_g, gn_b):
    length = proj.shape[0]
    nc = _chunks_per_step(length, (5, 4, 2, 1))
    t = nc * CHUNK
    lg = jnp.log1p(-jnp.exp2(-5.0 - jnp.arange(RET_HEADS, dtype=F32)))
    i = jnp.arange(CHUNK, dtype=F32)
    intra = jnp.exp(lg[:, None, None] * jnp.abs(i[:, None] - i[None, :]))
    qdec = jnp.broadcast_to(jnp.exp(lg[:, None] * (i[None, :] + 1.0))[:, :, None], (RET_HEADS, CHUNK, RET_DK))
    kdec = jnp.broadcast_to(jnp.exp(lg[:, None] * (CHUNK - 1.0 - i[None, :]))[:, :, None], (RET_HEADS, CHUNK, RET_DK))
    cdec = jnp.broadcast_to(jnp.exp(lg * CHUNK)[:, None, None], (RET_HEADS, SUBLANES, RET_DV))
    col = lambda c0: pl.BlockSpec((t, RET_W), lambda c: (c, c0 // RET_W))
    tab = pl.BlockSpec((t, RET_DK), lambda c: (c, 0))
    whole = lambda a: pl.BlockSpec(a.shape, lambda c: (0,) * a.ndim)
    gng, gnb = gn_g.reshape(RET_HEADS, 1, RET_DV), gn_b.reshape(RET_HEADS, 1, RET_DV)
    consts = (intra, qdec, kdec, cdec, gng, gnb)
    return pl.pallas_call(
        functools.partial(_ret_kernel, nc=nc),
        grid=(length // t,),
        in_specs=[col(RET_Q0), col(RET_K0), col(RET_V0), col(RET_G0), tab, tab] + [whole(a) for a in consts],
        out_specs=pl.BlockSpec((t, RET_W), lambda c: (c, 0)),
        out_shape=jax.ShapeDtypeStruct((length, RET_W), BF16),
        scratch_shapes=[pltpu.VMEM((RET_HEADS, RET_DK, RET_DV), F32)],
        compiler_params=_cparams(("arbitrary",)),
        name="retention",
    )(proj, proj, proj, proj, cos, sin, *consts)


def _gla_kernel(q_ref, k_ref, v_ref, g_ref, ga_ref, aup_ref, ab_ref, gng_ref, gnb_ref, o_ref, state_ref, *, nc):
    @pl.when(pl.program_id(0) == 0)
    def _():
        state_ref[...] = jnp.zeros_like(state_ref)

    z = _dot_hi(ga_ref[...], aup_ref[...]) + ab_ref[...]
    log_a = -_softplus(-z) / GLA_TAU
    q = _by_head((q_ref[...].astype(F32) * (GLA_DK ** -0.5)).astype(BF16), GLA_HEADS, GLA_DK)
    k = k_ref[...].astype(F32)
    v = _by_head(v_ref[...], GLA_HEADS, GLA_DV)
    row = lax.broadcasted_iota(jnp.int32, (CHUNK, CHUNK), 0)
    colm = lax.broadcasted_iota(jnp.int32, (CHUNK, CHUNK), 1)
    later = (colm > row).astype(BF16)
    state = state_ref[...]
    outs = []
    for j in range(nc):
        sl = slice(j * CHUNK, (j + 1) * CHUNK)
        la = log_a[sl]
        to_end = _dot_sel_lhs(later, la)
        dec = _by_head(jnp.exp(to_end[0:1] + la[0:1]), GLA_HEADS, GLA_DK)
        kc = _by_head((k[sl] * jnp.exp(to_end)).astype(BF16), GLA_HEADS, GLA_DK)
        state = state * dec + _bmm_tn(v[:, sl], kc)
        outs.append(_bmm_nt(q[:, sl], state.astype(BF16)))
    state_ref[...] = state
    y = _groupnorm_lanes(jnp.concatenate(outs, axis=1), gng_ref[...], gnb_ref[...], GN_EPS)
    gate = _by_head(g_ref[...].astype(F32), GLA_HEADS, GLA_DV)
    o_ref[...] = _heads_to_lanes(y * (gate * _sigmoid(gate))).astype(o_ref.dtype)


def _gla(proj, small, a_up, a_bias, gn_g, gn_b):
    length = proj.shape[0]
    nc = _chunks_per_step(length, (5, 4, 2, 1))
    t = nc * CHUNK
    a_up_pad = jnp.zeros((LANES, GLA_WK), F32).at[:GLA_RANK].set(a_up)
    kcol = lambda c0: pl.BlockSpec((t, GLA_WK), lambda c: (c, c0 // GLA_WK))
    vcol = lambda c0: pl.BlockSpec((t, GLA_WV), lambda c: (c, c0 // GLA_WV))
    whole = lambda a: pl.BlockSpec(a.shape, lambda c: (0,) * a.ndim)
    consts = (a_up_pad, a_bias[None], gn_g.reshape(GLA_HEADS, 1, GLA_DV), gn_b.reshape(GLA_HEADS, 1, GLA_DV))
    return pl.pallas_call(
        functools.partial(_gla_kernel, nc=nc),
        grid=(length // t,),
        in_specs=[kcol(GLA_Q0), kcol(GLA_K0), vcol(GLA_V0), vcol(GLA_G0),
                  pl.BlockSpec((t, LANES), lambda c: (c, 2))] + [whole(a) for a in consts],
        out_specs=pl.BlockSpec((t, GLA_WV), lambda c: (c, 0)),
        out_shape=jax.ShapeDtypeStruct((length, GLA_WV), BF16),
        scratch_shapes=[pltpu.VMEM((GLA_HEADS, GLA_DV, GLA_DK), F32)],
        compiler_params=_cparams(("arbitrary",)),
        name="gla",
    )(proj, proj, proj, proj, small, *consts)


RW_PAIRS = RWKV_HEADS // 2


def _head_sum(x, ones_blk):
    parts = [_dot_sel_rhs(x[:, p * LANES:(p + 1) * LANES], ones_blk) for p in range(RW_PAIRS)]
    return jnp.concatenate(parts, axis=1)


def _bmm(a, b):
    return jnp.einsum('pik,pkj->pij', a, b, preferred_element_type=F32)


def _bmm_nt(a, b):
    return jnp.einsum('pik,pjk->pij', a, b, preferred_element_type=F32)


def _bmm_tn(a, b):
    return jnp.einsum('pki,pkj->pij', a, b, preferred_element_type=F32)


def _bmm3(a, b):
    (ah, al), (bh, bl) = _split(a), _split(b)
    return _bmm(jnp.concatenate([ah, ah, al], axis=2), jnp.concatenate([bh, bl, bh], axis=1))


def _bmm3_tn(a, b):
    (ah, al), (bh, bl) = _split(a), _split(b)
    return _bmm_tn(jnp.concatenate([ah, ah, al], axis=1), jnp.concatenate([bh, bl, bh], axis=1))


def _unit_lower_inverse_minus_eye(m, i0, i1):
    def joins(s):
        return jnp.logical_and(i0 // (2 * s) == i1 // (2 * s),
                               jnp.logical_and((i0 // s) % 2 == 1, (i1 // s) % 2 == 0))

    xp = -jnp.where(joins(1), m, 0.0)
    s = 2
    while s < CHUNK:
        c = jnp.where(joins(s), m, 0.0)
        y = c + _bmm3(xp, c)
        xp = xp - (y + _bmm3(y, xp))
        s *= 2
    return xp


def _rwkv_kernel(r_ref, k_ref, v_ref, sm_ref, mixr_ref, mixk_ref, mixv_ref, mixs_ref, wup_ref, w0_ref,
                 aup_ref, a0_ref, gup_ref, kk_ref, ka_ref, rk_ref, gng_ref, gnb_ref, o_ref,
                 state_ref, cr_ref, ck_ref, cv_ref, cs_ref, a_s, bt_s, kt_s, bp_s, kp_s, v_s, r_s, gam_s,
                 o_s, g_s, bonus_s, y_s, u_s, *, nc):
    t = nc * CHUNK

    @pl.when(pl.program_id(0) == 0)
    def _():
        state_ref[...] = jnp.zeros_like(state_ref)
        cr_ref[...] = jnp.zeros_like(cr_ref)
        ck_ref[...] = jnp.zeros_like(ck_ref)
        cv_ref[...] = jnp.zeros_like(cv_ref)
        cs_ref[...] = jnp.zeros_like(cs_ref)

    rowi = lax.broadcasted_iota(jnp.int32, (t, 1), 0)

    def shifted(x, carry_ref, mix):
        prev = jnp.where(rowi == 0, carry_ref[SUBLANES - 1:SUBLANES], pltpu.roll(x, 1, 0))
        carry_ref[...] = x[t - SUBLANES:t]
        return x + (prev - x) * mix

    r = shifted(r_ref[...].astype(F32), cr_ref, mixr_ref[...])
    k = shifted(k_ref[...].astype(F32), ck_ref, mixk_ref[...])
    v = shifted(v_ref[...].astype(F32), cv_ref, mixv_ref[...])
    sm = shifted(sm_ref[...], cs_ref, mixs_ref[...])
    wa, gl = sm[:, :LANES], sm[:, LANES:]

    i0 = lax.broadcasted_iota(jnp.int32, (LANES, LANES), 0)
    i1 = lax.broadcasted_iota(jnp.int32, (LANES, LANES), 1)
    same_head = (i0 // RWKV_DH) == (i1 // RWKV_DH)
    ones_blk = same_head.astype(BF16)

    w = -_softplus(-(w0_ref[...] + _dot_hi(jnp.tanh(wa), wup_ref[...]))) - 0.5
    a = _sigmoid(a0_ref[...] + _dot_hi(wa, aup_ref[...]))
    g_s[...] = _dot_hi(_sigmoid(gl), gup_ref[...])
    kk = k * kk_ref[...]
    kkn = kk / jnp.maximum(jnp.sqrt(_head_sum(kk * kk, ones_blk)), NORM_EPS)
    km = k * (1.0 + (a - 1.0) * ka_ref[...])
    b = kkn * a
    bonus_s[...] = _head_sum(r * km * rk_ref[...], ones_blk) * v
    v_s[...] = v
    r_s[...] = r

    log_d = -jnp.exp(w)
    c0 = lax.broadcasted_iota(jnp.int32, (CHUNK, CHUNK), 0)
    c1 = lax.broadcasted_iota(jnp.int32, (CHUNK, CHUNK), 1)
    upto = (c1 <= c0).astype(BF16)
    for j in range(nc):
        rows = slice(j * CHUNK, (j + 1) * CHUNK)
        ld = log_d[rows]
        c = _dot_sel_lhs(upto, ld)
        c_end = c[CHUNK - 1:CHUNK]
        grow = jnp.exp(-c)
        to_end = jnp.exp(c_end - c)
        a_s[rows] = kkn[rows] * jnp.exp(c - ld)
        bt_s[rows] = b[rows] * grow
        kt_s[rows] = km[rows] * grow
        bp_s[rows] = b[rows] * to_end
        kp_s[rows] = km[rows] * to_end
        gam_s[j * SUBLANES:(j + 1) * SUBLANES] = jnp.broadcast_to(jnp.exp(c_end), (SUBLANES, RWKV_W))

    lo = lax.broadcasted_iota(jnp.int32, (CHUNK, LANES), 1) < RWKV_DH
    strict_lower = jnp.logical_and(same_head, i1 < i0)

    def by_pair(x):
        return jnp.stack([x[:, p * LANES:(p + 1) * LANES] for p in range(RW_PAIRS)], axis=0)

    def stacked(ref, rows):
        x = by_pair(ref[rows, :])
        return jnp.concatenate([jnp.where(lo, x, 0.0), jnp.where(lo, 0.0, x)], axis=1)

    def chunk_local(j, carry):
        rows = pl.ds(pl.multiple_of(j * CHUNK, CHUNK), CHUNK)
        a2, bp2 = stacked(a_s, rows), stacked(bp_s, rows)
        m_all = _bmm_nt(a2.astype(BF16),
                        jnp.concatenate([stacked(bt_s, rows), stacked(kt_s, rows)], axis=1).astype(BF16))
        m_b = jnp.where(strict_lower, m_all[:, :, :LANES], 0.0)
        m_k = jnp.where(strict_lower, m_all[:, :, LANES:], 0.0)
        t_mi = _unit_lower_inverse_minus_eye(m_b, i0, i1)
        z2 = bp2 + _bmm3_tn(t_mi, bp2)
        yk = _bmm3_tn(jnp.concatenate([a2, m_k], axis=2), z2)
        kpp = stacked(kp_s, rows) - yk[:, LANES:]
        uu = _bmm_tn(stacked(v_s, rows).astype(BF16), kpp.astype(BF16))
        y_s[j] = yk[:, :LANES].astype(BF16)
        u_s[j] = uu[:, :RWKV_DH] + uu[:, RWKV_DH:]
        return carry

    lax.fori_loop(0, nc, chunk_local, 0)

    def chunk_advance(j, carry):
        rows = pl.ds(pl.multiple_of(j * CHUNK, CHUNK), CHUNK)
        gam = by_pair(gam_s[pl.ds(pl.multiple_of(j * SUBLANES, SUBLANES), SUBLANES)])[:, 0:1]
        s = state_ref[...]
        s_hi, s_lo = _split(s)
        y = y_s[j]
        s = s * gam - _bmm(jnp.concatenate([s_hi, s_lo], axis=2), jnp.concatenate([y, y], axis=1)) + u_s[j]
        state_ref[...] = s
        o2 = _bmm_nt(stacked(r_s, rows).astype(BF16), s.astype(BF16))
        o_pair = jnp.concatenate([o2[:, :CHUNK], o2[:, CHUNK:]], axis=2)
        o_s[rows, :] = jnp.concatenate([o_pair[p] for p in range(RW_PAIRS)], axis=1)
        return carry

    lax.fori_loop(0, nc, chunk_advance, 0)

    o = o_s[...]
    mu = _head_sum(o, ones_blk) * (1.0 / RWKV_DH)
    oc = o - mu
    var = _head_sum(oc * oc, ones_blk) * (1.0 / RWKV_DH)
    y = oc * lax.rsqrt(var + RWKV_GN_EPS) * gng_ref[...] + gnb_ref[...]
    o_ref[...] = ((y + bonus_s[...]) * g_s[...]).astype(o_ref.dtype)


def _rwkv(proj, small, mix, w_up, w0, a_up, a0, g_up, k_k, k_a, r_k, gn_g, gn_b):
    length = proj.shape[0]
    nc = _chunks_per_step(length, (5, 4, 2, 1))
    t = nc * CHUNK
    w = RWKV_W
    mix_s = mix[3 * w:][None]
    wup_pad = jnp.zeros((LANES, w), F32).at[:RWKV_RANK_W].set(w_up)
    aup_pad = jnp.zeros((LANES, w), F32).at[RWKV_RANK_W:].set(a_up)
    col = lambda c0: pl.BlockSpec((t, w), lambda c: (c, c0 // w))
    vec = pl.BlockSpec((1, w), lambda c: (0, 0))
    mat = pl.BlockSpec((LANES, w), lambda c: (0, 0))
    wide = lambda: pltpu.VMEM((t, w), F32)
    carry = lambda n: pltpu.VMEM((SUBLANES, n), F32)
    return pl.pallas_call(
        functools.partial(_rwkv_kernel, nc=nc),
        grid=(length // t,),
        in_specs=[col(RW_R0), col(RW_K0), col(RW_V0), pl.BlockSpec((t, 2 * LANES), lambda c: (c, 0)),
                  vec, vec, vec, pl.BlockSpec((1, 2 * LANES), lambda c: (0, 0)),
                  mat, vec, mat, vec, mat, vec, vec, vec, vec, vec],
        out_specs=pl.BlockSpec((t, w), lambda c: (c, 0)),
        out_shape=jax.ShapeDtypeStruct((length, w), BF16),
        scratch_shapes=[pltpu.VMEM((RW_PAIRS, RWKV_DH, LANES), F32),
                        carry(w), carry(w), carry(w), carry(2 * LANES),
                        wide(), wide(), wide(), wide(), wide(), wide(), wide(),
                        pltpu.VMEM((nc * SUBLANES, w), F32), wide(), wide(), wide(),
                        pltpu.VMEM((nc, RW_PAIRS, LANES, LANES), BF16),
                        pltpu.VMEM((nc, RW_PAIRS, RWKV_DH, LANES), F32)],
        compiler_params=_cparams(("arbitrary",)),
        name="rwkv7",
    )(proj, proj, proj, small, mix[None, :w], mix[None, w:2 * w], mix[None, 2 * w:3 * w], mix_s,
      wup_pad, w0[None], aup_pad, a0[None], g_up, k_k[None], k_a[None], r_k.reshape(1, w),
      gn_g[None], gn_b[None])


def _merge_kernel(oret_ref, ogla_ref, orw_ref, wret_ref, wgla_ref, wrw_ref, gret_ref, ggla_ref, grw_ref, o_ref):
    acc = _sigmoid(gret_ref[...].astype(F32)) * _dot(oret_ref[...], wret_ref[...])
    acc = acc + _sigmoid(ggla_ref[...].astype(F32)) * _dot(ogla_ref[...], wgla_ref[...])
    acc = acc + _sigmoid(grw_ref[...].astype(F32)) * _dot(orw_ref[...], wrw_ref[...])
    o_ref[...] = acc.astype(o_ref.dtype)


def _merge(o_ret, o_gla, o_rw, w_ret, w_gla, w_rw, layer, gates):
    m = o_ret.shape[0]
    tm = _pick(m, (640, 256, 128, 64))
    tn = 512
    act = lambda n: pl.BlockSpec((tm, n), lambda i, j: (i, 0))
    wgt = lambda n: pl.BlockSpec((None, n, tn), lambda i, j: (layer, 0, j))
    gate = lambda b: pl.BlockSpec((tm, tn), lambda i, j: (i, b * (D_MODEL // tn) + j))
    return pl.pallas_call(
        _merge_kernel,
        grid=(m // tm, D_MODEL // tn),
        in_specs=[act(RET_W), act(GLA_WV), act(RWKV_W), wgt(RET_W), wgt(GLA_WV), wgt(RWKV_W),
                  gate(0), gate(1), gate(2)],
        out_specs=pl.BlockSpec((tm, tn), lambda i, j: (i, j)),
        out_shape=jax.ShapeDtypeStruct((m, D_MODEL), BF16),
        compiler_params=_cparams(("parallel", "parallel")),
        name="branch_merge",
    )(o_ret, o_gla, o_rw, w_ret, w_gla, w_rw, gates, gates, gates)


def _router_kernel(h_ref, rw_ref, bias_ref, eidx_ref, ewts_ref):
    tm = h_ref.shape[0]
    logits = jnp.dot(h_ref[...], rw_ref[...], preferred_element_type=F32, precision=lax.Precision.HIGHEST)
    s_all = _sigmoid(logits.T[0:N_EXPERTS])
    sel_all = s_all + bias_ref[...]
    ids = lax.broadcasted_iota(jnp.int32, (EXPERTS_PER_GROUP, tm), 0)
    neg = jnp.float32(-jnp.inf)
    best_score = best_e1 = best_e2 = best_w1 = best_w2 = None
    for gidx in range(N_GROUPS):
        rows = slice(gidx * EXPERTS_PER_GROUP, (gidx + 1) * EXPERTS_PER_GROUP)
        sel, s = sel_all[rows], s_all[rows]
        m1 = jnp.max(sel, axis=0, keepdims=True)
        i1 = jnp.min(jnp.where(sel == m1, ids, EXPERTS_PER_GROUP), axis=0, keepdims=True)
        rest = jnp.where(ids == i1, neg, sel)
        m2 = jnp.max(rest, axis=0, keepdims=True)
        i2 = jnp.min(jnp.where(rest == m2, ids, EXPERTS_PER_GROUP), axis=0, keepdims=True)
        score = m1 + m2
        w1 = jnp.sum(jnp.where(ids == i1, s, 0.0), axis=0, keepdims=True)
        w2 = jnp.sum(jnp.where(ids == i2, s, 0.0), axis=0, keepdims=True)
        e1 = i1 + gidx * EXPERTS_PER_GROUP
        e2 = i2 + gidx * EXPERTS_PER_GROUP
        if gidx == 0:
            best_score, best_e1, best_e2, best_w1, best_w2 = score, e1, e2, w1, w2
        else:
            take = score > best_score
            best_score = jnp.where(take, score, best_score)
            best_e1, best_e2 = jnp.where(take, e1, best_e1), jnp.where(take, e2, best_e2)
            best_w1, best_w2 = jnp.where(take, w1, best_w1), jnp.where(take, w2, best_w2)
    total = best_w1 + best_w2
    slot = lax.broadcasted_iota(jnp.int32, (SUBLANES, tm), 0)
    eidx_ref[...] = jnp.where(slot == 0, best_e1, jnp.where(slot == 1, best_e2, 0))
    ewts_ref[...] = jnp.where(slot == 0, best_w1 / total, jnp.where(slot == 1, best_w2 / total, 0.0))


def _router(h, router_w, router_bias):
    m = h.shape[0]
    tm = _pick(m, (256, 128))
    rw_pad = jnp.zeros((D_MODEL, LANES), F32).at[:, :N_EXPERTS].set(router_w)
    return pl.pallas_call(
        _router_kernel,
        grid=(m // tm,),
        in_specs=[pl.BlockSpec((tm, D_MODEL), lambda i: (i, 0)),
                  pl.BlockSpec((D_MODEL, LANES), lambda i: (0, 0)),
                  pl.BlockSpec((N_EXPERTS, 1), lambda i: (0, 0))],
        out_specs=(pl.BlockSpec((SUBLANES, tm), lambda i: (0, i)), pl.BlockSpec((SUBLANES, tm), lambda i: (0, i))),
        out_shape=(jax.ShapeDtypeStruct((SUBLANES, m), jnp.int32), jax.ShapeDtypeStruct((SUBLANES, m), F32)),
        compiler_params=_cparams(("parallel",)),
        name="router",
    )(h, rw_pad, router_bias[:, None])


def _row_copy(src_hbm, row, dst_buf, sem):
    return pltpu.make_async_copy(src_hbm.at[pl.ds(row, 1)], dst_buf, sem)


def _ffn_kernel(te_ref, rt_ref, na_ref, h_hbm, w1_ref, w3_ref, w2_ref, y_ref, xbuf, sem, *, tm):
    del te_ref
    i = pl.program_id(0)
    n = pl.num_programs(0)
    n_active = na_ref[0]
    slot = i % 2

    def start_tile(tile, dst_slot):
        for r in range(tm):
            _row_copy(h_hbm, rt_ref[tile * tm + r], xbuf.at[dst_slot, pl.ds(r, 1)], sem.at[dst_slot]).start()

    def wait_tile(dst_slot):
        def body(r, c):
            _row_copy(h_hbm, 0, xbuf.at[dst_slot, pl.ds(r, 1)], sem.at[dst_slot]).wait()
            return c
        lax.fori_loop(0, tm, body, 0, unroll=DMA_UNROLL)

    @pl.when(i == 0)
    def _():
        start_tile(0, 0)

    wait_tile(slot)
    nxt = jnp.minimum(i + 1, n - 1)

    @pl.when(i < n_active)
    def _():
        start_tile(nxt, 1 - slot)
        x = xbuf[slot].astype(BF16)
        h1 = _dot(x, w1_ref[...])
        h3 = _dot(x, w3_ref[...])
        he = (h1 * _sigmoid(h1)) * h3
        y_ref[...] = _dot(he.astype(BF16), w2_ref[...])

    @pl.when(i >= n_active)
    def _():
        start_tile(nxt, 1 - slot)
        y_ref[...] = jnp.zeros_like(y_ref)

    @pl.when(i == n - 1)
    def _():
        wait_tile(1 - slot)


def _expert_ffn(h, tile_expert, row_token, n_active, layer, w1, w3, w2):
    d = h.shape[1]
    tm = MOE_TM
    n_tiles = tile_expert.shape[0]
    wspec = lambda shape: pl.BlockSpec((None, None) + shape, lambda i, te, rt, na: (layer, te[i], 0, 0),
                                       pipeline_mode=pl.Buffered(1))
    return pl.pallas_call(
        functools.partial(_ffn_kernel, tm=tm),
        grid_spec=pltpu.PrefetchScalarGridSpec(
            num_scalar_prefetch=3,
            grid=(n_tiles,),
            in_specs=[pl.BlockSpec(memory_space=pl.ANY), wspec((d, D_FF_PAD)), wspec((d, D_FF_PAD)),
                      wspec((D_FF_PAD, d))],
            out_specs=pl.BlockSpec((tm, d), lambda i, te, rt, na: (i, 0)),
            scratch_shapes=[pltpu.VMEM((2, tm, d), F32), pltpu.SemaphoreType.DMA((2,))]),
        out_shape=jax.ShapeDtypeStruct((n_tiles * tm, d), F32),
        compiler_params=_cparams(("arbitrary",)),
        name="expert_ffn",
    )(tile_expert, row_token, n_active, h, w1, w3, w2)


def _combine_ln_kernel(p1_ref, p2_ref, h_ref, y_hbm, w_ref, g_ref, b_ref, o_ref, ob_ref, ybuf, sem, *, tm, n_mask):
    i = pl.program_id(0)
    n = pl.num_programs(0)
    slot = i % 2

    def start_tile(tile, dst_slot):
        for r in range(tm):
            _row_copy(y_hbm, p1_ref[tile * tm + r], ybuf.at[dst_slot, 0, pl.ds(r, 1)], sem.at[dst_slot]).start()
            _row_copy(y_hbm, p2_ref[tile * tm + r], ybuf.at[dst_slot, 1, pl.ds(r, 1)], sem.at[dst_slot]).start()

    def wait_tile(dst_slot):
        def body(r, c):
            _row_copy(y_hbm, 0, ybuf.at[dst_slot, 0, pl.ds(r, 1)], sem.at[dst_slot]).wait()
            _row_copy(y_hbm, 0, ybuf.at[dst_slot, 1, pl.ds(r, 1)], sem.at[dst_slot]).wait()
            return c
        lax.fori_loop(0, tm, body, 0, unroll=DMA_UNROLL)

    @pl.when(i == 0)
    def _():
        start_tile(0, 0)

    wait_tile(slot)
    start_tile(jnp.minimum(i + 1, n - 1), 1 - slot)
    w = w_ref[...]
    ffn = w[:, 0:1] * ybuf[slot, 0] + w[:, 1:2] * ybuf[slot, 1]
    x = h_ref[...] * DN_ALPHA + ffn
    _emit_ln(_layernorm_rows(x, g_ref[...], b_ref[...]), o_ref, ob_ref, n_mask, tm)

    @pl.when(i == n - 1)
    def _():
        wait_tile(1 - slot)


def _combine_ln(h, y, pos1, pos2, wts, g, b, n_mask):
    m, d = h.shape
    tm = _pick(m, (256, 128, 64))
    row = pl.BlockSpec((tm, d), lambda i, p1, p2: (i, 0))
    vec = pl.BlockSpec((1, d), lambda i, p1, p2: (0, 0))
    return pl.pallas_call(
        functools.partial(_combine_ln_kernel, tm=tm, n_mask=n_mask),
        grid_spec=pltpu.PrefetchScalarGridSpec(
            num_scalar_prefetch=2,
            grid=(m // tm,),
            in_specs=[row, pl.BlockSpec(memory_space=pl.ANY),
                      pl.BlockSpec((tm, 2), lambda i, p1, p2: (i, 0)), vec, vec],
            out_specs=(row, row),
            scratch_shapes=[pltpu.VMEM((2, 2, tm, d), F32), pltpu.SemaphoreType.DMA((2,))]),
        out_shape=(jax.ShapeDtypeStruct((m, d), F32), jax.ShapeDtypeStruct((m, d), BF16)),
        compiler_params=_cparams(("arbitrary",)),
        name="moe_combine_layernorm",
    )(pos1, pos2, h, y, wts, g[None], b[None])


def _dispatch_plan(eidx, tm):
    m = eidx.shape[1]
    flat_e = jnp.concatenate([eidx[0], eidx[1]])
    tok = jnp.arange(m, dtype=jnp.int32)
    flat_t = jnp.concatenate([tok, tok])
    onehot = (flat_e[:, None] == jnp.arange(N_EXPERTS, dtype=jnp.int32)[None, :]).astype(jnp.int32)
    rank = jnp.take_along_axis(jnp.cumsum(onehot, axis=0) - onehot, flat_e[:, None], axis=1)[:, 0]
    counts = jnp.sum(onehot, axis=0)
    padded = ((counts + tm - 1) // tm) * tm
    ends = jnp.cumsum(padded)
    offs = ends - padded
    pos = (offs[flat_e] + rank).astype(jnp.int32)
    n_tiles = (2 * m) // tm + N_EXPERTS
    row_token = jnp.zeros((n_tiles * tm,), jnp.int32).at[pos].set(flat_t)
    n_active = (ends[-1] // tm).astype(jnp.int32)
    tile_id = jnp.arange(n_tiles, dtype=jnp.int32)
    tile_id = jnp.minimum(tile_id, n_active - 1)
    tile_expert = (jnp.sum(tile_id[:, None] >= (offs // tm)[None, :], axis=1) - 1).astype(jnp.int32)
    return tile_expert, row_token, n_active.reshape(1), pos[:m], pos[m:]


def _moe_ln(h, router_w, router_bias, layer, w1, w3, w2, g, b, n_mask):
    eidx, ewts = _router(h, router_w, router_bias)
    tile_expert, row_token, n_active, pos1, pos2 = _dispatch_plan(eidx, MOE_TM)
    y = _expert_ffn(h, tile_expert, row_token, n_active, layer, w1, w3, w2)
    return _combine_ln(h, y, pos1, pos2, ewts[0:2].T, g, b, n_mask)


def _cast_pad_kernel(x_ref, o_ref):
    rows, cols = x_ref.shape
    o_ref[...] = jnp.zeros(o_ref.shape, o_ref.dtype)
    o_ref[0:rows, 0:cols] = x_ref[...].astype(o_ref.dtype)


def _cast_pad(w, rows_out, cols_out, block_rows, block_cols):
    n_l, n_e, r, c = w.shape
    in_rows = r if rows_out != r else block_rows
    in_cols = c if cols_out != c else block_cols
    out_rows = rows_out if rows_out != r else block_rows
    out_cols = cols_out if cols_out != c else block_cols
    grid = (n_l * n_e, rows_out // out_rows, cols_out // out_cols)
    idx = lambda e, i, j: (e // n_e, e % n_e, i, j)
    return pl.pallas_call(
        _cast_pad_kernel,
        grid=grid,
        in_specs=[pl.BlockSpec((None, None, in_rows, in_cols), idx)],
        out_specs=pl.BlockSpec((None, None, out_rows, out_cols), idx),
        out_shape=jax.ShapeDtypeStruct((n_l, n_e, rows_out, cols_out), BF16),
        compiler_params=_cparams(("parallel", "parallel", "parallel")),
        name="cast_pad_weights",
    )(w)


def _cast_pad_transposed_kernel(x_ref, o_ref):
    rows, cols = x_ref.shape
    x = jnp.concatenate([x_ref[...], jnp.zeros((o_ref.shape[1] - rows, cols), x_ref.dtype)], axis=0)
    o_ref[...] = x.T.astype(o_ref.dtype)


def _cast_pad_transposed(wt, cols_out, block):
    n_l, n_e, c, r = wt.shape
    return pl.pallas_call(
        _cast_pad_transposed_kernel,
        grid=(n_l * n_e, r // block),
        in_specs=[pl.BlockSpec((None, None, c, block), lambda e, i: (e // n_e, e % n_e, 0, i))],
        out_specs=pl.BlockSpec((None, None, block, cols_out), lambda e, i: (e // n_e, e % n_e, i, 0)),
        out_shape=jax.ShapeDtypeStruct((n_l, n_e, r, cols_out), BF16),
        compiler_params=_cparams(("parallel", "parallel")),
        name="cast_pad_weights_t",
    )(wt)


def _prep_experts(w1, w3, w2):
    d = w1.shape[2]
    w1 = _cast_pad_transposed(jnp.swapaxes(w1, 2, 3), D_FF_PAD, 1024)
    w3 = _cast_pad_transposed(jnp.swapaxes(w3, 2, 3), D_FF_PAD, 1024)
    w2 = _cast_pad(w2, D_FF_PAD, d, D_FF_PAD, 2048)
    return w1, w3, w2


def _rotary_tables(length):
    half = RET_DK // 2
    pos = jnp.arange(length, dtype=jnp.int32) - N_PAD
    inv = ROPE_BASE ** (-jnp.arange(half, dtype=F32) / half)
    ang = pos.astype(F32)[:, None] * inv[None, :]
    cos, sin = jnp.cos(ang), jnp.sin(ang)
    return jnp.concatenate([cos, cos], axis=1), jnp.concatenate([-sin, sin], axis=1)


def _mixer(hb, cos, sin, layer, w_in, ret_gn_g, ret_gn_b, gla_a_up, gla_a_bias, gla_gn_g, gla_gn_b, rw_mix, rw_w_up,
           rw_w0, rw_a_up, rw_a0, rw_g_up, rw_k_k, rw_k_a, rw_r_k, rw_gn_g, rw_gn_b, w_br_ret, w_br_gla,
           w_br_rw, w_out):
    cols = lambda a, b: lax.slice(w_in, (layer, 0, a), (layer + 1, D_MODEL, b))[0]
    w_small = jnp.concatenate([cols(W_IN_SM0, W_IN_GATE0), cols(W_IN_GA0, W_IN_RW0),
                               jnp.zeros((D_MODEL, LANES - GLA_RANK), BF16)], axis=1)
    proj = _matmul(hb, w_in, BF16, 512, layer=layer, n=N_HEAD)
    proj_rw = _matmul(hb, cols(W_IN_RW0, W_IN_SM0), BF16, 512)
    gates = _matmul(hb, cols(W_IN_GATE0, w_in.shape[2]), BF16, 512)
    small = _matmul(hb, w_small, F32, N_SMALL)
    o_ret = _retention(proj, cos, sin, ret_gn_g, ret_gn_b)
    o_gla = _gla(proj, small, gla_a_up, gla_a_bias, gla_gn_g, gla_gn_b)
    o_rw = _rwkv(proj_rw, small, rw_mix, rw_w_up, rw_w0, rw_a_up, rw_a0, rw_g_up, rw_k_k, rw_k_a, rw_r_k,
                 rw_gn_g, rw_gn_b)
    merged = _merge(o_ret, o_gla, o_rw, w_br_ret, w_br_gla, w_br_rw, layer, gates)
    return _matmul(merged, w_out, F32, 512, layer=layer)


def kernel(x, meta, ln_in_g, ln_in_b, w_in, ret_gn_g, ret_gn_b, gla_a_up, gla_a_bias, gla_gn_g, gla_gn_b, rw_mix, rw_w_up, rw_w0, rw_a_up, rw_a0, rw_g_up, rw_k_k, rw_k_a, rw_r_k, rw_gn_g, rw_gn_b, w_br_ret, w_br_gla, w_br_rw, w_out, ln1_g, ln1_b, router_w, router_bias, exp_w1, exp_w3, exp_w2, ln2_g, ln2_b):
    batch, seq, d = x.shape
    assert batch == 1 and d == D_MODEL
    real = N_PAD + N_META + seq
    length = -(-real // ROW_TILE) * ROW_TILE
    h = jnp.concatenate([jnp.zeros((N_PAD, d), x.dtype), meta.astype(x.dtype), x[0],
                         jnp.zeros((length - real, d), x.dtype)], axis=0)
    cos, sin = _rotary_tables(length)
    h, hb = _layer_norm(h, None, ln_in_g, ln_in_b, N_PAD)
    w_in, w_out = w_in.astype(BF16), w_out.astype(BF16)
    w_br_ret, w_br_gla, w_br_rw = w_br_ret.astype(BF16), w_br_gla.astype(BF16), w_br_rw.astype(BF16)
    w1, w3, w2 = _prep_experts(exp_w1, exp_w3, exp_w2)
    for l in range(DEPTH):
        mix = _mixer(hb, cos, sin, l, w_in, ret_gn_g[l], ret_gn_b[l], gla_a_up[l], gla_a_bias[l], gla_gn_g[l],
                     gla_gn_b[l], rw_mix[l], rw_w_up[l], rw_w0[l], rw_a_up[l], rw_a0[l], rw_g_up[l], rw_k_k[l],
                     rw_k_a[l], rw_r_k[l], rw_gn_g[l], rw_gn_b[l], w_br_ret, w_br_gla, w_br_rw, w_out)
        h, _ = _layer_norm(h, mix, ln1_g[l], ln1_b[l], 0)
        h, hb = _moe_ln(h, router_w, router_bias, l, w1, w3, w2, ln2_g[l], ln2_b[l], N_PAD)
    return h[N_PAD + N_META:real][None]
```

```python
import functools

import jax
import jax.numpy as jnp
from jax import lax
from jax.experimental import pallas as pl
from jax.experimental.pallas import tpu as pltpu

F32 = jnp.float32
BF16 = jnp.bfloat16

D_MODEL = 4096
DEPTH = 2
CHUNK = 64
N_META = 16
N_PAD = CHUNK - N_META

RET_HEADS, RET_DK, RET_DV = 12, 128, 128
RET_W = RET_HEADS * RET_DV
ROPE_BASE = 10000.0

GLA_HEADS, GLA_DK, GLA_DV = 6, 128, 256
GLA_WK = GLA_HEADS * GLA_DK
GLA_WV = GLA_HEADS * GLA_DV
GLA_RANK = 16
GLA_TAU = 16.0

RWKV_HEADS, RWKV_DH = 16, 64
RWKV_W = RWKV_HEADS * RWKV_DH
RWKV_RANK_W, RWKV_RANK_A, RWKV_RANK_G = 64, 64, 128

N_EXPERTS = 32
N_GROUPS = 4
EXPERTS_PER_GROUP = N_EXPERTS // N_GROUPS
D_FF = 704

DN_ALPHA = (2 * DEPTH) ** 0.25
LN_EPS = 1e-5
GN_EPS = 1e-5
RWKV_GN_EPS = 64e-5
NORM_EPS = 1e-12

LANES = 128
SUBLANES = 8
VMEM_LIMIT_BYTES = 56 * 1024 * 1024

RET_Q0, RET_K0, RET_V0, RET_G0 = 0, RET_W, 2 * RET_W, 3 * RET_W
GLA_Q0 = 4 * RET_W
GLA_K0 = GLA_Q0 + GLA_WK
GLA_V0 = GLA_K0 + GLA_WK
GLA_G0 = GLA_V0 + GLA_WV
N_HEAD = GLA_G0 + GLA_WV
W_IN_GA0 = N_HEAD
W_IN_RW0 = W_IN_GA0 + GLA_RANK
W_IN_SM0 = W_IN_RW0 + 3 * RWKV_W
W_IN_GATE0 = W_IN_SM0 + RWKV_RANK_W + RWKV_RANK_A + RWKV_RANK_G
RW_R0, RW_K0, RW_V0 = 0, RWKV_W, 2 * RWKV_W
N_SMALL = 3 * LANES
D_FF_PAD = 768

ROW_TILE = 256
MOE_TM = 256
DMA_UNROLL = 8
DMA_THREADS = 2


def _cparams(sem):
    return pltpu.CompilerParams(dimension_semantics=sem, vmem_limit_bytes=VMEM_LIMIT_BYTES)


def _pick(n, cands):
    for c in cands:
        if n % c == 0:
            return c
    raise ValueError(f"no tile in {cands} divides {n}")


def _dot(a, b):
    return jnp.dot(a, b, preferred_element_type=F32)


def _split(x):
    hi = x.astype(BF16)
    return hi, (x - hi.astype(F32)).astype(BF16)


def _split3(x):
    hi = x.astype(BF16)
    rest = x - hi.astype(F32)
    mid = rest.astype(BF16)
    return hi, mid, (rest - mid.astype(F32)).astype(BF16)


def _dot_hi(a, b):
    (ah, al), (bh, bl) = _split(a), _split(b)
    return _dot(jnp.concatenate([ah, ah, al], axis=1), jnp.concatenate([bh, bl, bh], axis=0))


def _dot_sel_rhs(a, sel):
    return _dot(jnp.concatenate(_split3(a), axis=1), jnp.concatenate([sel, sel, sel], axis=0))


def _dot_sel_lhs(sel, b):
    return _dot(jnp.concatenate([sel, sel, sel], axis=1), jnp.concatenate(_split3(b), axis=0))


def _dot_nt(a, b):
    return lax.dot_general(a, b, (((1,), (1,)), ((), ())), preferred_element_type=F32)


def _dot_tn(a, b):
    return lax.dot_general(a, b, (((0,), (0,)), ((), ())), preferred_element_type=F32)


def _sigmoid(x):
    return 1.0 / (1.0 + jnp.exp(-x))


def _softplus(x):
    return jnp.maximum(x, 0.0) + jnp.log(1.0 + jnp.exp(-jnp.abs(x)))


def _mm_kernel(x_ref, w_ref, o_ref):
    o_ref[...] = _dot(x_ref[...], w_ref[...]).astype(o_ref.dtype)


def _matmul(x, w, out_dtype, tn, layer=None, n=None):
    m, k = x.shape
    n = w.shape[-1] if n is None else n
    assert n % tn == 0
    tm = _pick(m, (1280, 640, 256, 128, 64))
    if w.ndim == 3:
        w_spec = pl.BlockSpec((None, k, tn), lambda i, j: (layer, 0, j))
    else:
        w_spec = pl.BlockSpec((k, tn), lambda i, j: (0, j))
    return pl.pallas_call(
        _mm_kernel,
        grid=(m // tm, n // tn),
        in_specs=[pl.BlockSpec((tm, k), lambda i, j: (i, 0)), w_spec],
        out_specs=pl.BlockSpec((tm, tn), lambda i, j: (i, j)),
        out_shape=jax.ShapeDtypeStruct((m, n), out_dtype),
        compiler_params=_cparams(("parallel", "parallel")),
        name="dense_matmul",
    )(x, w)


def _layernorm_rows(x, g, b):
    mu = jnp.mean(x, axis=-1, keepdims=True)
    xc = x - mu
    var = jnp.mean(xc * xc, axis=-1, keepdims=True)
    return xc * lax.rsqrt(var + LN_EPS) * g + b


HIGH_HALF = 0xFFFF0000


def _pack_halves(y):
    half = y.shape[1] // 2
    lo = pltpu.bitcast(y[:, :half].astype(BF16).astype(F32), jnp.uint32) >> 16
    hi = pltpu.bitcast(y[:, half:].astype(BF16).astype(F32), jnp.uint32) & jnp.uint32(HIGH_HALF)
    return lo | hi


def _unpack_halves(w):
    lo = pltpu.bitcast(w << 16, F32)
    hi = pltpu.bitcast(w & jnp.uint32(HIGH_HALF), F32)
    return jnp.concatenate([lo, hi], axis=1)


def _emit_ln(y, o_ref, ob_ref, n_mask, tm):
    o_ref[...] = y
    if ob_ref.dtype == jnp.uint32:
        ob_ref[...] = _pack_halves(y)
        return
    yb = y.astype(BF16)
    if n_mask:
        row = pl.program_id(0) * tm + lax.broadcasted_iota(jnp.int32, (tm, 1), 0)
        yb = jnp.where(row >= n_mask, yb, jnp.zeros_like(yb))
    ob_ref[...] = yb


def _res_ln_kernel(h_ref, f_ref, g_ref, b_ref, o_ref, ob_ref, *, alpha, n_mask, tm):
    x = h_ref[...] * alpha + f_ref[...]
    _emit_ln(_layernorm_rows(x, g_ref[...], b_ref[...]), o_ref, ob_ref, n_mask, tm)


def _ln_kernel(h_ref, g_ref, b_ref, o_ref, ob_ref, *, n_mask, tm):
    _emit_ln(_layernorm_rows(h_ref[...], g_ref[...], b_ref[...]), o_ref, ob_ref, n_mask, tm)


def _layer_norm(h, f, g, b, n_mask, packed=False):
    m, d = h.shape
    tm = _pick(m, (256, 128, 64))
    row = pl.BlockSpec((tm, d), lambda i: (i, 0))
    vec = pl.BlockSpec((1, d), lambda i: (0, 0))
    if packed:
        narrow, narrow_spec = jax.ShapeDtypeStruct((m, d // 2), jnp.uint32), pl.BlockSpec((tm, d // 2), lambda i: (i, 0))
    else:
        narrow, narrow_spec = jax.ShapeDtypeStruct((m, d), BF16), row
    if f is None:
        kern = functools.partial(_ln_kernel, n_mask=n_mask, tm=tm)
        args, specs = (h, g[None], b[None]), [row, vec, vec]
    else:
        kern = functools.partial(_res_ln_kernel, alpha=DN_ALPHA, n_mask=n_mask, tm=tm)
        args, specs = (h, f, g[None], b[None]), [row, row, vec, vec]
    return pl.pallas_call(
        kern, grid=(m // tm,), in_specs=specs, out_specs=(row, narrow_spec),
        out_shape=(jax.ShapeDtypeStruct((m, d), F32), narrow),
        compiler_params=_cparams(("parallel",)), name="res_layernorm",
    )(*args)


def _groupnorm_lanes(o, g, b, eps):
    mu = jnp.mean(o, axis=-1, keepdims=True)
    oc = o - mu
    var = jnp.mean(oc * oc, axis=-1, keepdims=True)
    return oc * lax.rsqrt(var + eps) * g + b


def _chunks_per_step(length, cands):
    return _pick(length // CHUNK, cands)


def _by_head(x, n_heads, width):
    return jnp.stack([x[:, h * width:(h + 1) * width] for h in range(n_heads)], axis=0)


def _heads_to_lanes(x):
    return jnp.concatenate([x[h] for h in range(x.shape[0])], axis=1)


def _ret_kernel(q_ref, k_ref, v_ref, g_ref, cos_ref, sin_ref, intra_ref, qdec_ref, kdec_ref, cdec_ref,
                gng_ref, gnb_ref, o_ref, state_ref, *, nc):
    @pl.when(pl.program_id(0) == 0)
    def _():
        state_ref[...] = jnp.zeros_like(state_ref)

    cos, sin = cos_ref[...], sin_ref[...]
    half = RET_DK // 2

    def rotated(ref):
        x = ref[...].astype(F32)
        return jnp.stack([xh * cos + pltpu.roll(xh, half, 1) * sin
                          for xh in (x[:, h * RET_DK:(h + 1) * RET_DK] for h in range(RET_HEADS))], axis=0)

    q = rotated(q_ref) * (RET_DK ** -0.5)
    k = rotated(k_ref)
    v = _by_head(v_ref[...], RET_HEADS, RET_DV)
    intra, qdec, kdec = intra_ref[...], qdec_ref[...], kdec_ref[...]
    cdec = cdec_ref[...][:, 0:1]
    state = state_ref[...]
    outs = []
    for j in range(nc):
        sl = slice(j * CHUNK, (j + 1) * CHUNK)
        qc, kc, vc = q[:, sl], k[:, sl], v[:, sl]
        s = _bmm_nt(qc.astype(BF16), kc.astype(BF16)) * intra
        outs.append(_bmm(s.astype(BF16), vc) + _bmm((qc * qdec).astype(BF16), state.astype(BF16)))
        state = state * cdec + _bmm_tn((kc * kdec).astype(BF16), vc)
    state_ref[...] = state
    y = _groupnorm_lanes(jnp.concatenate(outs, axis=1), gng_ref[...], gnb_ref[...], GN_EPS)
    gate = _by_head(g_ref[...].astype(F32), RET_HEADS, RET_DV)
    o_ref[...] = _heads_to_lanes(y * (gate * _sigmoid(gate))).astype(o_ref.dtype)


def _retention(proj, cos, sin, gn_g, gn_b):
    length = proj.shape[0]
    nc = _chunks_per_step(length, (5, 4, 2, 1))
    t = nc * CHUNK
    lg = jnp.log1p(-jnp.exp2(-5.0 - jnp.arange(RET_HEADS, dtype=F32)))
    i = jnp.arange(CHUNK, dtype=F32)
    intra = jnp.exp(lg[:, None, None] * jnp.abs(i[:, None] - i[None, :]))
    qdec = jnp.broadcast_to(jnp.exp(lg[:, None] * (i[None, :] + 1.0))[:, :, None], (RET_HEADS, CHUNK, RET_DK))
    kdec = jnp.broadcast_to(jnp.exp(lg[:, None] * (CHUNK - 1.0 - i[None, :]))[:, :, None], (RET_HEADS, CHUNK, RET_DK))
    cdec = jnp.broadcast_to(jnp.exp(lg * CHUNK)[:, None, None], (RET_HEADS, SUBLANES, RET_DV))
    col = lambda c0: pl.BlockSpec((t, RET_W), lambda c: (c, c0 // RET_W))
    tab = pl.BlockSpec((t, RET_DK), lambda c: (c, 0))
    whole = lambda a: pl.BlockSpec(a.shape, lambda c: (0,) * a.ndim)
    gng, gnb = gn_g.reshape(RET_HEADS, 1, RET_DV), gn_b.reshape(RET_HEADS, 1, RET_DV)
    consts = (intra, qdec, kdec, cdec, gng, gnb)
    return pl.pallas_call(
        functools.partial(_ret_kernel, nc=nc),
        grid=(length // t,),
        in_specs=[col(RET_Q0), col(RET_K0), col(RET_V0), col(RET_G0), tab, tab] + [whole(a) for a in consts],
        out_specs=pl.BlockSpec((t, RET_W), lambda c: (c, 0)),
        out_shape=jax.ShapeDtypeStruct((length, RET_W), BF16),
        scratch_shapes=[pltpu.VMEM((RET_HEADS, RET_DK, RET_DV), F32)],
        compiler_params=_cparams(("arbitrary",)),
        name="retention",
    )(proj, proj, proj, proj, cos, sin, *consts)


def _gla_kernel(q_ref, k_ref, v_ref, g_ref, ga_ref, aup_ref, ab_ref, gng_ref, gnb_ref, o_ref, state_ref, *, nc):
    @pl.when(pl.program_id(0) == 0)
    def _():
        state_ref[...] = jnp.zeros_like(state_ref)

    z = _dot_hi(ga_ref[...], aup_ref[...]) + ab_ref[...]
    log_a = -_softplus(-z) / GLA_TAU
    q = _by_head((q_ref[...].astype(F32) * (GLA_DK ** -0.5)).astype(BF16), GLA_HEADS, GLA_DK)
    k = k_ref[...].astype(F32)
    v = _by_head(v_ref[...], GLA_HEADS, GLA_DV)
    row = lax.broadcasted_iota(jnp.int32, (CHUNK, CHUNK), 0)
    colm = lax.broadcasted_iota(jnp.int32, (CHUNK, CHUNK), 1)
    later = (colm > row).astype(BF16)
    state = state_ref[...]
    outs = []
    for j in range(nc):
        sl = slice(j * CHUNK, (j + 1) * CHUNK)
        la = log_a[sl]
        to_end = _dot_sel_lhs(later, la)
        dec = _by_head(jnp.exp(to_end[0:1] + la[0:1]), GLA_HEADS, GLA_DK)
        kc = _by_head((k[sl] * jnp.exp(to_end)).astype(BF16), GLA_HEADS, GLA_DK)
        state = state * dec + _bmm_tn(v[:, sl], kc)
        outs.append(_bmm_nt(q[:, sl], state.astype(BF16)))
    state_ref[...] = state
    y = _groupnorm_lanes(jnp.concatenate(outs, axis=1), gng_ref[...], gnb_ref[...], GN_EPS)
    gate = _by_head(g_ref[...].astype(F32), GLA_HEADS, GLA_DV)
    o_ref[...] = _heads_to_lanes(y * (gate * _sigmoid(gate))).astype(o_ref.dtype)


def _gla(proj, small, a_up, a_bias, gn_g, gn_b):
    length = proj.shape[0]
    nc = _chunks_per_step(length, (5, 4, 2, 1))
    t = nc * CHUNK
    a_up_pad = jnp.zeros((LANES, GLA_WK), F32).at[:GLA_RANK].set(a_up)
    kcol = lambda c0: pl.BlockSpec((t, GLA_WK), lambda c: (c, c0 // GLA_WK))
    vcol = lambda c0: pl.BlockSpec((t, GLA_WV), lambda c: (c, c0 // GLA_WV))
    whole = lambda a: pl.BlockSpec(a.shape, lambda c: (0,) * a.ndim)
    consts = (a_up_pad, a_bias[None], gn_g.reshape(GLA_HEADS, 1, GLA_DV), gn_b.reshape(GLA_HEADS, 1, GLA_DV))
    return pl.pallas_call(
        functools.partial(_gla_kernel, nc=nc),
        grid=(length // t,),
        in_specs=[kcol(GLA_Q0), kcol(GLA_K0), vcol(GLA_V0), vcol(GLA_G0),
                  pl.BlockSpec((t, LANES), lambda c: (c, 2))] + [whole(a) for a in consts],
        out_specs=pl.BlockSpec((t, GLA_WV), lambda c: (c, 0)),
        out_shape=jax.ShapeDtypeStruct((length, GLA_WV), BF16),
        scratch_shapes=[pltpu.VMEM((GLA_HEADS, GLA_DV, GLA_DK), F32)],
        compiler_params=_cparams(("arbitrary",)),
        name="gla",
    )(proj, proj, proj, proj, small, *consts)


RW_PAIRS = RWKV_HEADS // 2


def _head_sum(x, ones_blk):
    parts = [_dot_sel_rhs(x[:, p * LANES:(p + 1) * LANES], ones_blk) for p in range(RW_PAIRS)]
    return jnp.concatenate(parts, axis=1)


def _bmm(a, b):
    return jnp.einsum('pik,pkj->pij', a, b, preferred_element_type=F32)


def _bmm_nt(a, b):
    return jnp.einsum('pik,pjk->pij', a, b, preferred_element_type=F32)


def _bmm_tn(a, b):
    return jnp.einsum('pki,pkj->pij', a, b, preferred_element_type=F32)


def _bmm3(a, b):
    (ah, al), (bh, bl) = _split(a), _split(b)
    return _bmm(jnp.concatenate([ah, ah, al], axis=2), jnp.concatenate([bh, bl, bh], axis=1))


def _bmm3_tn(a, b):
    (ah, al), (bh, bl) = _split(a), _split(b)
    return _bmm_tn(jnp.concatenate([ah, ah, al], axis=1), jnp.concatenate([bh, bl, bh], axis=1))


def _unit_lower_inverse_minus_eye(m, i0, i1):
    def joins(s):
        return jnp.logical_and(i0 // (2 * s) == i1 // (2 * s),
                               jnp.logical_and((i0 // s) % 2 == 1, (i1 // s) % 2 == 0))

    xp = -jnp.where(joins(1), m, 0.0)
    s = 2
    while s < CHUNK:
        c = jnp.where(joins(s), m, 0.0)
        y = c + _bmm3(xp, c)
        xp = xp - (y + _bmm3(y, xp))
        s *= 2
    return xp


def _rwkv_kernel(r_ref, k_ref, v_ref, sm_ref, mixr_ref, mixk_ref, mixv_ref, mixs_ref, wup_ref, w0_ref,
                 aup_ref, a0_ref, gup_ref, kk_ref, ka_ref, rk_ref, gng_ref, gnb_ref, o_ref,
                 state_ref, cr_ref, ck_ref, cv_ref, cs_ref, a_s, bt_s, kt_s, bp_s, kp_s, v_s, r_s, gam_s,
                 o_s, g_s, bonus_s, y_s, u_s, *, nc):
    t = nc * CHUNK

    @pl.when(pl.program_id(0) == 0)
    def _():
        state_ref[...] = jnp.zeros_like(state_ref)
        cr_ref[...] = jnp.zeros_like(cr_ref)
        ck_ref[...] = jnp.zeros_like(ck_ref)
        cv_ref[...] = jnp.zeros_like(cv_ref)
        cs_ref[...] = jnp.zeros_like(cs_ref)

    rowi = lax.broadcasted_iota(jnp.int32, (t, 1), 0)

    def shifted(x, carry_ref, mix):
        prev = jnp.where(rowi == 0, carry_ref[SUBLANES - 1:SUBLANES], pltpu.roll(x, 1, 0))
        carry_ref[...] = x[t - SUBLANES:t]
        return x + (prev - x) * mix

    r = shifted(r_ref[...].astype(F32), cr_ref, mixr_ref[...])
    k = shifted(k_ref[...].astype(F32), ck_ref, mixk_ref[...])
    v = shifted(v_ref[...].astype(F32), cv_ref, mixv_ref[...])
    sm = shifted(sm_ref[...], cs_ref, mixs_ref[...])
    wa, gl = sm[:, :LANES], sm[:, LANES:]

    i0 = lax.broadcasted_iota(jnp.int32, (LANES, LANES), 0)
    i1 = lax.broadcasted_iota(jnp.int32, (LANES, LANES), 1)
    same_head = (i0 // RWKV_DH) == (i1 // RWKV_DH)
    ones_blk = same_head.astype(BF16)

    w = -_softplus(-(w0_ref[...] + _dot_hi(jnp.tanh(wa), wup_ref[...]))) - 0.5
    a = _sigmoid(a0_ref[...] + _dot_hi(wa, aup_ref[...]))
    g_s[...] = _dot_hi(_sigmoid(gl), gup_ref[...])
    kk = k * kk_ref[...]
    kkn = kk / jnp.maximum(jnp.sqrt(_head_sum(kk * kk, ones_blk)), NORM_EPS)
    km = k * (1.0 + (a - 1.0) * ka_ref[...])
    b = kkn * a
    bonus_s[...] = _head_sum(r * km * rk_ref[...], ones_blk) * v
    v_s[...] = v
    r_s[...] = r

    log_d = -jnp.exp(w)
    c0 = lax.broadcasted_iota(jnp.int32, (CHUNK, CHUNK), 0)
    c1 = lax.broadcasted_iota(jnp.int32, (CHUNK, CHUNK), 1)
    upto = (c1 <= c0).astype(BF16)
    for j in range(nc):
        rows = slice(j * CHUNK, (j + 1) * CHUNK)
        ld = log_d[rows]
        c = _dot_sel_lhs(upto, ld)
        c_end = c[CHUNK - 1:CHUNK]
        grow = jnp.exp(-c)
        to_end = jnp.exp(c_end - c)
        a_s[rows] = kkn[rows] * jnp.exp(c - ld)
        bt_s[rows] = b[rows] * grow
        kt_s[rows] = km[rows] * grow
        bp_s[rows] = b[rows] * to_end
        kp_s[rows] = km[rows] * to_end
        gam_s[j * SUBLANES:(j + 1) * SUBLANES] = jnp.broadcast_to(jnp.exp(c_end), (SUBLANES, RWKV_W))

    lo = lax.broadcasted_iota(jnp.int32, (CHUNK, LANES), 1) < RWKV_DH
    strict_lower = jnp.logical_and(same_head, i1 < i0)

    def by_pair(x):
        return jnp.stack([x[:, p * LANES:(p + 1) * LANES] for p in range(RW_PAIRS)], axis=0)

    def stacked(ref, rows):
        x = by_pair(ref[rows, :])
        return jnp.concatenate([jnp.where(lo, x, 0.0), jnp.where(lo, 0.0, x)], axis=1)

    def chunk_local(j, carry):
        rows = pl.ds(pl.multiple_of(j * CHUNK, CHUNK), CHUNK)
        a2, bp2 = stacked(a_s, rows), stacked(bp_s, rows)
        m_all = _bmm_nt(a2.astype(BF16),
                        jnp.concatenate([stacked(bt_s, rows), stacked(kt_s, rows)], axis=1).astype(BF16))
        m_b = jnp.where(strict_lower, m_all[:, :, :LANES], 0.0)
        m_k = jnp.where(strict_lower, m_all[:, :, LANES:], 0.0)
        t_mi = _unit_lower_inverse_minus_eye(m_b, i0, i1)
        z2 = bp2 + _bmm3_tn(t_mi, bp2)
        yk = _bmm3_tn(jnp.concatenate([a2, m_k], axis=2), z2)
        kpp = stacked(kp_s, rows) - yk[:, LANES:]
        uu = _bmm_tn(stacked(v_s, rows).astype(BF16), kpp.astype(BF16))
        y_s[j] = yk[:, :LANES].astype(BF16)
        u_s[j] = uu[:, :RWKV_DH] + uu[:, RWKV_DH:]
        return carry

    lax.fori_loop(0, nc, chunk_local, 0)

    def chunk_advance(j, carry):
        rows = pl.ds(pl.multiple_of(j * CHUNK, CHUNK), CHUNK)
        gam = by_pair(gam_s[pl.ds(pl.multiple_of(j * SUBLANES, SUBLANES), SUBLANES)])[:, 0:1]
        s = state_ref[...]
        s_hi, s_lo = _split(s)
        y = y_s[j]
        s = s * gam - _bmm(jnp.concatenate([s_hi, s_lo], axis=2), jnp.concatenate([y, y], axis=1)) + u_s[j]
        state_ref[...] = s
        o2 = _bmm_nt(stacked(r_s, rows).astype(BF16), s.astype(BF16))
        o_pair = jnp.concatenate([o2[:, :CHUNK], o2[:, CHUNK:]], axis=2)
        o_s[rows, :] = jnp.concatenate([o_pair[p] for p in range(RW_PAIRS)], axis=1)
        return carry

    lax.fori_loop(0, nc, chunk_advance, 0)

    o = o_s[...]
    mu = _head_sum(o, ones_blk) * (1.0 / RWKV_DH)
    oc = o - mu
    var = _head_sum(oc * oc, ones_blk) * (1.0 / RWKV_DH)
    y = oc * lax.rsqrt(var + RWKV_GN_EPS) * gng_ref[...] + gnb_ref[...]
    o_ref[...] = ((y + bonus_s[...]) * g_s[...]).astype(o_ref.dtype)


def _rwkv(proj, small, mix, w_up, w0, a_up, a0, g_up, k_k, k_a, r_k, gn_g, gn_b):
    length = proj.shape[0]
    nc = _chunks_per_step(length, (5, 4, 2, 1))
    t = nc * CHUNK
    w = RWKV_W
    mix_s = mix[3 * w:][None]
    wup_pad = jnp.zeros((LANES, w), F32).at[:RWKV_RANK_W].set(w_up)
    aup_pad = jnp.zeros((LANES, w), F32).at[RWKV_RANK_W:].set(a_up)
    col = lambda c0: pl.BlockSpec((t, w), lambda c: (c, c0 // w))
    vec = pl.BlockSpec((1, w), lambda c: (0, 0))
    mat = pl.BlockSpec((LANES, w), lambda c: (0, 0))
    wide = lambda: pltpu.VMEM((t, w), F32)
    carry = lambda n: pltpu.VMEM((SUBLANES, n), F32)
    return pl.pallas_call(
        functools.partial(_rwkv_kernel, nc=nc),
        grid=(length // t,),
        in_specs=[col(RW_R0), col(RW_K0), col(RW_V0), pl.BlockSpec((t, 2 * LANES), lambda c: (c, 0)),
                  vec, vec, vec, pl.BlockSpec((1, 2 * LANES), lambda c: (0, 0)),
                  mat, vec, mat, vec, mat, vec, vec, vec, vec, vec],
        out_specs=pl.BlockSpec((t, w), lambda c: (c, 0)),
        out_shape=jax.ShapeDtypeStruct((length, w), BF16),
        scratch_shapes=[pltpu.VMEM((RW_PAIRS, RWKV_DH, LANES), F32),
                        carry(w), carry(w), carry(w), carry(2 * LANES),
                        wide(), wide(), wide(), wide(), wide(), wide(), wide(),
                        pltpu.VMEM((nc * SUBLANES, w), F32), wide(), wide(), wide(),
                        pltpu.VMEM((nc, RW_PAIRS, LANES, LANES), BF16),
                        pltpu.VMEM((nc, RW_PAIRS, RWKV_DH, LANES), F32)],
        compiler_params=_cparams(("arbitrary",)),
        name="rwkv7",
    )(proj, proj, proj, small, mix[None, :w], mix[None, w:2 * w], mix[None, 2 * w:3 * w], mix_s,
      wup_pad, w0[None], aup_pad, a0[None], g_up, k_k[None], k_a[None], r_k.reshape(1, w),
      gn_g[None], gn_b[None])


def _merge_kernel(oret_ref, ogla_ref, orw_ref, wret_ref, wgla_ref, wrw_ref, gret_ref, ggla_ref, grw_ref, o_ref):
    acc = _sigmoid(gret_ref[...].astype(F32)) * _dot(oret_ref[...], wret_ref[...])
    acc = acc + _sigmoid(ggla_ref[...].astype(F32)) * _dot(ogla_ref[...], wgla_ref[...])
    acc = acc + _sigmoid(grw_ref[...].astype(F32)) * _dot(orw_ref[...], wrw_ref[...])
    o_ref[...] = acc.astype(o_ref.dtype)


def _merge(o_ret, o_gla, o_rw, w_ret, w_gla, w_rw, layer, gates):
    m = o_ret.shape[0]
    tm = _pick(m, (640, 256, 128, 64))
    tn = 512
    act = lambda n: pl.BlockSpec((tm, n), lambda i, j: (i, 0))
    wgt = lambda n: pl.BlockSpec((None, n, tn), lambda i, j: (layer, 0, j))
    gate = lambda b: pl.BlockSpec((tm, tn), lambda i, j: (i, b * (D_MODEL // tn) + j))
    return pl.pallas_call(
        _merge_kernel,
        grid=(m // tm, D_MODEL // tn),
        in_specs=[act(RET_W), act(GLA_WV), act(RWKV_W), wgt(RET_W), wgt(GLA_WV), wgt(RWKV_W),
                  gate(0), gate(1), gate(2)],
        out_specs=pl.BlockSpec((tm, tn), lambda i, j: (i, j)),
        out_shape=jax.ShapeDtypeStruct((m, D_MODEL), BF16),
        compiler_params=_cparams(("parallel", "parallel")),
        name="branch_merge",
    )(o_ret, o_gla, o_rw, w_ret, w_gla, w_rw, gates, gates, gates)


def _router_kernel(h_ref, rw_ref, bias_ref, eidx_ref, ewts_ref):
    tm = h_ref.shape[0]
    logits = jnp.dot(h_ref[...], rw_ref[...], preferred_element_type=F32, precision=lax.Precision.HIGHEST)
    s_all = _sigmoid(logits.T[0:N_EXPERTS])
    sel_all = s_all + bias_ref[...]
    ids = lax.broadcasted_iota(jnp.int32, (EXPERTS_PER_GROUP, tm), 0)
    neg = jnp.float32(-jnp.inf)
    best_score = best_e1 = best_e2 = best_w1 = best_w2 = None
    for gidx in range(N_GROUPS):
        rows = slice(gidx * EXPERTS_PER_GROUP, (gidx + 1) * EXPERTS_PER_GROUP)
        sel, s = sel_all[rows], s_all[rows]
        m1 = jnp.max(sel, axis=0, keepdims=True)
        i1 = jnp.min(jnp.where(sel == m1, ids, EXPERTS_PER_GROUP), axis=0, keepdims=True)
        rest = jnp.where(ids == i1, neg, sel)
        m2 = jnp.max(rest, axis=0, keepdims=True)
        i2 = jnp.min(jnp.where(rest == m2, ids, EXPERTS_PER_GROUP), axis=0, keepdims=True)
        score = m1 + m2
        w1 = jnp.sum(jnp.where(ids == i1, s, 0.0), axis=0, keepdims=True)
        w2 = jnp.sum(jnp.where(ids == i2, s, 0.0), axis=0, keepdims=True)
        e1 = i1 + gidx * EXPERTS_PER_GROUP
        e2 = i2 + gidx * EXPERTS_PER_GROUP
        if gidx == 0:
            best_score, best_e1, best_e2, best_w1, best_w2 = score, e1, e2, w1, w2
        else:
            take = score > best_score
            best_score = jnp.where(take, score, best_score)
            best_e1, best_e2 = jnp.where(take, e1, best_e1), jnp.where(take, e2, best_e2)
            best_w1, best_w2 = jnp.where(take, w1, best_w1), jnp.where(take, w2, best_w2)
    total = best_w1 + best_w2
    slot = lax.broadcasted_iota(jnp.int32, (SUBLANES, tm), 0)
    eidx_ref[...] = jnp.where(slot == 0, best_e1, jnp.where(slot == 1, best_e2, 0))
    ewts_ref[...] = jnp.where(slot == 0, best_w1 / total, jnp.where(slot == 1, best_w2 / total, 0.0))


def _router(h, router_w, router_bias):
    m = h.shape[0]
    tm = _pick(m, (256, 128))
    rw_pad = jnp.zeros((D_MODEL, LANES), F32).at[:, :N_EXPERTS].set(router_w)
    return pl.pallas_call(
        _router_kernel,
        grid=(m // tm,),
        in_specs=[pl.BlockSpec((tm, D_MODEL), lambda i: (i, 0)),
                  pl.BlockSpec((D_MODEL, LANES), lambda i: (0, 0)),
                  pl.BlockSpec((N_EXPERTS, 1), lambda i: (0, 0))],
        out_specs=(pl.BlockSpec((SUBLANES, tm), lambda i: (0, i)), pl.BlockSpec((SUBLANES, tm), lambda i: (0, i))),
        out_shape=(jax.ShapeDtypeStruct((SUBLANES, m), jnp.int32), jax.ShapeDtypeStruct((SUBLANES, m), F32)),
        compiler_params=_cparams(("parallel",)),
        name="router",
    )(h, rw_pad, router_bias[:, None])


def _row_copy(src_hbm, row, dst_buf, sem):
    return pltpu.make_async_copy(src_hbm.at[pl.ds(row, 1)], dst_buf, sem)


def _ffn_kernel(te_ref, rt_ref, na_ref, h_hbm, w1_ref, w3_ref, w2_ref, y_ref, xbuf, sem, *, tm):
    del te_ref
    i = pl.program_id(0)
    n = pl.num_programs(0)
    n_active = na_ref[0]
    slot = i % 2

    def start_tile(tile, dst_slot):
        for r in range(tm):
            _row_copy(h_hbm, rt_ref[tile * tm + r], xbuf.at[dst_slot, pl.ds(r, 1)],
                      sem.at[dst_slot]).start(priority=r % DMA_THREADS)

    def wait_tile(dst_slot):
        def body(r, c):
            _row_copy(h_hbm, 0, xbuf.at[dst_slot, pl.ds(r, 1)], sem.at[dst_slot]).wait()
            return c
        lax.fori_loop(0, tm, body, 0, unroll=DMA_UNROLL)

    @pl.when(i == 0)
    def _():
        start_tile(0, 0)

    wait_tile(slot)
    nxt = jnp.minimum(i + 1, n - 1)

    @pl.when(i < n_active)
    def _():
        start_tile(nxt, 1 - slot)
        x = _unpack_halves(xbuf[slot]).astype(BF16)
        h1 = _dot(x, w1_ref[...])
        h3 = _dot(x, w3_ref[...])
        he = (h1 * _sigmoid(h1)) * h3
        y_ref[...] = _pack_halves(_dot(he.astype(BF16), w2_ref[...]))

    @pl.when(i >= n_active)
    def _():
        start_tile(nxt, 1 - slot)
        y_ref[...] = jnp.zeros_like(y_ref)

    @pl.when(i == n - 1)
    def _():
        wait_tile(1 - slot)


def _expert_ffn(hp, tile_expert, row_token, n_active, layer, w1, w3, w2):
    half = hp.shape[1]
    d = 2 * half
    tm = MOE_TM
    n_tiles = tile_expert.shape[0]
    wspec = lambda shape: pl.BlockSpec((None, None) + shape, lambda i, te, rt, na: (layer, te[i], 0, 0))
    return pl.pallas_call(
        functools.partial(_ffn_kernel, tm=tm),
        grid_spec=pltpu.PrefetchScalarGridSpec(
            num_scalar_prefetch=3,
            grid=(n_tiles,),
            in_specs=[pl.BlockSpec(memory_space=pl.ANY), wspec((d, D_FF_PAD)), wspec((d, D_FF_PAD)),
                      wspec((D_FF_PAD, d))],
            out_specs=pl.BlockSpec((tm, half), lambda i, te, rt, na: (i, 0)),
            scratch_shapes=[pltpu.VMEM((2, tm, half), jnp.uint32), pltpu.SemaphoreType.DMA((2,))]),
        out_shape=jax.ShapeDtypeStruct((n_tiles * tm, half), jnp.uint32),
        compiler_params=_cparams(("arbitrary",)),
        name="expert_ffn",
    )(tile_expert, row_token, n_active, hp, w1, w3, w2)


def _combine_ln_kernel(p1_ref, p2_ref, h_ref, y_hbm, w_ref, g_ref, b_ref, o_ref, ob_ref, ybuf, sem, *, tm, n_mask):
    i = pl.program_id(0)
    n = pl.num_programs(0)
    slot = i % 2

    def start_tile(tile, dst_slot):
        for r in range(tm):
            _row_copy(y_hbm, p1_ref[tile * tm + r], ybuf.at[dst_slot, 0, pl.ds(r, 1)], sem.at[dst_slot]).start(priority=0)
            _row_copy(y_hbm, p2_ref[tile * tm + r], ybuf.at[dst_slot, 1, pl.ds(r, 1)],
                      sem.at[dst_slot]).start(priority=DMA_THREADS - 1)

    def wait_tile(dst_slot):
        def body(r, c):
            _row_copy(y_hbm, 0, ybuf.at[dst_slot, 0, pl.ds(r, 1)], sem.at[dst_slot]).wait()
            _row_copy(y_hbm, 0, ybuf.at[dst_slot, 1, pl.ds(r, 1)], sem.at[dst_slot]).wait()
            return c
        lax.fori_loop(0, tm, body, 0, unroll=DMA_UNROLL)

    @pl.when(i == 0)
    def _():
        start_tile(0, 0)

    wait_tile(slot)
    start_tile(jnp.minimum(i + 1, n - 1), 1 - slot)
    w = w_ref[...]
    ffn = w[:, 0:1] * _unpack_halves(ybuf[slot, 0]) + w[:, 1:2] * _unpack_halves(ybuf[slot, 1])
    x = h_ref[...] * DN_ALPHA + ffn
    _emit_ln(_layernorm_rows(x, g_ref[...], b_ref[...]), o_ref, ob_ref, n_mask, tm)

    @pl.when(i == n - 1)
    def _():
        wait_tile(1 - slot)


def _combine_ln(h, y, pos1, pos2, wts, g, b, n_mask):
    m, d = h.shape
    tm = _pick(m, (256, 128, 64))
    row = pl.BlockSpec((tm, d), lambda i, p1, p2: (i, 0))
    vec = pl.BlockSpec((1, d), lambda i, p1, p2: (0, 0))
    return pl.pallas_call(
        functools.partial(_combine_ln_kernel, tm=tm, n_mask=n_mask),
        grid_spec=pltpu.PrefetchScalarGridSpec(
            num_scalar_prefetch=2,
            grid=(m // tm,),
            in_specs=[row, pl.BlockSpec(memory_space=pl.ANY),
                      pl.BlockSpec((tm, 2), lambda i, p1, p2: (i, 0)), vec, vec],
            out_specs=(row, row),
            scratch_shapes=[pltpu.VMEM((2, 2, tm, d // 2), jnp.uint32), pltpu.SemaphoreType.DMA((2,))]),
        out_shape=(jax.ShapeDtypeStruct((m, d), F32), jax.ShapeDtypeStruct((m, d), BF16)),
        compiler_params=_cparams(("arbitrary",)),
        name="moe_combine_layernorm",
    )(pos1, pos2, h, y, wts, g[None], b[None])


def _dispatch_plan(eidx, tm):
    m = eidx.shape[1]
    flat_e = jnp.concatenate([eidx[0], eidx[1]])
    tok = jnp.arange(m, dtype=jnp.int32)
    flat_t = jnp.concatenate([tok, tok])
    onehot = (flat_e[:, None] == jnp.arange(N_EXPERTS, dtype=jnp.int32)[None, :]).astype(jnp.int32)
    rank = jnp.take_along_axis(jnp.cumsum(onehot, axis=0) - onehot, flat_e[:, None], axis=1)[:, 0]
    counts = jnp.sum(onehot, axis=0)
    padded = ((counts + tm - 1) // tm) * tm
    ends = jnp.cumsum(padded)
    offs = ends - padded
    pos = (offs[flat_e] + rank).astype(jnp.int32)
    n_tiles = (2 * m) // tm + N_EXPERTS
    row_token = jnp.zeros((n_tiles * tm,), jnp.int32).at[pos].set(flat_t)
    n_active = (ends[-1] // tm).astype(jnp.int32)
    tile_id = jnp.arange(n_tiles, dtype=jnp.int32)
    tile_id = jnp.minimum(tile_id, n_active - 1)
    tile_expert = (jnp.sum(tile_id[:, None] >= (offs // tm)[None, :], axis=1) - 1).astype(jnp.int32)
    return tile_expert, row_token, n_active.reshape(1), pos[:m], pos[m:]


def _moe_ln(h, hp, router_w, router_bias, layer, w1, w3, w2, g, b, n_mask):
    eidx, ewts = _router(h, router_w, router_bias)
    tile_expert, row_token, n_active, pos1, pos2 = _dispatch_plan(eidx, MOE_TM)
    y = _expert_ffn(hp, tile_expert, row_token, n_active, layer, w1, w3, w2)
    return _combine_ln(h, y, pos1, pos2, ewts[0:2].T, g, b, n_mask)


def _cast_pad_kernel(x_ref, o_ref):
    rows, cols = x_ref.shape
    o_ref[...] = jnp.zeros(o_ref.shape, o_ref.dtype)
    o_ref[0:rows, 0:cols] = x_ref[...].astype(o_ref.dtype)


def _cast_pad(w, rows_out, cols_out, block_rows, block_cols):
    n_l, n_e, r, c = w.shape
    in_rows = r if rows_out != r else block_rows
    in_cols = c if cols_out != c else block_cols
    out_rows = rows_out if rows_out != r else block_rows
    out_cols = cols_out if cols_out != c else block_cols
    grid = (n_l * n_e, rows_out // out_rows, cols_out // out_cols)
    idx = lambda e, i, j: (e // n_e, e % n_e, i, j)
    return pl.pallas_call(
        _cast_pad_kernel,
        grid=grid,
        in_specs=[pl.BlockSpec((None, None, in_rows, in_cols), idx)],
        out_specs=pl.BlockSpec((None, None, out_rows, out_cols), idx),
        out_shape=jax.ShapeDtypeStruct((n_l, n_e, rows_out, cols_out), BF16),
        compiler_params=_cparams(("parallel", "parallel", "parallel")),
        name="cast_pad_weights",
    )(w)


def _cast_pad_transposed_kernel(x_ref, o_ref):
    rows, cols = x_ref.shape
    x = jnp.concatenate([x_ref[...], jnp.zeros((o_ref.shape[1] - rows, cols), x_ref.dtype)], axis=0)
    o_ref[...] = x.T.astype(o_ref.dtype)


def _cast_pad_transposed(wt, cols_out, block):
    n_l, n_e, c, r = wt.shape
    return pl.pallas_call(
        _cast_pad_transposed_kernel,
        grid=(n_l * n_e, r // block),
        in_specs=[pl.BlockSpec((None, None, c, block), lambda e, i: (e // n_e, e % n_e, 0, i))],
        out_specs=pl.BlockSpec((None, None, block, cols_out), lambda e, i: (e // n_e, e % n_e, i, 0)),
        out_shape=jax.ShapeDtypeStruct((n_l, n_e, r, cols_out), BF16),
        compiler_params=_cparams(("parallel", "parallel")),
        name="cast_pad_weights_t",
    )(wt)


def _prep_experts(w1, w3, w2):
    d = w1.shape[2]
    w1 = _cast_pad_transposed(jnp.swapaxes(w1, 2, 3), D_FF_PAD, 1024)
    w3 = _cast_pad_transposed(jnp.swapaxes(w3, 2, 3), D_FF_PAD, 1024)
    w2 = _cast_pad(w2, D_FF_PAD, d, D_FF_PAD, 2048)
    return w1, w3, w2


def _rotary_tables(length):
    half = RET_DK // 2
    pos = jnp.arange(length, dtype=jnp.int32) - N_PAD
    inv = ROPE_BASE ** (-jnp.arange(half, dtype=F32) / half)
    ang = pos.astype(F32)[:, None] * inv[None, :]
    cos, sin = jnp.cos(ang), jnp.sin(ang)
    return jnp.concatenate([cos, cos], axis=1), jnp.concatenate([-sin, sin], axis=1)


def _mixer(hb, cos, sin, layer, w_in, ret_gn_g, ret_gn_b, gla_a_up, gla_a_bias, gla_gn_g, gla_gn_b, rw_mix, rw_w_up,
           rw_w0, rw_a_up, rw_a0, rw_g_up, rw_k_k, rw_k_a, rw_r_k, rw_gn_g, rw_gn_b, w_br_ret, w_br_gla,
           w_br_rw, w_out):
    cols = lambda a, b: lax.slice(w_in, (layer, 0, a), (layer + 1, D_MODEL, b))[0]
    w_small = jnp.concatenate([cols(W_IN_SM0, W_IN_GATE0), cols(W_IN_GA0, W_IN_RW0),
                               jnp.zeros((D_MODEL, LANES - GLA_RANK), BF16)], axis=1)
    proj = _matmul(hb, w_in, BF16, 512, layer=layer, n=N_HEAD)
    proj_rw = _matmul(hb, cols(W_IN_RW0, W_IN_SM0), BF16, 512)
    gates = _matmul(hb, cols(W_IN_GATE0, w_in.shape[2]), BF16, 512)
    small = _matmul(hb, w_small, F32, N_SMALL)
    o_ret = _retention(proj, cos, sin, ret_gn_g, ret_gn_b)
    o_gla = _gla(proj, small, gla_a_up, gla_a_bias, gla_gn_g, gla_gn_b)
    o_rw = _rwkv(proj_rw, small, rw_mix, rw_w_up, rw_w0, rw_a_up, rw_a0, rw_g_up, rw_k_k, rw_k_a, rw_r_k,
                 rw_gn_g, rw_gn_b)
    merged = _merge(o_ret, o_gla, o_rw, w_br_ret, w_br_gla, w_br_rw, layer, gates)
    return _matmul(merged, w_out, F32, 512, layer=layer)


def kernel(x, meta, ln_in_g, ln_in_b, w_in, ret_gn_g, ret_gn_b, gla_a_up, gla_a_bias, gla_gn_g, gla_gn_b, rw_mix, rw_w_up, rw_w0, rw_a_up, rw_a0, rw_g_up, rw_k_k, rw_k_a, rw_r_k, rw_gn_g, rw_gn_b, w_br_ret, w_br_gla, w_br_rw, w_out, ln1_g, ln1_b, router_w, router_bias, exp_w1, exp_w3, exp_w2, ln2_g, ln2_b):
    batch, seq, d = x.shape
    assert batch == 1 and d == D_MODEL
    real = N_PAD + N_META + seq
    length = -(-real // ROW_TILE) * ROW_TILE
    h = jnp.concatenate([jnp.zeros((N_PAD, d), x.dtype), meta.astype(x.dtype), x[0],
                         jnp.zeros((length - real, d), x.dtype)], axis=0)
    cos, sin = _rotary_tables(length)
    h, hb = _layer_norm(h, None, ln_in_g, ln_in_b, N_PAD)
    w_in, w_out = w_in.astype(BF16), w_out.astype(BF16)
    w_br_ret, w_br_gla, w_br_rw = w_br_ret.astype(BF16), w_br_gla.astype(BF16), w_br_rw.astype(BF16)
    w1, w3, w2 = _prep_experts(exp_w1, exp_w3, exp_w2)
    for l in range(DEPTH):
        mix = _mixer(hb, cos, sin, l, w_in, ret_gn_g[l], ret_gn_b[l], gla_a_up[l], gla_a_bias[l], gla_gn_g[l],
                     gla_gn_b[l], rw_mix[l], rw_w_up[l], rw_w0[l], rw_a_up[l], rw_a0[l], rw_g_up[l], rw_k_k[l],
                     rw_k_a[l], rw_r_k[l], rw_gn_g[l], rw_gn_b[l], w_br_ret, w_br_gla, w_br_rw, w_out)
        h, hp = _layer_norm(h, mix, ln1_g[l], ln1_b[l], 0, packed=True)
        h, hb = _moe_ln(h, hp, router_w, router_bias, l, w1, w3, w2, ln2_g[l], ln2_b[l], N_PAD)
    return h[N_PAD + N_META:real][None]
```

```python
import functools

import jax
import jax.numpy as jnp
from jax import lax
from jax.experimental import pallas as pl
from jax.experimental.pallas import tpu as pltpu

F32 = jnp.float32
BF16 = jnp.bfloat16

D_MODEL = 4096
DEPTH = 2
CHUNK = 64
N_META = 16
N_PAD = CHUNK - N_META

RET_HEADS, RET_DK, RET_DV = 12, 128, 128
RET_W = RET_HEADS * RET_DV
ROPE_BASE = 10000.0

GLA_HEADS, GLA_DK, GLA_DV = 6, 128, 256
GLA_WK = GLA_HEADS * GLA_DK
GLA_WV = GLA_HEADS * GLA_DV
GLA_RANK = 16
GLA_TAU = 16.0

RWKV_HEADS, RWKV_DH = 16, 64
RWKV_W = RWKV_HEADS * RWKV_DH
RWKV_RANK_W, RWKV_RANK_A, RWKV_RANK_G = 64, 64, 128

N_EXPERTS = 32
N_GROUPS = 4
EXPERTS_PER_GROUP = N_EXPERTS // N_GROUPS
D_FF = 704

DN_ALPHA = (2 * DEPTH) ** 0.25
LN_EPS = 1e-5
GN_EPS = 1e-5
RWKV_GN_EPS = 64e-5
NORM_EPS = 1e-12

LANES = 128
SUBLANES = 8
VMEM_LIMIT_BYTES = 56 * 1024 * 1024

RET_Q0, RET_K0, RET_V0, RET_G0 = 0, RET_W, 2 * RET_W, 3 * RET_W
GLA_Q0 = 4 * RET_W
GLA_K0 = GLA_Q0 + GLA_WK
GLA_V0 = GLA_K0 + GLA_WK
GLA_G0 = GLA_V0 + GLA_WV
N_HEAD = GLA_G0 + GLA_WV
W_IN_GA0 = N_HEAD
W_IN_RW0 = W_IN_GA0 + GLA_RANK
W_IN_SM0 = W_IN_RW0 + 3 * RWKV_W
W_IN_GATE0 = W_IN_SM0 + RWKV_RANK_W + RWKV_RANK_A + RWKV_RANK_G
RW_R0, RW_K0, RW_V0 = 0, RWKV_W, 2 * RWKV_W
N_SMALL = 3 * LANES
D_FF_PAD = 768

ROW_TILE = 256
FRONT = ROW_TILE
N_MASK = FRONT - N_META
MOE_TM = 256
DMA_UNROLL = 8
DMA_THREADS = 2
FFN_SLOTS = 3


def _cparams(sem):
    return pltpu.CompilerParams(dimension_semantics=sem, vmem_limit_bytes=VMEM_LIMIT_BYTES)


def _pick(n, cands):
    for c in cands:
        if n % c == 0:
            return c
    raise ValueError(f"no tile in {cands} divides {n}")


def _dot(a, b):
    return jnp.dot(a, b, preferred_element_type=F32)


def _split(x):
    hi = x.astype(BF16)
    return hi, (x - hi.astype(F32)).astype(BF16)


def _split3(x):
    hi = x.astype(BF16)
    rest = x - hi.astype(F32)
    mid = rest.astype(BF16)
    return hi, mid, (rest - mid.astype(F32)).astype(BF16)


def _dot_hi(a, b):
    (ah, al), (bh, bl) = _split(a), _split(b)
    return _dot(jnp.concatenate([ah, ah, al], axis=1), jnp.concatenate([bh, bl, bh], axis=0))


def _dot_sel_rhs(a, sel):
    return _dot(jnp.concatenate(_split3(a), axis=1), jnp.concatenate([sel, sel, sel], axis=0))


def _dot_sel_lhs(sel, b):
    return _dot(jnp.concatenate([sel, sel, sel], axis=1), jnp.concatenate(_split3(b), axis=0))


def _dot_nt(a, b):
    return lax.dot_general(a, b, (((1,), (1,)), ((), ())), preferred_element_type=F32)


def _dot_tn(a, b):
    return lax.dot_general(a, b, (((0,), (0,)), ((), ())), preferred_element_type=F32)


def _sigmoid(x):
    return 1.0 / (1.0 + jnp.exp(-x))


def _softplus(x):
    return jnp.maximum(x, 0.0) + jnp.log(1.0 + jnp.exp(-jnp.abs(x)))


def _mm_kernel(x_ref, w_ref, o_ref):
    o_ref[...] = _dot(x_ref[...], w_ref[...]).astype(o_ref.dtype)


def _matmul(x, w, out_dtype, tn, layer=None, n=None):
    m, k = x.shape
    n = w.shape[-1] if n is None else n
    assert n % tn == 0
    tm = _pick(m, (1280, 640, 256, 128, 64))
    if w.ndim == 3:
        w_spec = pl.BlockSpec((None, k, tn), lambda i, j: (layer, 0, j))
    else:
        w_spec = pl.BlockSpec((k, tn), lambda i, j: (0, j))
    return pl.pallas_call(
        _mm_kernel,
        grid=(m // tm, n // tn),
        in_specs=[pl.BlockSpec((tm, k), lambda i, j: (i, 0)), w_spec],
        out_specs=pl.BlockSpec((tm, tn), lambda i, j: (i, j)),
        out_shape=jax.ShapeDtypeStruct((m, n), out_dtype),
        compiler_params=_cparams(("parallel", "parallel")),
        name="dense_matmul",
    )(x, w)


def _layernorm_rows(x, g, b):
    mu = jnp.mean(x, axis=-1, keepdims=True)
    xc = x - mu
    var = jnp.mean(xc * xc, axis=-1, keepdims=True)
    return xc * lax.rsqrt(var + LN_EPS) * g + b


HIGH_HALF = 0xFFFF0000


def _pack_halves(y):
    half = y.shape[1] // 2
    lo = pltpu.bitcast(y[:, :half].astype(BF16).astype(F32), jnp.uint32) >> 16
    hi = pltpu.bitcast(y[:, half:].astype(BF16).astype(F32), jnp.uint32) & jnp.uint32(HIGH_HALF)
    return lo | hi


def _unpack_halves(w):
    lo = pltpu.bitcast(w << 16, F32)
    hi = pltpu.bitcast(w & jnp.uint32(HIGH_HALF), F32)
    return jnp.concatenate([lo, hi], axis=1)


def _emit_ln(y, o_ref, ob_ref, n_mask, tm):
    o_ref[...] = y
    if ob_ref.dtype == jnp.uint32:
        ob_ref[...] = _pack_halves(y)
        return
    yb = y.astype(BF16)
    if n_mask:
        row = pl.program_id(0) * tm + lax.broadcasted_iota(jnp.int32, (tm, 1), 0)
        yb = jnp.where(row >= n_mask, yb, jnp.zeros_like(yb))
    ob_ref[...] = yb


def _res_ln_kernel(h_ref, f_ref, g_ref, b_ref, o_ref, ob_ref, *, alpha, n_mask, tm):
    x = h_ref[...] * alpha + f_ref[...]
    _emit_ln(_layernorm_rows(x, g_ref[...], b_ref[...]), o_ref, ob_ref, n_mask, tm)


def _embed_ln_kernel(x_ref, meta_ref, g_ref, b_ref, o_ref, ob_ref, *, n_mask, tm):
    front = jnp.concatenate([jnp.zeros((n_mask, x_ref.shape[1]), F32), meta_ref[...]], axis=0)
    x = jnp.where(pl.program_id(0) == 0, front, x_ref[...])
    _emit_ln(_layernorm_rows(x, g_ref[...], b_ref[...]), o_ref, ob_ref, n_mask, tm)


def _embed_layer_norm(x, meta, g, b):
    seq, d = x.shape
    tm = FRONT
    assert seq % tm == 0 and N_MASK + meta.shape[0] == tm
    m = tm + seq
    row = pl.BlockSpec((tm, d), lambda i: (i, 0))
    vec = pl.BlockSpec((1, d), lambda i: (0, 0))
    return pl.pallas_call(
        functools.partial(_embed_ln_kernel, n_mask=N_MASK, tm=tm),
        grid=(m // tm,),
        in_specs=[pl.BlockSpec((tm, d), lambda i: (jnp.maximum(i - 1, 0), 0)),
                  pl.BlockSpec(meta.shape, lambda i: (0, 0)), vec, vec],
        out_specs=(row, row),
        out_shape=(jax.ShapeDtypeStruct((m, d), F32), jax.ShapeDtypeStruct((m, d), BF16)),
        compiler_params=_cparams(("parallel",)), name="embed_layernorm",
    )(x, meta, g[None], b[None])


def _layer_norm(h, f, g, b, n_mask, packed=False):
    m, d = h.shape
    tm = _pick(m, (256, 128, 64))
    row = pl.BlockSpec((tm, d), lambda i: (i, 0))
    vec = pl.BlockSpec((1, d), lambda i: (0, 0))
    if packed:
        narrow, narrow_spec = jax.ShapeDtypeStruct((m, d // 2), jnp.uint32), pl.BlockSpec((tm, d // 2), lambda i: (i, 0))
    else:
        narrow, narrow_spec = jax.ShapeDtypeStruct((m, d), BF16), row
    return pl.pallas_call(
        functools.partial(_res_ln_kernel, alpha=DN_ALPHA, n_mask=n_mask, tm=tm),
        grid=(m // tm,), in_specs=[row, row, vec, vec], out_specs=(row, narrow_spec),
        out_shape=(jax.ShapeDtypeStruct((m, d), F32), narrow),
        compiler_params=_cparams(("parallel",)), name="res_layernorm",
    )(h, f, g[None], b[None])


def _groupnorm_lanes(o, g, b, eps):
    mu = jnp.mean(o, axis=-1, keepdims=True)
    oc = o - mu
    var = jnp.mean(oc * oc, axis=-1, keepdims=True)
    return oc * lax.rsqrt(var + eps) * g + b


def _chunks_per_step(length, cands):
    return _pick(length // CHUNK, cands)


def _by_head(x, n_heads, width):
    return jnp.stack([x[:, h * width:(h + 1) * width] for h in range(n_heads)], axis=0)


def _heads_to_lanes(x):
    return jnp.concatenate([x[h] for h in range(x.shape[0])], axis=1)


def _ret_kernel(q_ref, k_ref, v_ref, g_ref, cos_ref, sin_ref, intra_ref, qdec_ref, kdec_ref, cdec_ref,
                gng_ref, gnb_ref, o_ref, state_ref, *, nc):
    @pl.when(pl.program_id(0) == 0)
    def _():
        state_ref[...] = jnp.zeros_like(state_ref)

    cos, sin = cos_ref[...], sin_ref[...]
    half = RET_DK // 2

    def rotated(ref):
        x = ref[...].astype(F32)
        return jnp.stack([xh * cos + pltpu.roll(xh, half, 1) * sin
                          for xh in (x[:, h * RET_DK:(h + 1) * RET_DK] for h in range(RET_HEADS))], axis=0)

    q = rotated(q_ref) * (RET_DK ** -0.5)
    k = rotated(k_ref)
    v = _by_head(v_ref[...], RET_HEADS, RET_DV)
    intra, qdec, kdec = intra_ref[...], qdec_ref[...], kdec_ref[...]
    cdec = cdec_ref[...][:, 0:1]
    state = state_ref[...]
    outs = []
    for j in range(nc):
        sl = slice(j * CHUNK, (j + 1) * CHUNK)
        qc, kc, vc = q[:, sl], k[:, sl], v[:, sl]
        s = _bmm_nt(qc.astype(BF16), kc.astype(BF16)) * intra
        outs.append(_bmm(s.astype(BF16), vc) + _bmm((qc * qdec).astype(BF16), state.astype(BF16)))
        state = state * cdec + _bmm_tn((kc * kdec).astype(BF16), vc)
    state_ref[...] = state
    y = _groupnorm_lanes(jnp.concatenate(outs, axis=1), gng_ref[...], gnb_ref[...], GN_EPS)
    gate = _by_head(g_ref[...].astype(F32), RET_HEADS, RET_DV)
    o_ref[...] = _heads_to_lanes(y * (gate * _sigmoid(gate))).astype(o_ref.dtype)


def _retention(proj, cos, sin, gn_g, gn_b):
    length = proj.shape[0]
    nc = _chunks_per_step(length, (5, 4, 2, 1))
    t = nc * CHUNK
    lg = jnp.log1p(-jnp.exp2(-5.0 - jnp.arange(RET_HEADS, dtype=F32)))
    i = jnp.arange(CHUNK, dtype=F32)
    intra = jnp.exp(lg[:, None, None] * jnp.abs(i[:, None] - i[None, :]))
    qdec = jnp.broadcast_to(jnp.exp(lg[:, None] * (i[None, :] + 1.0))[:, :, None], (RET_HEADS, CHUNK, RET_DK))
    kdec = jnp.broadcast_to(jnp.exp(lg[:, None] * (CHUNK - 1.0 - i[None, :]))[:, :, None], (RET_HEADS, CHUNK, RET_DK))
    cdec = jnp.broadcast_to(jnp.exp(lg * CHUNK)[:, None, None], (RET_HEADS, SUBLANES, RET_DV))
    col = lambda c0: pl.BlockSpec((t, RET_W), lambda c: (c, c0 // RET_W))
    tab = pl.BlockSpec((t, RET_DK), lambda c: (c, 0))
    whole = lambda a: pl.BlockSpec(a.shape, lambda c: (0,) * a.ndim)
    gng, gnb = gn_g.reshape(RET_HEADS, 1, RET_DV), gn_b.reshape(RET_HEADS, 1, RET_DV)
    consts = (intra, qdec, kdec, cdec, gng, gnb)
    return pl.pallas_call(
        functools.partial(_ret_kernel, nc=nc),
        grid=(length // t,),
        in_specs=[col(RET_Q0), col(RET_K0), col(RET_V0), col(RET_G0), tab, tab] + [whole(a) for a in consts],
        out_specs=pl.BlockSpec((t, RET_W), lambda c: (c, 0)),
        out_shape=jax.ShapeDtypeStruct((length, RET_W), BF16),
        scratch_shapes=[pltpu.VMEM((RET_HEADS, RET_DK, RET_DV), F32)],
        compiler_params=_cparams(("arbitrary",)),
        name="retention",
    )(proj, proj, proj, proj, cos, sin, *consts)


def _gla_kernel(q_ref, k_ref, v_ref, g_ref, ga_ref, aup_ref, ab_ref, gng_ref, gnb_ref, o_ref, state_ref, *, nc):
    @pl.when(pl.program_id(0) == 0)
    def _():
        state_ref[...] = jnp.zeros_like(state_ref)

    z = _dot_hi(ga_ref[...], aup_ref[...]) + ab_ref[...]
    log_a = -_softplus(-z) / GLA_TAU
    q = _by_head((q_ref[...].astype(F32) * (GLA_DK ** -0.5)).astype(BF16), GLA_HEADS, GLA_DK)
    k = k_ref[...].astype(F32)
    v = _by_head(v_ref[...], GLA_HEADS, GLA_DV)
    row = lax.broadcasted_iota(jnp.int32, (CHUNK, CHUNK), 0)
    colm = lax.broadcasted_iota(jnp.int32, (CHUNK, CHUNK), 1)
    later = (colm > row).astype(BF16)
    state = state_ref[...]
    outs = []
    for j in range(nc):
        sl = slice(j * CHUNK, (j + 1) * CHUNK)
        la = log_a[sl]
        to_end = _dot_sel_lhs(later, la)
        dec = _by_head(jnp.exp(to_end[0:1] + la[0:1]), GLA_HEADS, GLA_DK)
        kc = _by_head((k[sl] * jnp.exp(to_end)).astype(BF16), GLA_HEADS, GLA_DK)
        state = state * dec + _bmm_tn(v[:, sl], kc)
        outs.append(_bmm_nt(q[:, sl], state.astype(BF16)))
    state_ref[...] = state
    y = _groupnorm_lanes(jnp.concatenate(outs, axis=1), gng_ref[...], gnb_ref[...], GN_EPS)
    gate = _by_head(g_ref[...].astype(F32), GLA_HEADS, GLA_DV)
    o_ref[...] = _heads_to_lanes(y * (gate * _sigmoid(gate))).astype(o_ref.dtype)


def _gla(proj, small, a_up, a_bias, gn_g, gn_b):
    length = proj.shape[0]
    nc = _chunks_per_step(length, (5, 4, 2, 1))
    t = nc * CHUNK
    a_up_pad = jnp.zeros((LANES, GLA_WK), F32).at[:GLA_RANK].set(a_up)
    kcol = lambda c0: pl.BlockSpec((t, GLA_WK), lambda c: (c, c0 // GLA_WK))
    vcol = lambda c0: pl.BlockSpec((t, GLA_WV), lambda c: (c, c0 // GLA_WV))
    whole = lambda a: pl.BlockSpec(a.shape, lambda c: (0,) * a.ndim)
    consts = (a_up_pad, a_bias[None], gn_g.reshape(GLA_HEADS, 1, GLA_DV), gn_b.reshape(GLA_HEADS, 1, GLA_DV))
    return pl.pallas_call(
        functools.partial(_gla_kernel, nc=nc),
        grid=(length // t,),
        in_specs=[kcol(GLA_Q0), kcol(GLA_K0), vcol(GLA_V0), vcol(GLA_G0),
                  pl.BlockSpec((t, LANES), lambda c: (c, 2))] + [whole(a) for a in consts],
        out_specs=pl.BlockSpec((t, GLA_WV), lambda c: (c, 0)),
        out_shape=jax.ShapeDtypeStruct((length, GLA_WV), BF16),
        scratch_shapes=[pltpu.VMEM((GLA_HEADS, GLA_DV, GLA_DK), F32)],
        compiler_params=_cparams(("arbitrary",)),
        name="gla",
    )(proj, proj, proj, proj, small, *consts)


RW_PAIRS = RWKV_HEADS // 2


def _head_sum(x, ones_blk):
    parts = [_dot_sel_rhs(x[:, p * LANES:(p + 1) * LANES], ones_blk) for p in range(RW_PAIRS)]
    return jnp.concatenate(parts, axis=1)


def _bmm(a, b):
    return jnp.einsum('pik,pkj->pij', a, b, preferred_element_type=F32)


def _bmm_nt(a, b):
    return jnp.einsum('pik,pjk->pij', a, b, preferred_element_type=F32)


def _bmm_tn(a, b):
    return jnp.einsum('pki,pkj->pij', a, b, preferred_element_type=F32)


def _bmm3(a, b):
    (ah, al), (bh, bl) = _split(a), _split(b)
    return _bmm(jnp.concatenate([ah, ah, al], axis=2), jnp.concatenate([bh, bl, bh], axis=1))


def _bmm3_tn(a, b):
    (ah, al), (bh, bl) = _split(a), _split(b)
    return _bmm_tn(jnp.concatenate([ah, ah, al], axis=1), jnp.concatenate([bh, bl, bh], axis=1))


def _unit_lower_inverse_minus_eye(m, i0, i1):
    def joins(s):
        return jnp.logical_and(i0 // (2 * s) == i1 // (2 * s),
                               jnp.logical_and((i0 // s) % 2 == 1, (i1 // s) % 2 == 0))

    xp = -jnp.where(joins(1), m, 0.0)
    s = 2
    while s < CHUNK:
        c = jnp.where(joins(s), m, 0.0)
        y = c + _bmm3(xp, c)
        xp = xp - (y + _bmm3(y, xp))
        s *= 2
    return xp


def _rwkv_kernel(r_ref, k_ref, v_ref, sm_ref, mixr_ref, mixk_ref, mixv_ref, mixs_ref, wup_ref, w0_ref,
                 aup_ref, a0_ref, gup_ref, kk_ref, ka_ref, rk_ref, gng_ref, gnb_ref, o_ref,
                 state_ref, cr_ref, ck_ref, cv_ref, cs_ref, a_s, bt_s, kt_s, bp_s, kp_s, v_s, r_s, gam_s,
                 o_s, g_s, bonus_s, y_s, u_s, *, nc):
    t = nc * CHUNK

    @pl.when(pl.program_id(0) == 0)
    def _():
        state_ref[...] = jnp.zeros_like(state_ref)
        cr_ref[...] = jnp.zeros_like(cr_ref)
        ck_ref[...] = jnp.zeros_like(ck_ref)
        cv_ref[...] = jnp.zeros_like(cv_ref)
        cs_ref[...] = jnp.zeros_like(cs_ref)

    rowi = lax.broadcasted_iota(jnp.int32, (t, 1), 0)

    def shifted(x, carry_ref, mix):
        prev = jnp.where(rowi == 0, carry_ref[SUBLANES - 1:SUBLANES], pltpu.roll(x, 1, 0))
        carry_ref[...] = x[t - SUBLANES:t]
        return x + (prev - x) * mix

    r = shifted(r_ref[...].astype(F32), cr_ref, mixr_ref[...])
    k = shifted(k_ref[...].astype(F32), ck_ref, mixk_ref[...])
    v = shifted(v_ref[...].astype(F32), cv_ref, mixv_ref[...])
    sm = shifted(sm_ref[...], cs_ref, mixs_ref[...])
    wa, gl = sm[:, :LANES], sm[:, LANES:]

    i0 = lax.broadcasted_iota(jnp.int32, (LANES, LANES), 0)
    i1 = lax.broadcasted_iota(jnp.int32, (LANES, LANES), 1)
    same_head = (i0 // RWKV_DH) == (i1 // RWKV_DH)
    ones_blk = same_head.astype(BF16)

    w = -_softplus(-(w0_ref[...] + _dot_hi(jnp.tanh(wa), wup_ref[...]))) - 0.5
    a = _sigmoid(a0_ref[...] + _dot_hi(wa, aup_ref[...]))
    g_s[...] = _dot_hi(_sigmoid(gl), gup_ref[...])
    kk = k * kk_ref[...]
    kkn = kk / jnp.maximum(jnp.sqrt(_head_sum(kk * kk, ones_blk)), NORM_EPS)
    km = k * (1.0 + (a - 1.0) * ka_ref[...])
    b = kkn * a
    bonus_s[...] = _head_sum(r * km * rk_ref[...], ones_blk) * v
    v_s[...] = v
    r_s[...] = r

    log_d = -jnp.exp(w)
    c0 = lax.broadcasted_iota(jnp.int32, (CHUNK, CHUNK), 0)
    c1 = lax.broadcasted_iota(jnp.int32, (CHUNK, CHUNK), 1)
    upto = (c1 <= c0).astype(BF16)
    for j in range(nc):
        rows = slice(j * CHUNK, (j + 1) * CHUNK)
        ld = log_d[rows]
        c = _dot_sel_lhs(upto, ld)
        c_end = c[CHUNK - 1:CHUNK]
        grow = jnp.exp(-c)
        to_end = jnp.exp(c_end - c)
        a_s[rows] = kkn[rows] * jnp.exp(c - ld)
        bt_s[rows] = b[rows] * grow
        kt_s[rows] = km[rows] * grow
        bp_s[rows] = b[rows] * to_end
        kp_s[rows] = km[rows] * to_end
        gam_s[j * SUBLANES:(j + 1) * SUBLANES] = jnp.broadcast_to(jnp.exp(c_end), (SUBLANES, RWKV_W))

    lo = lax.broadcasted_iota(jnp.int32, (CHUNK, LANES), 1) < RWKV_DH
    strict_lower = jnp.logical_and(same_head, i1 < i0)

    def by_pair(x):
        return jnp.stack([x[:, p * LANES:(p + 1) * LANES] for p in range(RW_PAIRS)], axis=0)

    def stacked(ref, rows):
        x = by_pair(ref[rows, :])
        return jnp.concatenate([jnp.where(lo, x, 0.0), jnp.where(lo, 0.0, x)], axis=1)

    def chunk_local(j, carry):
        rows = pl.ds(pl.multiple_of(j * CHUNK, CHUNK), CHUNK)
        a2, bp2 = stacked(a_s, rows), stacked(bp_s, rows)
        m_all = _bmm_nt(a2.astype(BF16),
                        jnp.concatenate([stacked(bt_s, rows), stacked(kt_s, rows)], axis=1).astype(BF16))
        m_b = jnp.where(strict_lower, m_all[:, :, :LANES], 0.0)
        m_k = jnp.where(strict_lower, m_all[:, :, LANES:], 0.0)
        t_mi = _unit_lower_inverse_minus_eye(m_b, i0, i1)
        z2 = bp2 + _bmm3_tn(t_mi, bp2)
        yk = _bmm3_tn(jnp.concatenate([a2, m_k], axis=2), z2)
        kpp = stacked(kp_s, rows) - yk[:, LANES:]
        uu = _bmm_tn(stacked(v_s, rows).astype(BF16), kpp.astype(BF16))
        y_s[j] = yk[:, :LANES].astype(BF16)
        u_s[j] = uu[:, :RWKV_DH] + uu[:, RWKV_DH:]
        return carry

    lax.fori_loop(0, nc, chunk_local, 0)

    def chunk_advance(j, carry):
        rows = pl.ds(pl.multiple_of(j * CHUNK, CHUNK), CHUNK)
        gam = by_pair(gam_s[pl.ds(pl.multiple_of(j * SUBLANES, SUBLANES), SUBLANES)])[:, 0:1]
        s = state_ref[...]
        s_hi, s_lo = _split(s)
        y = y_s[j]
        s = s * gam - _bmm(jnp.concatenate([s_hi, s_lo], axis=2), jnp.concatenate([y, y], axis=1)) + u_s[j]
        state_ref[...] = s
        o2 = _bmm_nt(stacked(r_s, rows).astype(BF16), s.astype(BF16))
        o_pair = jnp.concatenate([o2[:, :CHUNK], o2[:, CHUNK:]], axis=2)
        o_s[rows, :] = jnp.concatenate([o_pair[p] for p in range(RW_PAIRS)], axis=1)
        return carry

    lax.fori_loop(0, nc, chunk_advance, 0)

    o = o_s[...]
    mu = _head_sum(o, ones_blk) * (1.0 / RWKV_DH)
    oc = o - mu
    var = _head_sum(oc * oc, ones_blk) * (1.0 / RWKV_DH)
    y = oc * lax.rsqrt(var + RWKV_GN_EPS) * gng_ref[...] + gnb_ref[...]
    o_ref[...] = ((y + bonus_s[...]) * g_s[...]).astype(o_ref.dtype)


def _rwkv(proj, small, mix, w_up, w0, a_up, a0, g_up, k_k, k_a, r_k, gn_g, gn_b):
    length = proj.shape[0]
    nc = _chunks_per_step(length, (5, 4, 2, 1))
    t = nc * CHUNK
    w = RWKV_W
    mix_s = mix[3 * w:][None]
    wup_pad = jnp.zeros((LANES, w), F32).at[:RWKV_RANK_W].set(w_up)
    aup_pad = jnp.zeros((LANES, w), F32).at[RWKV_RANK_W:].set(a_up)
    col = lambda c0: pl.BlockSpec((t, w), lambda c: (c, c0 // w))
    vec = pl.BlockSpec((1, w), lambda c: (0, 0))
    mat = pl.BlockSpec((LANES, w), lambda c: (0, 0))
    wide = lambda: pltpu.VMEM((t, w), F32)
    carry = lambda n: pltpu.VMEM((SUBLANES, n), F32)
    return pl.pallas_call(
        functools.partial(_rwkv_kernel, nc=nc),
        grid=(length // t,),
        in_specs=[col(RW_R0), col(RW_K0), col(RW_V0), pl.BlockSpec((t, 2 * LANES), lambda c: (c, 0)),
                  vec, vec, vec, pl.BlockSpec((1, 2 * LANES), lambda c: (0, 0)),
                  mat, vec, mat, vec, mat, vec, vec, vec, vec, vec],
        out_specs=pl.BlockSpec((t, w), lambda c: (c, 0)),
        out_shape=jax.ShapeDtypeStruct((length, w), BF16),
        scratch_shapes=[pltpu.VMEM((RW_PAIRS, RWKV_DH, LANES), F32),
                        carry(w), carry(w), carry(w), carry(2 * LANES),
                        wide(), wide(), wide(), wide(), wide(), wide(), wide(),
                        pltpu.VMEM((nc * SUBLANES, w), F32), wide(), wide(), wide(),
                        pltpu.VMEM((nc, RW_PAIRS, LANES, LANES), BF16),
                        pltpu.VMEM((nc, RW_PAIRS, RWKV_DH, LANES), F32)],
        compiler_params=_cparams(("arbitrary",)),
        name="rwkv7",
    )(proj, proj, proj, small, mix[None, :w], mix[None, w:2 * w], mix[None, 2 * w:3 * w], mix_s,
      wup_pad, w0[None], aup_pad, a0[None], g_up, k_k[None], k_a[None], r_k.reshape(1, w),
      gn_g[None], gn_b[None])


def _merge_kernel(oret_ref, ogla_ref, orw_ref, wret_ref, wgla_ref, wrw_ref, gret_ref, ggla_ref, grw_ref, o_ref):
    acc = _sigmoid(gret_ref[...].astype(F32)) * _dot(oret_ref[...], wret_ref[...])
    acc = acc + _sigmoid(ggla_ref[...].astype(F32)) * _dot(ogla_ref[...], wgla_ref[...])
    acc = acc + _sigmoid(grw_ref[...].astype(F32)) * _dot(orw_ref[...], wrw_ref[...])
    o_ref[...] = acc.astype(o_ref.dtype)


def _merge(o_ret, o_gla, o_rw, w_ret, w_gla, w_rw, layer, gates):
    m = o_ret.shape[0]
    tm = _pick(m, (640, 256, 128, 64))
    tn = 512
    act = lambda n: pl.BlockSpec((tm, n), lambda i, j: (i, 0))
    wgt = lambda n: pl.BlockSpec((None, n, tn), lambda i, j: (layer, 0, j))
    gate = lambda b: pl.BlockSpec((tm, tn), lambda i, j: (i, b * (D_MODEL // tn) + j))
    return pl.pallas_call(
        _merge_kernel,
        grid=(m // tm, D_MODEL // tn),
        in_specs=[act(RET_W), act(GLA_WV), act(RWKV_W), wgt(RET_W), wgt(GLA_WV), wgt(RWKV_W),
                  gate(0), gate(1), gate(2)],
        out_specs=pl.BlockSpec((tm, tn), lambda i, j: (i, j)),
        out_shape=jax.ShapeDtypeStruct((m, D_MODEL), BF16),
        compiler_params=_cparams(("parallel", "parallel")),
        name="branch_merge",
    )(o_ret, o_gla, o_rw, w_ret, w_gla, w_rw, gates, gates, gates)


def _router_kernel(h_ref, rw_ref, bias_ref, eidx_ref, ewts_ref):
    tm = h_ref.shape[0]
    (hh, hl), (wh, wl) = _split(h_ref[...]), _split(rw_ref[...])
    logits = (_dot(hh, wl) + _dot(hl, wh)) + _dot(hh, wh)
    s_all = _sigmoid(logits.T[0:N_EXPERTS])
    sel_all = s_all + bias_ref[...]
    ids = lax.broadcasted_iota(jnp.int32, (EXPERTS_PER_GROUP, tm), 0)
    neg = jnp.float32(-jnp.inf)
    best_score = best_e1 = best_e2 = best_w1 = best_w2 = None
    for gidx in range(N_GROUPS):
        rows = slice(gidx * EXPERTS_PER_GROUP, (gidx + 1) * EXPERTS_PER_GROUP)
        sel, s = sel_all[rows], s_all[rows]
        m1 = jnp.max(sel, axis=0, keepdims=True)
        i1 = jnp.min(jnp.where(sel == m1, ids, EXPERTS_PER_GROUP), axis=0, keepdims=True)
        rest = jnp.where(ids == i1, neg, sel)
        m2 = jnp.max(rest, axis=0, keepdims=True)
        i2 = jnp.min(jnp.where(rest == m2, ids, EXPERTS_PER_GROUP), axis=0, keepdims=True)
        score = m1 + m2
        w1 = jnp.sum(jnp.where(ids == i1, s, 0.0), axis=0, keepdims=True)
        w2 = jnp.sum(jnp.where(ids == i2, s, 0.0), axis=0, keepdims=True)
        e1 = i1 + gidx * EXPERTS_PER_GROUP
        e2 = i2 + gidx * EXPERTS_PER_GROUP
        if gidx == 0:
            best_score, best_e1, best_e2, best_w1, best_w2 = score, e1, e2, w1, w2
        else:
            take = score > best_score
            best_score = jnp.where(take, score, best_score)
            best_e1, best_e2 = jnp.where(take, e1, best_e1), jnp.where(take, e2, best_e2)
            best_w1, best_w2 = jnp.where(take, w1, best_w1), jnp.where(take, w2, best_w2)
    total = best_w1 + best_w2
    slot = lax.broadcasted_iota(jnp.int32, (SUBLANES, tm), 0)
    eidx_ref[...] = jnp.where(slot == 0, best_e1, jnp.where(slot == 1, best_e2, 0))
    ewts_ref[...] = jnp.where(slot == 0, best_w1 / total, jnp.where(slot == 1, best_w2 / total, 0.0))


def _router(h, router_w, router_bias):
    m = h.shape[0]
    tm = _pick(m, (256, 128))
    rw_pad = jnp.zeros((D_MODEL, LANES), F32).at[:, :N_EXPERTS].set(router_w)
    return pl.pallas_call(
        _router_kernel,
        grid=(m // tm,),
        in_specs=[pl.BlockSpec((tm, D_MODEL), lambda i: (i, 0)),
                  pl.BlockSpec((D_MODEL, LANES), lambda i: (0, 0)),
                  pl.BlockSpec((N_EXPERTS, 1), lambda i: (0, 0))],
        out_specs=(pl.BlockSpec((SUBLANES, tm), lambda i: (0, i)), pl.BlockSpec((SUBLANES, tm), lambda i: (0, i))),
        out_shape=(jax.ShapeDtypeStruct((SUBLANES, m), jnp.int32), jax.ShapeDtypeStruct((SUBLANES, m), F32)),
        compiler_params=_cparams(("parallel",)),
        name="router",
    )(h, rw_pad, router_bias[:, None])


def _row_copy(src_hbm, row, dst_buf, sem):
    return pltpu.make_async_copy(src_hbm.at[pl.ds(row, 1)], dst_buf, sem)


def _ffn_kernel(te_ref, rt_ref, na_ref, h_hbm, w1_ref, w3_ref, w2_ref, y_ref, xbuf, sem, *, tm):
    del te_ref
    i = pl.program_id(0)
    n = pl.num_programs(0)
    n_active = na_ref[0]
    ahead = FFN_SLOTS - 1
    slot = i % FFN_SLOTS

    def start_tile(tile, dst_slot):
        for r in range(tm):
            _row_copy(h_hbm, rt_ref[tile * tm + r], xbuf.at[dst_slot, pl.ds(r, 1)],
                      sem.at[dst_slot]).start(priority=r % DMA_THREADS)

    def wait_tile(dst_slot):
        def body(r, c):
            _row_copy(h_hbm, 0, xbuf.at[dst_slot, pl.ds(r, 1)], sem.at[dst_slot]).wait()
            return c
        lax.fori_loop(0, tm, body, 0, unroll=DMA_UNROLL)

    @pl.when(i == 0)
    def _():
        for t in range(ahead):
            start_tile(jnp.minimum(t, n - 1), t)

    wait_tile(slot)
    nxt = jnp.minimum(i + ahead, n - 1)
    nxt_slot = (i + ahead) % FFN_SLOTS

    @pl.when(i < n_active)
    def _():
        start_tile(nxt, nxt_slot)
        x = _unpack_halves(xbuf[slot]).astype(BF16)
        h1 = _dot(x, w1_ref[...])
        h3 = _dot(x, w3_ref[...])
        he = (h1 * _sigmoid(h1)) * h3
        y_ref[...] = _pack_halves(_dot(he.astype(BF16), w2_ref[...]))

    @pl.when(i >= n_active)
    def _():
        start_tile(nxt, nxt_slot)
        y_ref[...] = jnp.zeros_like(y_ref)

    @pl.when(i == n - 1)
    def _():
        for t in range(1, FFN_SLOTS):
            wait_tile((i + t) % FFN_SLOTS)


def _expert_ffn(hp, tile_expert, row_token, n_active, layer, w1, w3, w2):
    half = hp.shape[1]
    d = 2 * half
    tm = MOE_TM
    n_tiles = tile_expert.shape[0]
    wspec = lambda shape: pl.BlockSpec((None, None) + shape, lambda i, te, rt, na: (layer, te[i], 0, 0))
    return pl.pallas_call(
        functools.partial(_ffn_kernel, tm=tm),
        grid_spec=pltpu.PrefetchScalarGridSpec(
            num_scalar_prefetch=3,
            grid=(n_tiles,),
            in_specs=[pl.BlockSpec(memory_space=pl.ANY), wspec((d, D_FF_PAD)), wspec((d, D_FF_PAD)),
                      wspec((D_FF_PAD, d))],
            out_specs=pl.BlockSpec((tm, half), lambda i, te, rt, na: (i, 0)),
            scratch_shapes=[pltpu.VMEM((FFN_SLOTS, tm, half), jnp.uint32), pltpu.SemaphoreType.DMA((FFN_SLOTS,))]),
        out_shape=jax.ShapeDtypeStruct((n_tiles * tm, half), jnp.uint32),
        compiler_params=_cparams(("arbitrary",)),
        name="expert_ffn",
    )(tile_expert, row_token, n_active, hp, w1, w3, w2)


def _combine_ln_kernel(p1_ref, p2_ref, h_ref, y_hbm, w_ref, g_ref, b_ref, o_ref, ob_ref, ybuf, sem, *, tm, n_mask):
    i = pl.program_id(0)
    n = pl.num_programs(0)
    slot = i % 2

    def start_tile(tile, dst_slot):
        for r in range(tm):
            _row_copy(y_hbm, p1_ref[tile * tm + r], ybuf.at[dst_slot, 0, pl.ds(r, 1)], sem.at[dst_slot]).start(priority=0)
            _row_copy(y_hbm, p2_ref[tile * tm + r], ybuf.at[dst_slot, 1, pl.ds(r, 1)],
                      sem.at[dst_slot]).start(priority=DMA_THREADS - 1)

    def wait_tile(dst_slot):
        def body(r, c):
            _row_copy(y_hbm, 0, ybuf.at[dst_slot, 0, pl.ds(r, 1)], sem.at[dst_slot]).wait()
            _row_copy(y_hbm, 0, ybuf.at[dst_slot, 1, pl.ds(r, 1)], sem.at[dst_slot]).wait()
            return c
        lax.fori_loop(0, tm, body, 0, unroll=DMA_UNROLL)

    @pl.when(i == 0)
    def _():
        start_tile(0, 0)

    wait_tile(slot)
    start_tile(jnp.minimum(i + 1, n - 1), 1 - slot)
    w = w_ref[...]
    ffn = w[:, 0:1] * _unpack_halves(ybuf[slot, 0]) + w[:, 1:2] * _unpack_halves(ybuf[slot, 1])
    x = h_ref[...] * DN_ALPHA + ffn
    _emit_ln(_layernorm_rows(x, g_ref[...], b_ref[...]), o_ref, ob_ref, n_mask, tm)

    @pl.when(i == n - 1)
    def _():
        wait_tile(1 - slot)


def _combine_ln(h, y, pos1, pos2, wts, g, b, n_mask, frames_only):
    m, d = h.shape
    tm = FRONT if frames_only else _pick(m, (256, 128, 64))
    row = pl.BlockSpec((tm, d), lambda i, p1, p2: (i, 0))
    vec = pl.BlockSpec((1, d), lambda i, p1, p2: (0, 0))
    if frames_only:
        out_rows, out_row = m - tm, pl.BlockSpec((tm, d), lambda i, p1, p2: (jnp.maximum(i - 1, 0), 0))
    else:
        out_rows, out_row = m, row
    return pl.pallas_call(
        functools.partial(_combine_ln_kernel, tm=tm, n_mask=n_mask),
        grid_spec=pltpu.PrefetchScalarGridSpec(
            num_scalar_prefetch=2,
            grid=(m // tm,),
            in_specs=[row, pl.BlockSpec(memory_space=pl.ANY),
                      pl.BlockSpec((tm, 2), lambda i, p1, p2: (i, 0)), vec, vec],
            out_specs=(out_row, row),
            scratch_shapes=[pltpu.VMEM((2, 2, tm, d // 2), jnp.uint32), pltpu.SemaphoreType.DMA((2,))]),
        out_shape=(jax.ShapeDtypeStruct((out_rows, d), F32), jax.ShapeDtypeStruct((m, d), BF16)),
        compiler_params=_cparams(("arbitrary",)),
        name="moe_combine_layernorm",
    )(pos1, pos2, h, y, wts, g[None], b[None])


def _dispatch_plan(eidx, tm):
    m = eidx.shape[1]
    flat_e = jnp.concatenate([eidx[0], eidx[1]])
    tok = jnp.arange(m, dtype=jnp.int32)
    flat_t = jnp.concatenate([tok, tok])
    onehot = (flat_e[:, None] == jnp.arange(N_EXPERTS, dtype=jnp.int32)[None, :]).astype(jnp.int32)
    rank = jnp.take_along_axis(jnp.cumsum(onehot, axis=0) - onehot, flat_e[:, None], axis=1)[:, 0]
    counts = jnp.sum(onehot, axis=0)
    padded = ((counts + tm - 1) // tm) * tm
    ends = jnp.cumsum(padded)
    offs = ends - padded
    pos = (offs[flat_e] + rank).astype(jnp.int32)
    n_tiles = (2 * m) // tm + N_EXPERTS
    row_token = jnp.zeros((n_tiles * tm,), jnp.int32).at[pos].set(flat_t)
    n_active = (ends[-1] // tm).astype(jnp.int32)
    tile_id = jnp.arange(n_tiles, dtype=jnp.int32)
    tile_id = jnp.minimum(tile_id, n_active - 1)
    tile_expert = (jnp.sum(tile_id[:, None] >= (offs // tm)[None, :], axis=1) - 1).astype(jnp.int32)
    return tile_expert, row_token, n_active.reshape(1), pos[:m], pos[m:]


def _moe_ln(h, hp, router_w, router_bias, layer, w1, w3, w2, g, b, n_mask, frames_only=False):
    eidx, ewts = _router(h, router_w, router_bias)
    tile_expert, row_token, n_active, pos1, pos2 = _dispatch_plan(eidx, MOE_TM)
    y = _expert_ffn(hp, tile_expert, row_token, n_active, layer, w1, w3, w2)
    return _combine_ln(h, y, pos1, pos2, ewts[0:2].T, g, b, n_mask, frames_only)


def _cast_pad_kernel(x_ref, o_ref):
    rows, cols = x_ref.shape
    o_ref[...] = jnp.zeros(o_ref.shape, o_ref.dtype)
    o_ref[0:rows, 0:cols] = x_ref[...].astype(o_ref.dtype)


def _cast_pad(w, rows_out, cols_out, block_rows, block_cols):
    n_l, n_e, r, c = w.shape
    in_rows = r if rows_out != r else block_rows
    in_cols = c if cols_out != c else block_cols
    out_rows = rows_out if rows_out != r else block_rows
    out_cols = cols_out if cols_out != c else block_cols
    grid = (n_l * n_e, rows_out // out_rows, cols_out // out_cols)
    idx = lambda e, i, j: (e // n_e, e % n_e, i, j)
    return pl.pallas_call(
        _cast_pad_kernel,
        grid=grid,
        in_specs=[pl.BlockSpec((None, None, in_rows, in_cols), idx)],
        out_specs=pl.BlockSpec((None, None, out_rows, out_cols), idx),
        out_shape=jax.ShapeDtypeStruct((n_l, n_e, rows_out, cols_out), BF16),
        compiler_params=_cparams(("parallel", "parallel", "parallel")),
        name="cast_pad_weights",
    )(w)


def _cast_pad_transposed_kernel(x_ref, o_ref):
    rows, cols = x_ref.shape
    x = jnp.concatenate([x_ref[...], jnp.zeros((o_ref.shape[1] - rows, cols), x_ref.dtype)], axis=0)
    o_ref[...] = x.T.astype(o_ref.dtype)


def _cast_pad_transposed(wt, cols_out, block):
    n_l, n_e, c, r = wt.shape
    return pl.pallas_call(
        _cast_pad_transposed_kernel,
        grid=(n_l * n_e, r // block),
        in_specs=[pl.BlockSpec((None, None, c, block), lambda e, i: (e // n_e, e % n_e, 0, i))],
        out_specs=pl.BlockSpec((None, None, block, cols_out), lambda e, i: (e // n_e, e % n_e, i, 0)),
        out_shape=jax.ShapeDtypeStruct((n_l, n_e, r, cols_out), BF16),
        compiler_params=_cparams(("parallel", "parallel")),
        name="cast_pad_weights_t",
    )(wt)


def _prep_experts(w1, w3, w2):
    d = w1.shape[2]
    w1 = _cast_pad_transposed(jnp.swapaxes(w1, 2, 3), D_FF_PAD, 1024)
    w3 = _cast_pad_transposed(jnp.swapaxes(w3, 2, 3), D_FF_PAD, 1024)
    w2 = _cast_pad(w2, D_FF_PAD, d, D_FF_PAD, 2048)
    return w1, w3, w2


def _rotary_tables(length):
    half = RET_DK // 2
    pos = jnp.arange(length, dtype=jnp.int32) - N_MASK
    inv = ROPE_BASE ** (-jnp.arange(half, dtype=F32) / half)
    ang = pos.astype(F32)[:, None] * inv[None, :]
    cos, sin = jnp.cos(ang), jnp.sin(ang)
    return jnp.concatenate([cos, cos], axis=1), jnp.concatenate([-sin, sin], axis=1)


def _mixer(hb, cos, sin, layer, w_in, ret_gn_g, ret_gn_b, gla_a_up, gla_a_bias, gla_gn_g, gla_gn_b, rw_mix, rw_w_up,
           rw_w0, rw_a_up, rw_a0, rw_g_up, rw_k_k, rw_k_a, rw_r_k, rw_gn_g, rw_gn_b, w_br_ret, w_br_gla,
           w_br_rw, w_out):
    cols = lambda a, b: lax.slice(w_in, (layer, 0, a), (layer + 1, D_MODEL, b))[0]
    w_small = jnp.concatenate([cols(W_IN_SM0, W_IN_GATE0), cols(W_IN_GA0, W_IN_RW0),
                               jnp.zeros((D_MODEL, LANES - GLA_RANK), BF16)], axis=1)
    proj = _matmul(hb, w_in, BF16, 512, layer=layer, n=N_HEAD)
    proj_rw = _matmul(hb, cols(W_IN_RW0, W_IN_SM0), BF16, 512)
    gates = _matmul(hb, cols(W_IN_GATE0, w_in.shape[2]), BF16, 512)
    small = _matmul(hb, w_small, F32, N_SMALL)
    o_ret = _retention(proj, cos, sin, ret_gn_g, ret_gn_b)
    o_gla = _gla(proj, small, gla_a_up, gla_a_bias, gla_gn_g, gla_gn_b)
    o_rw = _rwkv(proj_rw, small, rw_mix, rw_w_up, rw_w0, rw_a_up, rw_a0, rw_g_up, rw_k_k, rw_k_a, rw_r_k,
                 rw_gn_g, rw_gn_b)
    merged = _merge(o_ret, o_gla, o_rw, w_br_ret, w_br_gla, w_br_rw, layer, gates)
    return _matmul(merged, w_out, F32, 512, layer=layer)


def kernel(x, meta, ln_in_g, ln_in_b, w_in, ret_gn_g, ret_gn_b, gla_a_up, gla_a_bias, gla_gn_g, gla_gn_b, rw_mix, rw_w_up, rw_w0, rw_a_up, rw_a0, rw_g_up, rw_k_k, rw_k_a, rw_r_k, rw_gn_g, rw_gn_b, w_br_ret, w_br_gla, w_br_rw, w_out, ln1_g, ln1_b, router_w, router_bias, exp_w1, exp_w3, exp_w2, ln2_g, ln2_b):
    batch, seq, d = x.shape
    assert batch == 1 and d == D_MODEL
    assert meta.shape == (N_META, d)
    cos, sin = _rotary_tables(FRONT + seq)
    h, hb = _embed_layer_norm(x[0], meta.astype(x.dtype), ln_in_g, ln_in_b)
    w_in, w_out = w_in.astype(BF16), w_out.astype(BF16)
    w_br_ret, w_br_gla, w_br_rw = w_br_ret.astype(BF16), w_br_gla.astype(BF16), w_br_rw.astype(BF16)
    w1, w3, w2 = _prep_experts(exp_w1, exp_w3, exp_w2)
    for l in range(DEPTH):
        mix = _mixer(hb, cos, sin, l, w_in, ret_gn_g[l], ret_gn_b[l], gla_a_up[l], gla_a_bias[l], gla_gn_g[l],
                     gla_gn_b[l], rw_mix[l], rw_w_up[l], rw_w0[l], rw_a_up[l], rw_a0[l], rw_g_up[l], rw_k_k[l],
                     rw_k_a[l], rw_r_k[l], rw_gn_g[l], rw_gn_b[l], w_br_ret, w_br_gla, w_br_rw, w_out)
        h, hp = _layer_norm(h, mix, ln1_g[l], ln1_b[l], 0, packed=True)
        h, hb = _moe_ln(h, hp, router_w, router_bias, l, w1, w3, w2, ln2_g[l], ln2_b[l], N_MASK,
                        frames_only=(l == DEPTH - 1))
    return h[None]
```

```python
import functools

import jax
import jax.numpy as jnp
from jax import lax
from jax.experimental import pallas as pl
from jax.experimental.pallas import tpu as pltpu

F32 = jnp.float32
BF16 = jnp.bfloat16

D_MODEL = 4096
DEPTH = 2
CHUNK = 64
N_META = 16
N_PAD = CHUNK - N_META

RET_HEADS, RET_DK, RET_DV = 12, 128, 128
RET_W = RET_HEADS * RET_DV
ROPE_BASE = 10000.0

GLA_HEADS, GLA_DK, GLA_DV = 6, 128, 256
GLA_WK = GLA_HEADS * GLA_DK
GLA_WV = GLA_HEADS * GLA_DV
GLA_RANK = 16
GLA_TAU = 16.0

RWKV_HEADS, RWKV_DH = 16, 64
RWKV_W = RWKV_HEADS * RWKV_DH
RWKV_RANK_W, RWKV_RANK_A, RWKV_RANK_G = 64, 64, 128

N_EXPERTS = 32
N_GROUPS = 4
EXPERTS_PER_GROUP = N_EXPERTS // N_GROUPS
D_FF = 704

DN_ALPHA = (2 * DEPTH) ** 0.25
LN_EPS = 1e-5
GN_EPS = 1e-5
RWKV_GN_EPS = 64e-5
NORM_EPS = 1e-12

LANES = 128
SUBLANES = 8
VMEM_LIMIT_BYTES = 56 * 1024 * 1024

RET_Q0, RET_K0, RET_V0, RET_G0 = 0, RET_W, 2 * RET_W, 3 * RET_W
GLA_Q0 = 4 * RET_W
GLA_K0 = GLA_Q0 + GLA_WK
GLA_V0 = GLA_K0 + GLA_WK
GLA_G0 = GLA_V0 + GLA_WV
N_HEAD = GLA_G0 + GLA_WV
W_IN_GA0 = N_HEAD
W_IN_RW0 = W_IN_GA0 + GLA_RANK
W_IN_SM0 = W_IN_RW0 + 3 * RWKV_W
W_IN_GATE0 = W_IN_SM0 + RWKV_RANK_W + RWKV_RANK_A + RWKV_RANK_G
RW_R0, RW_K0, RW_V0 = 0, RWKV_W, 2 * RWKV_W
N_SMALL = 3 * LANES
D_FF_PAD = 768

ROW_TILE = 256
FRONT = ROW_TILE
N_MASK = FRONT - N_META
MOE_TM = 256
DMA_UNROLL = 8
DMA_THREADS = 2
FFN_SLOTS = 3


def _cparams(sem):
    return pltpu.CompilerParams(dimension_semantics=sem, vmem_limit_bytes=VMEM_LIMIT_BYTES)


def _pick(n, cands):
    for c in cands:
        if n % c == 0:
            return c
    raise ValueError(f"no tile in {cands} divides {n}")


def _dot(a, b):
    return jnp.dot(a, b, preferred_element_type=F32)


def _split(x):
    hi = x.astype(BF16)
    return hi, (x - hi.astype(F32)).astype(BF16)


def _split3(x):
    hi = x.astype(BF16)
    rest = x - hi.astype(F32)
    mid = rest.astype(BF16)
    return hi, mid, (rest - mid.astype(F32)).astype(BF16)


def _dot_hi(a, b):
    (ah, al), (bh, bl) = _split(a), _split(b)
    return _dot(jnp.concatenate([ah, ah, al], axis=1), jnp.concatenate([bh, bl, bh], axis=0))


def _dot_sel_rhs(a, sel):
    return _dot(jnp.concatenate(_split3(a), axis=1), jnp.concatenate([sel, sel, sel], axis=0))


def _dot_sel_lhs(sel, b):
    return _dot(jnp.concatenate([sel, sel, sel], axis=1), jnp.concatenate(_split3(b), axis=0))


def _dot_nt(a, b):
    return lax.dot_general(a, b, (((1,), (1,)), ((), ())), preferred_element_type=F32)


def _dot_tn(a, b):
    return lax.dot_general(a, b, (((0,), (0,)), ((), ())), preferred_element_type=F32)


def _sigmoid(x):
    return 1.0 / (1.0 + jnp.exp(-x))


def _softplus(x):
    return jnp.maximum(x, 0.0) + jnp.log(1.0 + jnp.exp(-jnp.abs(x)))


def _mm_kernel(x_ref, w_ref, o_ref):
    o_ref[...] = _dot(x_ref[...], w_ref[...]).astype(o_ref.dtype)


def _matmul(x, w, out_dtype, tn, layer=None, n=None):
    m, k = x.shape
    n = w.shape[-1] if n is None else n
    assert n % tn == 0
    tm = _pick(m, (1280, 640, 256, 128, 64))
    if w.ndim == 3:
        w_spec = pl.BlockSpec((None, k, tn), lambda i, j: (layer, 0, j))
    else:
        w_spec = pl.BlockSpec((k, tn), lambda i, j: (0, j))
    return pl.pallas_call(
        _mm_kernel,
        grid=(m // tm, n // tn),
        in_specs=[pl.BlockSpec((tm, k), lambda i, j: (i, 0)), w_spec],
        out_specs=pl.BlockSpec((tm, tn), lambda i, j: (i, j)),
        out_shape=jax.ShapeDtypeStruct((m, n), out_dtype),
        compiler_params=_cparams(("parallel", "parallel")),
        name="dense_matmul",
    )(x, w)


def _mm_nt_kernel(x_ref, wt_ref, o_ref, wb_ref):
    @pl.when(pl.program_id(1) == 0)
    def _():
        wb_ref[...] = wt_ref[0].astype(BF16)

    o_ref[...] = _dot_nt(x_ref[...], wb_ref[...]).astype(o_ref.dtype)


def _matmul_nt(x, wt, layer, row0, n, out_dtype, tn=512):
    m, k = x.shape
    assert n % tn == 0 and row0 % 16 == 0 and wt.shape[2] == k
    tm = _pick(m, (1280, 640, 256, 128, 64))
    wt_spec = pl.BlockSpec((pl.Element(1), pl.Element(tn), pl.Element(k)),
                           lambda j, i: (layer, pl.multiple_of(row0 + j * tn, 16), 0))
    return pl.pallas_call(
        _mm_nt_kernel,
        grid=(n // tn, m // tm),
        in_specs=[pl.BlockSpec((tm, k), lambda j, i: (i, 0)), wt_spec],
        out_specs=pl.BlockSpec((tm, tn), lambda j, i: (i, j)),
        out_shape=jax.ShapeDtypeStruct((m, n), out_dtype),
        scratch_shapes=[pltpu.VMEM((tn, k), BF16)],
        compiler_params=_cparams(("arbitrary", "arbitrary")),
        name="dense_matmul_nt",
    )(x, wt)


def _layernorm_rows(x, g, b):
    mu = jnp.mean(x, axis=-1, keepdims=True)
    xc = x - mu
    var = jnp.mean(xc * xc, axis=-1, keepdims=True)
    return xc * lax.rsqrt(var + LN_EPS) * g + b


HIGH_HALF = 0xFFFF0000


def _pack_halves(y):
    half = y.shape[1] // 2
    lo = pltpu.bitcast(y[:, :half].astype(BF16).astype(F32), jnp.uint32) >> 16
    hi = pltpu.bitcast(y[:, half:].astype(BF16).astype(F32), jnp.uint32) & jnp.uint32(HIGH_HALF)
    return lo | hi


def _unpack_halves(w):
    lo = pltpu.bitcast(w << 16, F32)
    hi = pltpu.bitcast(w & jnp.uint32(HIGH_HALF), F32)
    return jnp.concatenate([lo, hi], axis=1)


def _emit_ln(y, o_ref, ob_ref, n_mask, tm):
    o_ref[...] = y
    if ob_ref.dtype == jnp.uint32:
        ob_ref[...] = _pack_halves(y)
        return
    yb = y.astype(BF16)
    if n_mask:
        row = pl.program_id(0) * tm + lax.broadcasted_iota(jnp.int32, (tm, 1), 0)
        yb = jnp.where(row >= n_mask, yb, jnp.zeros_like(yb))
    ob_ref[...] = yb


def _res_ln_kernel(h_ref, f_ref, g_ref, b_ref, o_ref, ob_ref, *, alpha, n_mask, tm):
    x = h_ref[...] * alpha + f_ref[...]
    _emit_ln(_layernorm_rows(x, g_ref[...], b_ref[...]), o_ref, ob_ref, n_mask, tm)


def _embed_ln_kernel(x_ref, meta_ref, g_ref, b_ref, o_ref, ob_ref, *, n_mask, tm):
    front = jnp.concatenate([jnp.zeros((n_mask, x_ref.shape[1]), F32), meta_ref[...]], axis=0)
    x = jnp.where(pl.program_id(0) == 0, front, x_ref[...])
    _emit_ln(_layernorm_rows(x, g_ref[...], b_ref[...]), o_ref, ob_ref, n_mask, tm)


def _embed_layer_norm(x, meta, g, b):
    seq, d = x.shape
    tm = FRONT
    assert seq % tm == 0 and N_MASK + meta.shape[0] == tm
    m = tm + seq
    row = pl.BlockSpec((tm, d), lambda i: (i, 0))
    vec = pl.BlockSpec((1, d), lambda i: (0, 0))
    return pl.pallas_call(
        functools.partial(_embed_ln_kernel, n_mask=N_MASK, tm=tm),
        grid=(m // tm,),
        in_specs=[pl.BlockSpec((tm, d), lambda i: (jnp.maximum(i - 1, 0), 0)),
                  pl.BlockSpec(meta.shape, lambda i: (0, 0)), vec, vec],
        out_specs=(row, row),
        out_shape=(jax.ShapeDtypeStruct((m, d), F32), jax.ShapeDtypeStruct((m, d), BF16)),
        compiler_params=_cparams(("parallel",)), name="embed_layernorm",
    )(x, meta, g[None], b[None])


def _layer_norm(h, f, g, b, n_mask, packed=False):
    m, d = h.shape
    tm = _pick(m, (256, 128, 64))
    row = pl.BlockSpec((tm, d), lambda i: (i, 0))
    vec = pl.BlockSpec((1, d), lambda i: (0, 0))
    if packed:
        narrow, narrow_spec = jax.ShapeDtypeStruct((m, d // 2), jnp.uint32), pl.BlockSpec((tm, d // 2), lambda i: (i, 0))
    else:
        narrow, narrow_spec = jax.ShapeDtypeStruct((m, d), BF16), row
    return pl.pallas_call(
        functools.partial(_res_ln_kernel, alpha=DN_ALPHA, n_mask=n_mask, tm=tm),
        grid=(m // tm,), in_specs=[row, row, vec, vec], out_specs=(row, narrow_spec),
        out_shape=(jax.ShapeDtypeStruct((m, d), F32), narrow),
        compiler_params=_cparams(("parallel",)), name="res_layernorm",
    )(h, f, g[None], b[None])


def _groupnorm_lanes(o, g, b, eps):
    mu = jnp.mean(o, axis=-1, keepdims=True)
    oc = o - mu
    var = jnp.mean(oc * oc, axis=-1, keepdims=True)
    return oc * lax.rsqrt(var + eps) * g + b


def _chunks_per_step(length, cands):
    return _pick(length // CHUNK, cands)


def _by_head(x, n_heads, width):
    return jnp.stack([x[:, h * width:(h + 1) * width] for h in range(n_heads)], axis=0)


def _heads_to_lanes(x):
    return jnp.concatenate([x[h] for h in range(x.shape[0])], axis=1)


def _ret_kernel(q_ref, k_ref, v_ref, g_ref, cos_ref, sin_ref, intra_ref, qdec_ref, kdec_ref, cdec_ref,
                gng_ref, gnb_ref, o_ref, state_ref, *, nc):
    @pl.when(pl.program_id(0) == 0)
    def _():
        state_ref[...] = jnp.zeros_like(state_ref)

    cos, sin = cos_ref[...], sin_ref[...]
    half = RET_DK // 2

    def rotated(ref):
        x = ref[...].astype(F32)
        return jnp.stack([xh * cos + pltpu.roll(xh, half, 1) * sin
                          for xh in (x[:, h * RET_DK:(h + 1) * RET_DK] for h in range(RET_HEADS))], axis=0)

    q = rotated(q_ref) * (RET_DK ** -0.5)
    k = rotated(k_ref)
    v = _by_head(v_ref[...], RET_HEADS, RET_DV)
    intra, qdec, kdec = intra_ref[...], qdec_ref[...], kdec_ref[...]
    cdec = cdec_ref[...][:, 0:1]
    state = state_ref[...]
    outs = []
    for j in range(nc):
        sl = slice(j * CHUNK, (j + 1) * CHUNK)
        qc, kc, vc = q[:, sl], k[:, sl], v[:, sl]
        s = _bmm_nt(qc.astype(BF16), kc.astype(BF16)) * intra
        outs.append(_bmm(s.astype(BF16), vc) + _bmm((qc * qdec).astype(BF16), state.astype(BF16)))
        state = state * cdec + _bmm_tn((kc * kdec).astype(BF16), vc)
    state_ref[...] = state
    y = _groupnorm_lanes(jnp.concatenate(outs, axis=1), gng_ref[...], gnb_ref[...], GN_EPS)
    gate = _by_head(g_ref[...].astype(F32), RET_HEADS, RET_DV)
    o_ref[...] = _heads_to_lanes(y * (gate * _sigmoid(gate))).astype(o_ref.dtype)


def _retention(proj, cos, sin, gn_g, gn_b):
    length = proj.shape[0]
    nc = _chunks_per_step(length, (5, 4, 2, 1))
    t = nc * CHUNK
    lg = jnp.log1p(-jnp.exp2(-5.0 - jnp.arange(RET_HEADS, dtype=F32)))
    i = jnp.arange(CHUNK, dtype=F32)
    intra = jnp.exp(lg[:, None, None] * jnp.abs(i[:, None] - i[None, :]))
    qdec = jnp.broadcast_to(jnp.exp(lg[:, None] * (i[None, :] + 1.0))[:, :, None], (RET_HEADS, CHUNK, RET_DK))
    kdec = jnp.broadcast_to(jnp.exp(lg[:, None] * (CHUNK - 1.0 - i[None, :]))[:, :, None], (RET_HEADS, CHUNK, RET_DK))
    cdec = jnp.broadcast_to(jnp.exp(lg * CHUNK)[:, None, None], (RET_HEADS, SUBLANES, RET_DV))
    col = lambda c0: pl.BlockSpec((t, RET_W), lambda c: (c, c0 // RET_W))
    tab = pl.BlockSpec((t, RET_DK), lambda c: (c, 0))
    whole = lambda a: pl.BlockSpec(a.shape, lambda c: (0,) * a.ndim)
    gng, gnb = gn_g.reshape(RET_HEADS, 1, RET_DV), gn_b.reshape(RET_HEADS, 1, RET_DV)
    consts = (intra, qdec, kdec, cdec, gng, gnb)
    return pl.pallas_call(
        functools.partial(_ret_kernel, nc=nc),
        grid=(length // t,),
        in_specs=[col(RET_Q0), col(RET_K0), col(RET_V0), col(RET_G0), tab, tab] + [whole(a) for a in consts],
        out_specs=pl.BlockSpec((t, RET_W), lambda c: (c, 0)),
        out_shape=jax.ShapeDtypeStruct((length, RET_W), BF16),
        scratch_shapes=[pltpu.VMEM((RET_HEADS, RET_DK, RET_DV), F32)],
        compiler_params=_cparams(("arbitrary",)),
        name="retention",
    )(proj, proj, proj, proj, cos, sin, *consts)


def _gla_kernel(q_ref, k_ref, v_ref, g_ref, ga_ref, aup_ref, ab_ref, gng_ref, gnb_ref, o_ref, state_ref, *, nc):
    @pl.when(pl.program_id(0) == 0)
    def _():
        state_ref[...] = jnp.zeros_like(state_ref)

    z = _dot_hi(ga_ref[...], aup_ref[...]) + ab_ref[...]
    log_a = -_softplus(-z) / GLA_TAU
    q = _by_head((q_ref[...].astype(F32) * (GLA_DK ** -0.5)).astype(BF16), GLA_HEADS, GLA_DK)
    k = k_ref[...].astype(F32)
    v = _by_head(v_ref[...], GLA_HEADS, GLA_DV)
    row = lax.broadcasted_iota(jnp.int32, (CHUNK, CHUNK), 0)
    colm = lax.broadcasted_iota(jnp.int32, (CHUNK, CHUNK), 1)
    later = (colm > row).astype(BF16)
    state = state_ref[...]
    outs = []
    for j in range(nc):
        sl = slice(j * CHUNK, (j + 1) * CHUNK)
        la = log_a[sl]
        to_end = _dot_sel_lhs(later, la)
        dec = _by_head(jnp.exp(to_end[0:1] + la[0:1]), GLA_HEADS, GLA_DK)
        kc = _by_head((k[sl] * jnp.exp(to_end)).astype(BF16), GLA_HEADS, GLA_DK)
        state = state * dec + _bmm_tn(v[:, sl], kc)
        outs.append(_bmm_nt(q[:, sl], state.astype(BF16)))
    state_ref[...] = state
    y = _groupnorm_lanes(jnp.concatenate(outs, axis=1), gng_ref[...], gnb_ref[...], GN_EPS)
    gate = _by_head(g_ref[...].astype(F32), GLA_HEADS, GLA_DV)
    o_ref[...] = _heads_to_lanes(y * (gate * _sigmoid(gate))).astype(o_ref.dtype)


def _gla(proj, small, a_up, a_bias, gn_g, gn_b):
    length = proj.shape[0]
    nc = _chunks_per_step(length, (5, 4, 2, 1))
    t = nc * CHUNK
    a_up_pad = jnp.zeros((LANES, GLA_WK), F32).at[:GLA_RANK].set(a_up)
    kcol = lambda c0: pl.BlockSpec((t, GLA_WK), lambda c: (c, c0 // GLA_WK))
    vcol = lambda c0: pl.BlockSpec((t, GLA_WV), lambda c: (c, c0 // GLA_WV))
    whole = lambda a: pl.BlockSpec(a.shape, lambda c: (0,) * a.ndim)
    consts = (a_up_pad, a_bias[None], gn_g.reshape(GLA_HEADS, 1, GLA_DV), gn_b.reshape(GLA_HEADS, 1, GLA_DV))
    return pl.pallas_call(
        functools.partial(_gla_kernel, nc=nc),
        grid=(length // t,),
        in_specs=[kcol(GLA_Q0), kcol(GLA_K0), vcol(GLA_V0), vcol(GLA_G0),
                  pl.BlockSpec((t, LANES), lambda c: (c, 2))] + [whole(a) for a in consts],
        out_specs=pl.BlockSpec((t, GLA_WV), lambda c: (c, 0)),
        out_shape=jax.ShapeDtypeStruct((length, GLA_WV), BF16),
        scratch_shapes=[pltpu.VMEM((GLA_HEADS, GLA_DV, GLA_DK), F32)],
        compiler_params=_cparams(("arbitrary",)),
        name="gla",
    )(proj, proj, proj, proj, small, *consts)


RW_PAIRS = RWKV_HEADS // 2


def _head_sum(x, ones_blk):
    parts = [_dot_sel_rhs(x[:, p * LANES:(p + 1) * LANES], ones_blk) for p in range(RW_PAIRS)]
    return jnp.concatenate(parts, axis=1)


def _bmm(a, b):
    return jnp.einsum('pik,pkj->pij', a, b, preferred_element_type=F32)


def _bmm_nt(a, b):
    return jnp.einsum('pik,pjk->pij', a, b, preferred_element_type=F32)


def _bmm_tn(a, b):
    return jnp.einsum('pki,pkj->pij', a, b, preferred_element_type=F32)


def _bmm3(a, b):
    (ah, al), (bh, bl) = _split(a), _split(b)
    return _bmm(jnp.concatenate([ah, ah, al], axis=2), jnp.concatenate([bh, bl, bh], axis=1))


def _bmm3_tn(a, b):
    (ah, al), (bh, bl) = _split(a), _split(b)
    return _bmm_tn(jnp.concatenate([ah, ah, al], axis=1), jnp.concatenate([bh, bl, bh], axis=1))


def _unit_lower_inverse_minus_eye(m, i0, i1):
    def joins(s):
        return jnp.logical_and(i0 // (2 * s) == i1 // (2 * s),
                               jnp.logical_and((i0 // s) % 2 == 1, (i1 // s) % 2 == 0))

    xp = -jnp.where(joins(1), m, 0.0)
    s = 2
    while s < CHUNK:
        c = jnp.where(joins(s), m, 0.0)
        y = c + _bmm3(xp, c)
        xp = xp - (y + _bmm3(y, xp))
        s *= 2
    return xp


def _rwkv_kernel(r_ref, k_ref, v_ref, sm_ref, mixr_ref, mixk_ref, mixv_ref, mixs_ref, wup_ref, w0_ref,
                 aup_ref, a0_ref, gup_ref, kk_ref, ka_ref, rk_ref, gng_ref, gnb_ref, o_ref,
                 state_ref, cr_ref, ck_ref, cv_ref, cs_ref, a_s, bt_s, kt_s, bp_s, kp_s, v_s, r_s, gam_s,
                 o_s, g_s, bonus_s, y_s, u_s, *, nc):
    t = nc * CHUNK

    @pl.when(pl.program_id(0) == 0)
    def _():
        state_ref[...] = jnp.zeros_like(state_ref)
        cr_ref[...] = jnp.zeros_like(cr_ref)
        ck_ref[...] = jnp.zeros_like(ck_ref)
        cv_ref[...] = jnp.zeros_like(cv_ref)
        cs_ref[...] = jnp.zeros_like(cs_ref)

    rowi = lax.broadcasted_iota(jnp.int32, (t, 1), 0)

    def shifted(x, carry_ref, mix):
        prev = jnp.where(rowi == 0, carry_ref[SUBLANES - 1:SUBLANES], pltpu.roll(x, 1, 0))
        carry_ref[...] = x[t - SUBLANES:t]
        return x + (prev - x) * mix

    r = shifted(r_ref[...].astype(F32), cr_ref, mixr_ref[...])
    k = shifted(k_ref[...].astype(F32), ck_ref, mixk_ref[...])
    v = shifted(v_ref[...].astype(F32), cv_ref, mixv_ref[...])
    sm = shifted(sm_ref[...], cs_ref, mixs_ref[...])
    wa, gl = sm[:, :LANES], sm[:, LANES:]

    i0 = lax.broadcasted_iota(jnp.int32, (LANES, LANES), 0)
    i1 = lax.broadcasted_iota(jnp.int32, (LANES, LANES), 1)
    same_head = (i0 // RWKV_DH) == (i1 // RWKV_DH)
    ones_blk = same_head.astype(BF16)

    w = -_softplus(-(w0_ref[...] + _dot_hi(jnp.tanh(wa), wup_ref[...]))) - 0.5
    a = _sigmoid(a0_ref[...] + _dot_hi(wa, aup_ref[...]))
    g_s[...] = _dot_hi(_sigmoid(gl), gup_ref[...])
    kk = k * kk_ref[...]
    kkn = kk / jnp.maximum(jnp.sqrt(_head_sum(kk * kk, ones_blk)), NORM_EPS)
    km = k * (1.0 + (a - 1.0) * ka_ref[...])
    b = kkn * a
    bonus_s[...] = _head_sum(r * km * rk_ref[...], ones_blk) * v
    v_s[...] = v
    r_s[...] = r

    log_d = -jnp.exp(w)
    c0 = lax.broadcasted_iota(jnp.int32, (CHUNK, CHUNK), 0)
    c1 = lax.broadcasted_iota(jnp.int32, (CHUNK, CHUNK), 1)
    upto = (c1 <= c0).astype(BF16)
    for j in range(nc):
        rows = slice(j * CHUNK, (j + 1) * CHUNK)
        ld = log_d[rows]
        c = _dot_sel_lhs(upto, ld)
        c_end = c[CHUNK - 1:CHUNK]
        grow = jnp.exp(-c)
        to_end = jnp.exp(c_end - c)
        a_s[rows] = kkn[rows] * jnp.exp(c - ld)
        bt_s[rows] = b[rows] * grow
        kt_s[rows] = km[rows] * grow
        bp_s[rows] = b[rows] * to_end
        kp_s[rows] = km[rows] * to_end
        gam_s[j * SUBLANES:(j + 1) * SUBLANES] = jnp.broadcast_to(jnp.exp(c_end), (SUBLANES, RWKV_W))

    lo = lax.broadcasted_iota(jnp.int32, (CHUNK, LANES), 1) < RWKV_DH
    strict_lower = jnp.logical_and(same_head, i1 < i0)

    def by_pair(x):
        return jnp.stack([x[:, p * LANES:(p + 1) * LANES] for p in range(RW_PAIRS)], axis=0)

    def stacked(ref, rows):
        x = by_pair(ref[rows, :])
        return jnp.concatenate([jnp.where(lo, x, 0.0), jnp.where(lo, 0.0, x)], axis=1)

    def chunk_local(j, carry):
        rows = pl.ds(pl.multiple_of(j * CHUNK, CHUNK), CHUNK)
        a2, bp2 = stacked(a_s, rows), stacked(bp_s, rows)
        m_all = _bmm_nt(a2.astype(BF16),
                        jnp.concatenate([stacked(bt_s, rows), stacked(kt_s, rows)], axis=1).astype(BF16))
        m_b = jnp.where(strict_lower, m_all[:, :, :LANES], 0.0)
        m_k = jnp.where(strict_lower, m_all[:, :, LANES:], 0.0)
        t_mi = _unit_lower_inverse_minus_eye(m_b, i0, i1)
        z2 = bp2 + _bmm3_tn(t_mi, bp2)
        yk = _bmm3_tn(jnp.concatenate([a2, m_k], axis=2), z2)
        kpp = stacked(kp_s, rows) - yk[:, LANES:]
        uu = _bmm_tn(stacked(v_s, rows).astype(BF16), kpp.astype(BF16))
        y_s[j] = yk[:, :LANES].astype(BF16)
        u_s[j] = uu[:, :RWKV_DH] + uu[:, RWKV_DH:]
        return carry

    lax.fori_loop(0, nc, chunk_local, 0)

    def chunk_advance(j, carry):
        rows = pl.ds(pl.multiple_of(j * CHUNK, CHUNK), CHUNK)
        gam = by_pair(gam_s[pl.ds(pl.multiple_of(j * SUBLANES, SUBLANES), SUBLANES)])[:, 0:1]
        s = state_ref[...]
        s_hi, s_lo = _split(s)
        y = y_s[j]
        s = s * gam - _bmm(jnp.concatenate([s_hi, s_lo], axis=2), jnp.concatenate([y, y], axis=1)) + u_s[j]
        state_ref[...] = s
        o2 = _bmm_nt(stacked(r_s, rows).astype(BF16), s.astype(BF16))
        o_pair = jnp.concatenate([o2[:, :CHUNK], o2[:, CHUNK:]], axis=2)
        o_s[rows, :] = jnp.concatenate([o_pair[p] for p in range(RW_PAIRS)], axis=1)
        return carry

    lax.fori_loop(0, nc, chunk_advance, 0)

    o = o_s[...]
    mu = _head_sum(o, ones_blk) * (1.0 / RWKV_DH)
    oc = o - mu
    var = _head_sum(oc * oc, ones_blk) * (1.0 / RWKV_DH)
    y = oc * lax.rsqrt(var + RWKV_GN_EPS) * gng_ref[...] + gnb_ref[...]
    o_ref[...] = ((y + bonus_s[...]) * g_s[...]).astype(o_ref.dtype)


def _rwkv(proj, small, mix, w_up, w0, a_up, a0, g_up, k_k, k_a, r_k, gn_g, gn_b):
    length = proj.shape[0]
    nc = _chunks_per_step(length, (5, 4, 2, 1))
    t = nc * CHUNK
    w = RWKV_W
    mix_s = mix[3 * w:][None]
    wup_pad = jnp.zeros((LANES, w), F32).at[:RWKV_RANK_W].set(w_up)
    aup_pad = jnp.zeros((LANES, w), F32).at[RWKV_RANK_W:].set(a_up)
    col = lambda c0: pl.BlockSpec((t, w), lambda c: (c, c0 // w))
    vec = pl.BlockSpec((1, w), lambda c: (0, 0))
    mat = pl.BlockSpec((LANES, w), lambda c: (0, 0))
    wide = lambda: pltpu.VMEM((t, w), F32)
    carry = lambda n: pltpu.VMEM((SUBLANES, n), F32)
    return pl.pallas_call(
        functools.partial(_rwkv_kernel, nc=nc),
        grid=(length // t,),
        in_specs=[col(RW_R0), col(RW_K0), col(RW_V0), pl.BlockSpec((t, 2 * LANES), lambda c: (c, 0)),
                  vec, vec, vec, pl.BlockSpec((1, 2 * LANES), lambda c: (0, 0)),
                  mat, vec, mat, vec, mat, vec, vec, vec, vec, vec],
        out_specs=pl.BlockSpec((t, w), lambda c: (c, 0)),
        out_shape=jax.ShapeDtypeStruct((length, w), BF16),
        scratch_shapes=[pltpu.VMEM((RW_PAIRS, RWKV_DH, LANES), F32),
                        carry(w), carry(w), carry(w), carry(2 * LANES),
                        wide(), wide(), wide(), wide(), wide(), wide(), wide(),
                        pltpu.VMEM((nc * SUBLANES, w), F32), wide(), wide(), wide(),
                        pltpu.VMEM((nc, RW_PAIRS, LANES, LANES), BF16),
                        pltpu.VMEM((nc, RW_PAIRS, RWKV_DH, LANES), F32)],
        compiler_params=_cparams(("arbitrary",)),
        name="rwkv7",
    )(proj, proj, proj, small, mix[None, :w], mix[None, w:2 * w], mix[None, 2 * w:3 * w], mix_s,
      wup_pad, w0[None], aup_pad, a0[None], g_up, k_k[None], k_a[None], r_k.reshape(1, w),
      gn_g[None], gn_b[None])


def _merge_kernel(oret_ref, ogla_ref, orw_ref, wret_ref, wgla_ref, wrw_ref, gret_ref, ggla_ref, grw_ref, o_ref):
    acc = _sigmoid(gret_ref[...].astype(F32)) * _dot(oret_ref[...], wret_ref[...])
    acc = acc + _sigmoid(ggla_ref[...].astype(F32)) * _dot(ogla_ref[...], wgla_ref[...])
    acc = acc + _sigmoid(grw_ref[...].astype(F32)) * _dot(orw_ref[...], wrw_ref[...])
    o_ref[...] = acc.astype(o_ref.dtype)


def _merge(o_ret, o_gla, o_rw, w_ret, w_gla, w_rw, layer, gates):
    m = o_ret.shape[0]
    tm = _pick(m, (640, 256, 128, 64))
    tn = 512
    act = lambda n: pl.BlockSpec((tm, n), lambda i, j: (i, 0))
    wgt = lambda n: pl.BlockSpec((None, n, tn), lambda i, j: (layer, 0, j))
    gate = lambda b: pl.BlockSpec((tm, tn), lambda i, j: (i, b * (D_MODEL // tn) + j))
    return pl.pallas_call(
        _merge_kernel,
        grid=(m // tm, D_MODEL // tn),
        in_specs=[act(RET_W), act(GLA_WV), act(RWKV_W), wgt(RET_W), wgt(GLA_WV), wgt(RWKV_W),
                  gate(0), gate(1), gate(2)],
        out_specs=pl.BlockSpec((tm, tn), lambda i, j: (i, j)),
        out_shape=jax.ShapeDtypeStruct((m, D_MODEL), BF16),
        compiler_params=_cparams(("parallel", "parallel")),
        name="branch_merge",
    )(o_ret, o_gla, o_rw, w_ret, w_gla, w_rw, gates, gates, gates)


def _router_kernel(h_ref, rw_ref, bias_ref, eidx_ref, ewts_ref):
    tm = h_ref.shape[0]
    (hh, hl), (wh, wl) = _split(h_ref[...]), _split(rw_ref[...])
    logits = (_dot(hh, wl) + _dot(hl, wh)) + _dot(hh, wh)
    s_all = _sigmoid(logits.T[0:N_EXPERTS])
    sel_all = s_all + bias_ref[...]
    ids = lax.broadcasted_iota(jnp.int32, (EXPERTS_PER_GROUP, tm), 0)
    neg = jnp.float32(-jnp.inf)
    best_score = best_e1 = best_e2 = best_w1 = best_w2 = None
    for gidx in range(N_GROUPS):
        rows = slice(gidx * EXPERTS_PER_GROUP, (gidx + 1) * EXPERTS_PER_GROUP)
        sel, s = sel_all[rows], s_all[rows]
        m1 = jnp.max(sel, axis=0, keepdims=True)
        i1 = jnp.min(jnp.where(sel == m1, ids, EXPERTS_PER_GROUP), axis=0, keepdims=True)
        rest = jnp.where(ids == i1, neg, sel)
        m2 = jnp.max(rest, axis=0, keepdims=True)
        i2 = jnp.min(jnp.where(rest == m2, ids, EXPERTS_PER_GROUP), axis=0, keepdims=True)
        score = m1 + m2
        w1 = jnp.sum(jnp.where(ids == i1, s, 0.0), axis=0, keepdims=True)
        w2 = jnp.sum(jnp.where(ids == i2, s, 0.0), axis=0, keepdims=True)
        e1 = i1 + gidx * EXPERTS_PER_GROUP
        e2 = i2 + gidx * EXPERTS_PER_GROUP
        if gidx == 0:
            best_score, best_e1, best_e2, best_w1, best_w2 = score, e1, e2, w1, w2
        else:
            take = score > best_score
            best_score = jnp.where(take, score, best_score)
            best_e1, best_e2 = jnp.where(take, e1, best_e1), jnp.where(take, e2, best_e2)
            best_w1, best_w2 = jnp.where(take, w1, best_w1), jnp.where(take, w2, best_w2)
    total = best_w1 + best_w2
    slot = lax.broadcasted_iota(jnp.int32, (SUBLANES, tm), 0)
    eidx_ref[...] = jnp.where(slot == 0, best_e1, jnp.where(slot == 1, best_e2, 0))
    ewts_ref[...] = jnp.where(slot == 0, best_w1 / total, jnp.where(slot == 1, best_w2 / total, 0.0))


def _router(h, router_w, router_bias):
    m = h.shape[0]
    tm = _pick(m, (256, 128))
    rw_pad = jnp.zeros((D_MODEL, LANES), F32).at[:, :N_EXPERTS].set(router_w)
    return pl.pallas_call(
        _router_kernel,
        grid=(m // tm,),
        in_specs=[pl.BlockSpec((tm, D_MODEL), lambda i: (i, 0)),
                  pl.BlockSpec((D_MODEL, LANES), lambda i: (0, 0)),
                  pl.BlockSpec((N_EXPERTS, 1), lambda i: (0, 0))],
        out_specs=(pl.BlockSpec((SUBLANES, tm), lambda i: (0, i)), pl.BlockSpec((SUBLANES, tm), lambda i: (0, i))),
        out_shape=(jax.ShapeDtypeStruct((SUBLANES, m), jnp.int32), jax.ShapeDtypeStruct((SUBLANES, m), F32)),
        compiler_params=_cparams(("parallel",)),
        name="router",
    )(h, rw_pad, router_bias[:, None])


def _row_copy(src_hbm, row, dst_buf, sem):
    return pltpu.make_async_copy(src_hbm.at[pl.ds(row, 1)], dst_buf, sem)


def _ffn_kernel(te_ref, rt_ref, na_ref, h_hbm, w1_ref, w3_ref, w2_ref, y_ref, xbuf, sem, *, tm):
    del te_ref
    i = pl.program_id(0)
    n = pl.num_programs(0)
    n_active = na_ref[0]
    ahead = FFN_SLOTS - 1
    slot = i % FFN_SLOTS

    def start_tile(tile, dst_slot):
        for r in range(tm):
            _row_copy(h_hbm, rt_ref[tile * tm + r], xbuf.at[dst_slot, pl.ds(r, 1)],
                      sem.at[dst_slot]).start(priority=r % DMA_THREADS)

    def wait_tile(dst_slot):
        def body(r, c):
            _row_copy(h_hbm, 0, xbuf.at[dst_slot, pl.ds(r, 1)], sem.at[dst_slot]).wait()
            return c
        lax.fori_loop(0, tm, body, 0, unroll=DMA_UNROLL)

    @pl.when(i == 0)
    def _():
        for t in range(ahead):
            start_tile(jnp.minimum(t, n - 1), t)

    wait_tile(slot)
    nxt = jnp.minimum(i + ahead, n - 1)
    nxt_slot = (i + ahead) % FFN_SLOTS

    @pl.when(i < n_active)
    def _():
        start_tile(nxt, nxt_slot)
        x = _unpack_halves(xbuf[slot]).astype(BF16)
        h1 = _dot(x, w1_ref[...])
        h3 = _dot(x, w3_ref[...])
        he = (h1 * _sigmoid(h1)) * h3
        y_ref[...] = _pack_halves(_dot(he.astype(BF16), w2_ref[...]))

    @pl.when(i >= n_active)
    def _():
        start_tile(nxt, nxt_slot)
        y_ref[...] = jnp.zeros_like(y_ref)

    @pl.when(i == n - 1)
    def _():
        for t in range(1, FFN_SLOTS):
            wait_tile((i + t) % FFN_SLOTS)


def _expert_ffn(hp, tile_expert, row_token, n_active, layer, w1, w3, w2):
    half = hp.shape[1]
    d = 2 * half
    tm = MOE_TM
    n_tiles = tile_expert.shape[0]
    wspec = lambda shape: pl.BlockSpec((None, None) + shape, lambda i, te, rt, na: (layer, te[i], 0, 0))
    return pl.pallas_call(
        functools.partial(_ffn_kernel, tm=tm),
        grid_spec=pltpu.PrefetchScalarGridSpec(
            num_scalar_prefetch=3,
            grid=(n_tiles,),
            in_specs=[pl.BlockSpec(memory_space=pl.ANY), wspec((d, D_FF_PAD)), wspec((d, D_FF_PAD)),
                      wspec((D_FF_PAD, d))],
            out_specs=pl.BlockSpec((tm, half), lambda i, te, rt, na: (i, 0)),
            scratch_shapes=[pltpu.VMEM((FFN_SLOTS, tm, half), jnp.uint32), pltpu.SemaphoreType.DMA((FFN_SLOTS,))]),
        out_shape=jax.ShapeDtypeStruct((n_tiles * tm, half), jnp.uint32),
        compiler_params=_cparams(("arbitrary",)),
        name="expert_ffn",
    )(tile_expert, row_token, n_active, hp, w1, w3, w2)


def _combine_ln_kernel(p1_ref, p2_ref, h_ref, y_hbm, w_ref, g_ref, b_ref, o_ref, ob_ref, ybuf, sem, *, tm, n_mask):
    i = pl.program_id(0)
    n = pl.num_programs(0)
    slot = i % 2

    def start_tile(tile, dst_slot):
        for r in range(tm):
            _row_copy(y_hbm, p1_ref[tile * tm + r], ybuf.at[dst_slot, 0, pl.ds(r, 1)], sem.at[dst_slot]).start(priority=0)
            _row_copy(y_hbm, p2_ref[tile * tm + r], ybuf.at[dst_slot, 1, pl.ds(r, 1)],
                      sem.at[dst_slot]).start(priority=DMA_THREADS - 1)

    def wait_tile(dst_slot):
        def body(r, c):
            _row_copy(y_hbm, 0, ybuf.at[dst_slot, 0, pl.ds(r, 1)], sem.at[dst_slot]).wait()
            _row_copy(y_hbm, 0, ybuf.at[dst_slot, 1, pl.ds(r, 1)], sem.at[dst_slot]).wait()
            return c
        lax.fori_loop(0, tm, body, 0, unroll=DMA_UNROLL)

    @pl.when(i == 0)
    def _():
        start_tile(0, 0)

    wait_tile(slot)
    start_tile(jnp.minimum(i + 1, n - 1), 1 - slot)
    w = w_ref[...]
    ffn = w[:, 0:1] * _unpack_halves(ybuf[slot, 0]) + w[:, 1:2] * _unpack_halves(ybuf[slot, 1])
    x = h_ref[...] * DN_ALPHA + ffn
    _emit_ln(_layernorm_rows(x, g_ref[...], b_ref[...]), o_ref, ob_ref, n_mask, tm)

    @pl.when(i == n - 1)
    def _():
        wait_tile(1 - slot)


def _combine_ln(h, y, pos1, pos2, wts, g, b, n_mask, frames_only):
    m, d = h.shape
    tm = FRONT if frames_only else _pick(m, (256, 128, 64))
    row = pl.BlockSpec((tm, d), lambda i, p1, p2: (i, 0))
    vec = pl.BlockSpec((1, d), lambda i, p1, p2: (0, 0))
    if frames_only:
        out_rows, out_row = m - tm, pl.BlockSpec((tm, d), lambda i, p1, p2: (jnp.maximum(i - 1, 0), 0))
    else:
        out_rows, out_row = m, row
    return pl.pallas_call(
        functools.partial(_combine_ln_kernel, tm=tm, n_mask=n_mask),
        grid_spec=pltpu.PrefetchScalarGridSpec(
            num_scalar_prefetch=2,
            grid=(m // tm,),
            in_specs=[row, pl.BlockSpec(memory_space=pl.ANY),
                      pl.BlockSpec((tm, 2), lambda i, p1, p2: (i, 0)), vec, vec],
            out_specs=(out_row, row),
            scratch_shapes=[pltpu.VMEM((2, 2, tm, d // 2), jnp.uint32), pltpu.SemaphoreType.DMA((2,))]),
        out_shape=(jax.ShapeDtypeStruct((out_rows, d), F32), jax.ShapeDtypeStruct((m, d), BF16)),
        compiler_params=_cparams(("arbitrary",)),
        name="moe_combine_layernorm",
    )(pos1, pos2, h, y, wts, g[None], b[None])


def _dispatch_plan(eidx, tm):
    m = eidx.shape[1]
    flat_e = jnp.concatenate([eidx[0], eidx[1]])
    tok = jnp.arange(m, dtype=jnp.int32)
    flat_t = jnp.concatenate([tok, tok])
    onehot = (flat_e[:, None] == jnp.arange(N_EXPERTS, dtype=jnp.int32)[None, :]).astype(jnp.int32)
    rank = jnp.take_along_axis(jnp.cumsum(onehot, axis=0) - onehot, flat_e[:, None], axis=1)[:, 0]
    counts = jnp.sum(onehot, axis=0)
    padded = ((counts + tm - 1) // tm) * tm
    ends = jnp.cumsum(padded)
    offs = ends - padded
    pos = (offs[flat_e] + rank).astype(jnp.int32)
    n_tiles = (2 * m) // tm + N_EXPERTS
    row_token = jnp.zeros((n_tiles * tm,), jnp.int32).at[pos].set(flat_t)
    n_active = (ends[-1] // tm).astype(jnp.int32)
    tile_id = jnp.arange(n_tiles, dtype=jnp.int32)
    tile_id = jnp.minimum(tile_id, n_active - 1)
    tile_expert = (jnp.sum(tile_id[:, None] >= (offs // tm)[None, :], axis=1) - 1).astype(jnp.int32)
    return tile_expert, row_token, n_active.reshape(1), pos[:m], pos[m:]


def _moe_ln(h, hp, router_w, router_bias, layer, w1, w3, w2, g, b, n_mask, frames_only=False):
    eidx, ewts = _router(h, router_w, router_bias)
    tile_expert, row_token, n_active, pos1, pos2 = _dispatch_plan(eidx, MOE_TM)
    y = _expert_ffn(hp, tile_expert, row_token, n_active, layer, w1, w3, w2)
    return _combine_ln(h, y, pos1, pos2, ewts[0:2].T, g, b, n_mask, frames_only)


def _cast_pad_kernel(x_ref, o_ref):
    rows, cols = x_ref.shape
    o_ref[...] = jnp.zeros(o_ref.shape, o_ref.dtype)
    o_ref[0:rows, 0:cols] = x_ref[...].astype(o_ref.dtype)


def _cast_pad(w, rows_out, cols_out, block_rows, block_cols):
    n_l, n_e, r, c = w.shape
    in_rows = r if rows_out != r else block_rows
    in_cols = c if cols_out != c else block_cols
    out_rows = rows_out if rows_out != r else block_rows
    out_cols = cols_out if cols_out != c else block_cols
    grid = (n_l * n_e, rows_out // out_rows, cols_out // out_cols)
    idx = lambda e, i, j: (e // n_e, e % n_e, i, j)
    return pl.pallas_call(
        _cast_pad_kernel,
        grid=grid,
        in_specs=[pl.BlockSpec((None, None, in_rows, in_cols), idx)],
        out_specs=pl.BlockSpec((None, None, out_rows, out_cols), idx),
        out_shape=jax.ShapeDtypeStruct((n_l, n_e, rows_out, cols_out), BF16),
        compiler_params=_cparams(("parallel", "parallel", "parallel")),
        name="cast_pad_weights",
    )(w)


def _cast_pad_transposed_kernel(x_ref, o_ref):
    rows, cols = x_ref.shape
    x = jnp.concatenate([x_ref[...], jnp.zeros((o_ref.shape[1] - rows, cols), x_ref.dtype)], axis=0)
    o_ref[...] = x.T.astype(o_ref.dtype)


def _cast_pad_transposed(wt, cols_out, block):
    n_l, n_e, c, r = wt.shape
    return pl.pallas_call(
        _cast_pad_transposed_kernel,
        grid=(n_l * n_e, r // block),
        in_specs=[pl.BlockSpec((None, None, c, block), lambda e, i: (e // n_e, e % n_e, 0, i))],
        out_specs=pl.BlockSpec((None, None, block, cols_out), lambda e, i: (e // n_e, e % n_e, i, 0)),
        out_shape=jax.ShapeDtypeStruct((n_l, n_e, r, cols_out), BF16),
        compiler_params=_cparams(("parallel", "parallel")),
        name="cast_pad_weights_t",
    )(wt)


def _prep_experts(w1, w3, w2):
    d = w1.shape[2]
    w1 = _cast_pad_transposed(jnp.swapaxes(w1, 2, 3), D_FF_PAD, 1024)
    w3 = _cast_pad_transposed(jnp.swapaxes(w3, 2, 3), D_FF_PAD, 1024)
    w2 = _cast_pad(w2, D_FF_PAD, d, D_FF_PAD, 2048)
    return w1, w3, w2


def _rotary_tables(length):
    half = RET_DK // 2
    pos = jnp.arange(length, dtype=jnp.int32) - N_MASK
    inv = ROPE_BASE ** (-jnp.arange(half, dtype=F32) / half)
    ang = pos.astype(F32)[:, None] * inv[None, :]
    cos, sin = jnp.cos(ang), jnp.sin(ang)
    return jnp.concatenate([cos, cos], axis=1), jnp.concatenate([-sin, sin], axis=1)


def _mixer(hb, cos, sin, layer, w_in, ret_gn_g, ret_gn_b, gla_a_up, gla_a_bias, gla_gn_g, gla_gn_b, rw_mix, rw_w_up,
           rw_w0, rw_a_up, rw_a0, rw_g_up, rw_k_k, rw_k_a, rw_r_k, rw_gn_g, rw_gn_b, w_br_ret, w_br_gla,
           w_br_rw, w_out):
    rows = lambda a, b: lax.slice(w_in, (layer, a, 0), (layer + 1, b, D_MODEL))
    w_small = jnp.concatenate([rows(W_IN_SM0, W_IN_GATE0), rows(W_IN_GA0, W_IN_RW0),
                               jnp.zeros((1, LANES - GLA_RANK, D_MODEL), F32)], axis=1)
    proj = _matmul_nt(hb, w_in, layer, 0, N_HEAD, BF16)
    proj_rw = _matmul_nt(hb, w_in, layer, W_IN_RW0, W_IN_SM0 - W_IN_RW0, BF16)
    gates = _matmul_nt(hb, w_in, layer, W_IN_GATE0, w_in.shape[1] - W_IN_GATE0, BF16)
    small = _matmul_nt(hb, w_small, 0, 0, N_SMALL, F32, tn=N_SMALL)
    o_ret = _retention(proj, cos, sin, ret_gn_g, ret_gn_b)
    o_gla = _gla(proj, small, gla_a_up, gla_a_bias, gla_gn_g, gla_gn_b)
    o_rw = _rwkv(proj_rw, small, rw_mix, rw_w_up, rw_w0, rw_a_up, rw_a0, rw_g_up, rw_k_k, rw_k_a, rw_r_k,
                 rw_gn_g, rw_gn_b)
    merged = _merge(o_ret, o_gla, o_rw, w_br_ret, w_br_gla, w_br_rw, layer, gates)
    return _matmul(merged, w_out, F32, 512, layer=layer)


def kernel(x, meta, ln_in_g, ln_in_b, w_in, ret_gn_g, ret_gn_b, gla_a_up, gla_a_bias, gla_gn_g, gla_gn_b, rw_mix, rw_w_up, rw_w0, rw_a_up, rw_a0, rw_g_up, rw_k_k, rw_k_a, rw_r_k, rw_gn_g, rw_gn_b, w_br_ret, w_br_gla, w_br_rw, w_out, ln1_g, ln1_b, router_w, router_bias, exp_w1, exp_w3, exp_w2, ln2_g, ln2_b):
    batch, seq, d = x.shape
    assert batch == 1 and d == D_MODEL
    assert meta.shape == (N_META, d)
    cos, sin = _rotary_tables(FRONT + seq)
    h, hb = _embed_layer_norm(x[0], meta.astype(x.dtype), ln_in_g, ln_in_b)
    w_in = jnp.swapaxes(w_in, 1, 2)
    w_out = w_out.astype(BF16)
    w_br_ret, w_br_gla, w_br_rw = w_br_ret.astype(BF16), w_br_gla.astype(BF16), w_br_rw.astype(BF16)
    w1, w3, w2 = _prep_experts(exp_w1, exp_w3, exp_w2)
    for l in range(DEPTH):
        mix = _mixer(hb, cos, sin, l, w_in, ret_gn_g[l], ret_gn_b[l], gla_a_up[l], gla_a_bias[l], gla_gn_g[l],
                     gla_gn_b[l], rw_mix[l], rw_w_up[l], rw_w0[l], rw_a_up[l], rw_a0[l], rw_g_up[l], rw_k_k[l],
                     rw_k_a[l], rw_r_k[l], rw_gn_g[l], rw_gn_b[l], w_br_ret, w_br_gla, w_br_rw, w_out)
        h, hp = _layer_norm(h, mix, ln1_g[l], ln1_b[l], 0, packed=True)
        h, hb = _moe_ln(h, hp, router_w, router_bias, l, w1, w3, w2, ln2_g[l], ln2_b[l], N_MASK,
                        frames_only=(l == DEPTH - 1))
    return h[None]
```

```python
import functools

import jax
import jax.numpy as jnp
from jax import lax
from jax.experimental import pallas as pl
from jax.experimental.pallas import tpu as pltpu

F32 = jnp.float32
BF16 = jnp.bfloat16

D_MODEL = 4096
DEPTH = 2
CHUNK = 64
N_META = 16

RET_HEADS, RET_DK, RET_DV = 12, 128, 128
RET_W = RET_HEADS * RET_DV
ROPE_BASE = 10000.0

GLA_HEADS, GLA_DK, GLA_DV = 6, 128, 256
GLA_WK = GLA_HEADS * GLA_DK
GLA_WV = GLA_HEADS * GLA_DV
GLA_RANK = 16
GLA_TAU = 16.0

RWKV_HEADS, RWKV_DH = 16, 64
RWKV_W = RWKV_HEADS * RWKV_DH
RWKV_RANK_W, RWKV_RANK_A, RWKV_RANK_G = 64, 64, 128

N_EXPERTS = 32
N_GROUPS = 4
EXPERTS_PER_GROUP = N_EXPERTS // N_GROUPS
D_FF = 704

DN_ALPHA = (2 * DEPTH) ** 0.25
LN_EPS = 1e-5
GN_EPS = 1e-5
RWKV_GN_EPS = 64e-5
NORM_EPS = 1e-12

LANES = 128
SUBLANES = 8
VMEM_LIMIT_BYTES = 56 * 1024 * 1024

RET_Q0, RET_K0, RET_V0, RET_G0 = 0, RET_W, 2 * RET_W, 3 * RET_W
GLA_Q0 = 4 * RET_W
GLA_K0 = GLA_Q0 + GLA_WK
GLA_V0 = GLA_K0 + GLA_WK
GLA_G0 = GLA_V0 + GLA_WV
N_HEAD = GLA_G0 + GLA_WV
W_IN_GA0 = N_HEAD
W_IN_RW0 = W_IN_GA0 + GLA_RANK
W_IN_SM0 = W_IN_RW0 + 3 * RWKV_W
W_IN_GATE0 = W_IN_SM0 + RWKV_RANK_W + RWKV_RANK_A + RWKV_RANK_G
RW_R0, RW_K0, RW_V0 = 0, RWKV_W, 2 * RWKV_W
N_SMALL = 3 * LANES
D_FF_PAD = 768

ROW_TILE = 256
FRONT = ROW_TILE
N_MASK = FRONT - N_META
MOE_TM = 256
DMA_UNROLL = 8
DMA_THREADS = 2
FFN_SLOTS = 3


def _cparams(sem):
    return pltpu.CompilerParams(dimension_semantics=sem, vmem_limit_bytes=VMEM_LIMIT_BYTES)


def _pick(n, cands):
    for c in cands:
        if n % c == 0:
            return c
    raise ValueError(f"no tile in {cands} divides {n}")


def _dot(a, b):
    return jnp.dot(a, b, preferred_element_type=F32)


def _split(x):
    hi = x.astype(BF16)
    return hi, (x - hi.astype(F32)).astype(BF16)


def _split3(x):
    hi = x.astype(BF16)
    rest = x - hi.astype(F32)
    mid = rest.astype(BF16)
    return hi, mid, (rest - mid.astype(F32)).astype(BF16)


def _dot_hi(a, b):
    (ah, al), (bh, bl) = _split(a), _split(b)
    return _dot(jnp.concatenate([ah, ah, al], axis=1), jnp.concatenate([bh, bl, bh], axis=0))


def _dot_sel_lhs(sel, b):
    return _dot(jnp.concatenate([sel, sel, sel], axis=1), jnp.concatenate(_split3(b), axis=0))


def _dot_sel_rhs16(a, sel):
    return _dot(jnp.concatenate(_split(a), axis=1), jnp.concatenate([sel, sel], axis=0))


def _dot_nt(a, b):
    return lax.dot_general(a, b, (((1,), (1,)), ((), ())), preferred_element_type=F32)


def _sigmoid(x):
    return 1.0 / (1.0 + jnp.exp(-x))


def _softplus(x):
    return jnp.maximum(x, 0.0) + jnp.log(1.0 + jnp.exp(-jnp.abs(x)))


def _mm_kernel(x_ref, w_ref, o_ref):
    o_ref[...] = _dot(x_ref[...], w_ref[...]).astype(o_ref.dtype)


def _matmul(x, w, out_dtype, tn, layer=None, n=None):
    m, k = x.shape
    n = w.shape[-1] if n is None else n
    assert n % tn == 0
    tm = _pick(m, (1280, 640, 256, 128, 64))
    if w.ndim == 3:
        w_spec = pl.BlockSpec((None, k, tn), lambda i, j: (layer, 0, j))
    else:
        w_spec = pl.BlockSpec((k, tn), lambda i, j: (0, j))
    return pl.pallas_call(
        _mm_kernel,
        grid=(m // tm, n // tn),
        in_specs=[pl.BlockSpec((tm, k), lambda i, j: (i, 0)), w_spec],
        out_specs=pl.BlockSpec((tm, tn), lambda i, j: (i, j)),
        out_shape=jax.ShapeDtypeStruct((m, n), out_dtype),
        compiler_params=_cparams(("parallel", "parallel")),
        name="dense_matmul",
    )(x, w)


def _mm_nt_kernel(x_ref, wt_ref, o_ref, wb_ref):
    @pl.when(pl.program_id(1) == 0)
    def _():
        wb_ref[...] = wt_ref[0].astype(BF16)

    o_ref[...] = _dot_nt(x_ref[...], wb_ref[...]).astype(o_ref.dtype)


def _matmul_nt(x, wt, layer, row0, n, out_dtype, tn=512):
    m, k = x.shape
    assert n % tn == 0 and row0 % 16 == 0 and wt.shape[2] == k
    tm = _pick(m, (1280, 640, 256, 128, 64))
    wt_spec = pl.BlockSpec((pl.Element(1), pl.Element(tn), pl.Element(k)),
                           lambda j, i: (layer, pl.multiple_of(row0 + j * tn, 16), 0))
    return pl.pallas_call(
        _mm_nt_kernel,
        grid=(n // tn, m // tm),
        in_specs=[pl.BlockSpec((tm, k), lambda j, i: (i, 0)), wt_spec],
        out_specs=pl.BlockSpec((tm, tn), lambda j, i: (i, j)),
        out_shape=jax.ShapeDtypeStruct((m, n), out_dtype),
        scratch_shapes=[pltpu.VMEM((tn, k), BF16)],
        compiler_params=_cparams(("arbitrary", "arbitrary")),
        name="dense_matmul_nt",
    )(x, wt)


def _layernorm_rows(x, g, b):
    mu = jnp.mean(x, axis=-1, keepdims=True)
    xc = x - mu
    var = jnp.mean(xc * xc, axis=-1, keepdims=True)
    return xc * lax.rsqrt(var + LN_EPS) * g + b


HIGH_HALF = 0xFFFF0000


def _pack_halves(y):
    half = y.shape[1] // 2
    lo = pltpu.bitcast(y[:, :half].astype(BF16).astype(F32), jnp.uint32) >> 16
    hi = pltpu.bitcast(y[:, half:].astype(BF16).astype(F32), jnp.uint32) & jnp.uint32(HIGH_HALF)
    return lo | hi


def _unpack_halves(w):
    lo = pltpu.bitcast(w << 16, F32)
    hi = pltpu.bitcast(w & jnp.uint32(HIGH_HALF), F32)
    return jnp.concatenate([lo, hi], axis=1)


def _emit_ln(y, o_ref, ob_ref, n_mask, tm):
    o_ref[...] = y
    if ob_ref.dtype == jnp.uint32:
        ob_ref[...] = _pack_halves(y)
        return
    yb = y.astype(BF16)
    if n_mask:
        row = pl.program_id(0) * tm + lax.broadcasted_iota(jnp.int32, (tm, 1), 0)
        yb = jnp.where(row >= n_mask, yb, jnp.zeros_like(yb))
    ob_ref[...] = yb


def _res_ln_kernel(h_ref, f_ref, g_ref, b_ref, o_ref, ob_ref, *, alpha, n_mask, tm):
    x = h_ref[...] * alpha + f_ref[...]
    _emit_ln(_layernorm_rows(x, g_ref[...], b_ref[...]), o_ref, ob_ref, n_mask, tm)


def _embed_ln_kernel(x_ref, meta_ref, g_ref, b_ref, o_ref, ob_ref, *, n_mask, tm):
    front = jnp.concatenate([jnp.zeros((n_mask, x_ref.shape[1]), F32), meta_ref[...]], axis=0)
    x = jnp.where(pl.program_id(0) == 0, front, x_ref[...])
    _emit_ln(_layernorm_rows(x, g_ref[...], b_ref[...]), o_ref, ob_ref, n_mask, tm)


def _embed_layer_norm(x, meta, g, b):
    seq, d = x.shape
    tm = FRONT
    assert seq % tm == 0 and N_MASK + meta.shape[0] == tm
    m = tm + seq
    row = pl.BlockSpec((tm, d), lambda i: (i, 0))
    vec = pl.BlockSpec((1, d), lambda i: (0, 0))
    return pl.pallas_call(
        functools.partial(_embed_ln_kernel, n_mask=N_MASK, tm=tm),
        grid=(m // tm,),
        in_specs=[pl.BlockSpec((tm, d), lambda i: (jnp.maximum(i - 1, 0), 0)),
                  pl.BlockSpec(meta.shape, lambda i: (0, 0)), vec, vec],
        out_specs=(row, row),
        out_shape=(jax.ShapeDtypeStruct((m, d), F32), jax.ShapeDtypeStruct((m, d), BF16)),
        compiler_params=_cparams(("parallel",)), name="embed_layernorm",
    )(x, meta, g[None], b[None])


def _layer_norm(h, f, g, b, n_mask, packed=False):
    m, d = h.shape
    tm = _pick(m, (256, 128, 64))
    row = pl.BlockSpec((tm, d), lambda i: (i, 0))
    vec = pl.BlockSpec((1, d), lambda i: (0, 0))
    if packed:
        narrow, narrow_spec = jax.ShapeDtypeStruct((m, d // 2), jnp.uint32), pl.BlockSpec((tm, d // 2), lambda i: (i, 0))
    else:
        narrow, narrow_spec = jax.ShapeDtypeStruct((m, d), BF16), row
    return pl.pallas_call(
        functools.partial(_res_ln_kernel, alpha=DN_ALPHA, n_mask=n_mask, tm=tm),
        grid=(m // tm,), in_specs=[row, row, vec, vec], out_specs=(row, narrow_spec),
        out_shape=(jax.ShapeDtypeStruct((m, d), F32), narrow),
        compiler_params=_cparams(("parallel",)), name="res_layernorm",
    )(h, f, g[None], b[None])


def _groupnorm_lanes(o, g, b, eps):
    mu = jnp.mean(o, axis=-1, keepdims=True)
    oc = o - mu
    var = jnp.mean(oc * oc, axis=-1, keepdims=True)
    return oc * lax.rsqrt(var + eps) * g + b


def _chunks_per_step(length, cands):
    return _pick(length // CHUNK, cands)


def _by_head(x, n_heads, width):
    return jnp.stack([x[:, h * width:(h + 1) * width] for h in range(n_heads)], axis=0)


def _heads_to_lanes(x):
    return jnp.concatenate([x[h] for h in range(x.shape[0])], axis=1)


def _ret_kernel(q_ref, k_ref, v_ref, g_ref, cos_ref, sin_ref, intra_ref, qdec_ref, kdec_ref, cdec_ref,
                gng_ref, gnb_ref, o_ref, state_ref, *, nc):
    @pl.when(pl.program_id(0) == 0)
    def _():
        state_ref[...] = jnp.zeros_like(state_ref)

    cos, sin = cos_ref[...], sin_ref[...]
    half = RET_DK // 2

    def rotated(ref):
        x = ref[...].astype(F32)
        return jnp.stack([xh * cos + pltpu.roll(xh, half, 1) * sin
                          for xh in (x[:, h * RET_DK:(h + 1) * RET_DK] for h in range(RET_HEADS))], axis=0)

    q = rotated(q_ref) * (RET_DK ** -0.5)
    k = rotated(k_ref)
    v = _by_head(v_ref[...], RET_HEADS, RET_DV)
    intra, qdec, kdec = intra_ref[...], qdec_ref[...], kdec_ref[...]
    cdec = cdec_ref[...][:, 0:1]
    state = state_ref[...]
    outs = []
    for j in range(nc):
        sl = slice(j * CHUNK, (j + 1) * CHUNK)
        qc, kc, vc = q[:, sl], k[:, sl], v[:, sl]
        s = _bmm_nt(qc.astype(BF16), kc.astype(BF16)) * intra
        outs.append(_bmm(s.astype(BF16), vc) + _bmm((qc * qdec).astype(BF16), state.astype(BF16)))
        state = state * cdec + _bmm_tn((kc * kdec).astype(BF16), vc)
    state_ref[...] = state
    y = _groupnorm_lanes(jnp.concatenate(outs, axis=1), gng_ref[...], gnb_ref[...], GN_EPS)
    gate = _by_head(g_ref[...].astype(F32), RET_HEADS, RET_DV)
    o_ref[...] = _heads_to_lanes(y * (gate * _sigmoid(gate))).astype(o_ref.dtype)


def _retention(proj, cos, sin, gn_g, gn_b):
    length = proj.shape[0]
    nc = _chunks_per_step(length, (5, 4, 2, 1))
    t = nc * CHUNK
    lg = jnp.log1p(-jnp.exp2(-5.0 - jnp.arange(RET_HEADS, dtype=F32)))
    i = jnp.arange(CHUNK, dtype=F32)
    intra = jnp.exp(lg[:, None, None] * jnp.abs(i[:, None] - i[None, :]))
    qdec = jnp.broadcast_to(jnp.exp(lg[:, None] * (i[None, :] + 1.0))[:, :, None], (RET_HEADS, CHUNK, RET_DK))
    kdec = jnp.broadcast_to(jnp.exp(lg[:, None] * (CHUNK - 1.0 - i[None, :]))[:, :, None], (RET_HEADS, CHUNK, RET_DK))
    cdec = jnp.broadcast_to(jnp.exp(lg * CHUNK)[:, None, None], (RET_HEADS, SUBLANES, RET_DV))
    col = lambda c0: pl.BlockSpec((t, RET_W), lambda c: (c, c0 // RET_W))
    tab = pl.BlockSpec((t, RET_DK), lambda c: (c, 0))
    whole = lambda a: pl.BlockSpec(a.shape, lambda c: (0,) * a.ndim)
    gng, gnb = gn_g.reshape(RET_HEADS, 1, RET_DV), gn_b.reshape(RET_HEADS, 1, RET_DV)
    consts = (intra, qdec, kdec, cdec, gng, gnb)
    return pl.pallas_call(
        functools.partial(_ret_kernel, nc=nc),
        grid=(length // t,),
        in_specs=[col(RET_Q0), col(RET_K0), col(RET_V0), col(RET_G0), tab, tab] + [whole(a) for a in consts],
        out_specs=pl.BlockSpec((t, RET_W), lambda c: (c, 0)),
        out_shape=jax.ShapeDtypeStruct((length, RET_W), BF16),
        scratch_shapes=[pltpu.VMEM((RET_HEADS, RET_DK, RET_DV), F32)],
        compiler_params=_cparams(("arbitrary",)),
        name="retention",
    )(proj, proj, proj, proj, cos, sin, *consts)


def _gla_kernel(q_ref, k_ref, v_ref, g_ref, ga_ref, aup_ref, ab_ref, gng_ref, gnb_ref, o_ref, state_ref, *, nc):
    @pl.when(pl.program_id(0) == 0)
    def _():
        state_ref[...] = jnp.zeros_like(state_ref)

    z = _dot_hi(ga_ref[...], aup_ref[...]) + ab_ref[...]
    log_a = -_softplus(-z) / GLA_TAU
    q = _by_head((q_ref[...].astype(F32) * (GLA_DK ** -0.5)).astype(BF16), GLA_HEADS, GLA_DK)
    k = k_ref[...].astype(F32)
    v = _by_head(v_ref[...], GLA_HEADS, GLA_DV)
    row = lax.broadcasted_iota(jnp.int32, (CHUNK, CHUNK), 0)
    colm = lax.broadcasted_iota(jnp.int32, (CHUNK, CHUNK), 1)
    later = (colm > row).astype(BF16)
    state = state_ref[...]
    outs = []
    for j in range(nc):
        sl = slice(j * CHUNK, (j + 1) * CHUNK)
        la = log_a[sl]
        to_end = _dot_sel_lhs(later, la)
        dec = _by_head(jnp.exp(to_end[0:1] + la[0:1]), GLA_HEADS, GLA_DK)
        kc = _by_head((k[sl] * jnp.exp(to_end)).astype(BF16), GLA_HEADS, GLA_DK)
        state = state * dec + _bmm_tn(v[:, sl], kc)
        outs.append(_bmm_nt(q[:, sl], state.astype(BF16)))
    state_ref[...] = state
    y = _groupnorm_lanes(jnp.concatenate(outs, axis=1), gng_ref[...], gnb_ref[...], GN_EPS)
    gate = _by_head(g_ref[...].astype(F32), GLA_HEADS, GLA_DV)
    o_ref[...] = _heads_to_lanes(y * (gate * _sigmoid(gate))).astype(o_ref.dtype)


def _gla(proj, small, a_up, a_bias, gn_g, gn_b):
    length = proj.shape[0]
    nc = _chunks_per_step(length, (5, 4, 2, 1))
    t = nc * CHUNK
    a_up_pad = jnp.zeros((LANES, GLA_WK), F32).at[:GLA_RANK].set(a_up)
    kcol = lambda c0: pl.BlockSpec((t, GLA_WK), lambda c: (c, c0 // GLA_WK))
    vcol = lambda c0: pl.BlockSpec((t, GLA_WV), lambda c: (c, c0 // GLA_WV))
    whole = lambda a: pl.BlockSpec(a.shape, lambda c: (0,) * a.ndim)
    consts = (a_up_pad, a_bias[None], gn_g.reshape(GLA_HEADS, 1, GLA_DV), gn_b.reshape(GLA_HEADS, 1, GLA_DV))
    return pl.pallas_call(
        functools.partial(_gla_kernel, nc=nc),
        grid=(length // t,),
        in_specs=[kcol(GLA_Q0), kcol(GLA_K0), vcol(GLA_V0), vcol(GLA_G0),
                  pl.BlockSpec((t, LANES), lambda c: (c, 2))] + [whole(a) for a in consts],
        out_specs=pl.BlockSpec((t, GLA_WV), lambda c: (c, 0)),
        out_shape=jax.ShapeDtypeStruct((length, GLA_WV), BF16),
        scratch_shapes=[pltpu.VMEM((GLA_HEADS, GLA_DV, GLA_DK), F32)],
        compiler_params=_cparams(("arbitrary",)),
        name="gla",
    )(proj, proj, proj, proj, small, *consts)


RW_PAIRS = RWKV_HEADS // 2


def _head_sum(x, ones_blk):
    parts = [_dot_sel_rhs16(x[:, p * LANES:(p + 1) * LANES], ones_blk) for p in range(RW_PAIRS)]
    return jnp.concatenate(parts, axis=1)


def _bmm(a, b):
    return jnp.einsum('pik,pkj->pij', a, b, preferred_element_type=F32)


def _bmm_nt(a, b):
    return jnp.einsum('pik,pjk->pij', a, b, preferred_element_type=F32)


def _bmm_tn(a, b):
    return jnp.einsum('pki,pkj->pij', a, b, preferred_element_type=F32)


def _bmm3(a, b):
    (ah, al), (bh, bl) = _split(a), _split(b)
    return _bmm(jnp.concatenate([ah, ah, al], axis=2), jnp.concatenate([bh, bl, bh], axis=1))


def _bmm3_tn(a, b):
    (ah, al), (bh, bl) = _split(a), _split(b)
    return _bmm_tn(jnp.concatenate([ah, ah, al], axis=1), jnp.concatenate([bh, bl, bh], axis=1))


def _unit_lower_inverse_minus_eye(m, i0, i1):
    def joins(s):
        return jnp.logical_and(i0 // (2 * s) == i1 // (2 * s),
                               jnp.logical_and((i0 // s) % 2 == 1, (i1 // s) % 2 == 0))

    xp = -jnp.where(joins(1), m, 0.0)
    s = 2
    while s < CHUNK:
        c = jnp.where(joins(s), m, 0.0)
        y = c + _bmm3(xp, c)
        xp = xp - (y + _bmm3(y, xp))
        s *= 2
    return xp


def _rwkv_kernel(r_ref, k_ref, v_ref, sm_ref, mixr_ref, mixk_ref, mixv_ref, mixs_ref, wup_ref, w0_ref,
                 aup_ref, a0_ref, gup_ref, kk_ref, ka_ref, rk_ref, gng_ref, gnb_ref, o_ref,
                 state_ref, cr_ref, ck_ref, cv_ref, cs_ref, a_s, bt_s, kt_s, bp_s, kp_s, v_s, r_s, gam_s,
                 o_s, g_s, bonus_s, y_s, u_s, *, nc):
    t = nc * CHUNK

    @pl.when(pl.program_id(0) == 0)
    def _():
        state_ref[...] = jnp.zeros_like(state_ref)
        cr_ref[...] = jnp.zeros_like(cr_ref)
        ck_ref[...] = jnp.zeros_like(ck_ref)
        cv_ref[...] = jnp.zeros_like(cv_ref)
        cs_ref[...] = jnp.zeros_like(cs_ref)

    rowi = lax.broadcasted_iota(jnp.int32, (t, 1), 0)

    def shifted(x, carry_ref, mix):
        prev = jnp.where(rowi == 0, carry_ref[SUBLANES - 1:SUBLANES], pltpu.roll(x, 1, 0))
        carry_ref[...] = x[t - SUBLANES:t]
        return x + (prev - x) * mix

    r = shifted(r_ref[...].astype(F32), cr_ref, mixr_ref[...])
    k = shifted(k_ref[...].astype(F32), ck_ref, mixk_ref[...])
    v = shifted(v_ref[...].astype(F32), cv_ref, mixv_ref[...])
    sm = shifted(sm_ref[...], cs_ref, mixs_ref[...])
    wa, gl = sm[:, :LANES], sm[:, LANES:]

    i0 = lax.broadcasted_iota(jnp.int32, (LANES, LANES), 0)
    i1 = lax.broadcasted_iota(jnp.int32, (LANES, LANES), 1)
    same_head = (i0 // RWKV_DH) == (i1 // RWKV_DH)
    ones_blk = same_head.astype(BF16)

    w = -_softplus(-(w0_ref[...] + _dot_hi(jnp.tanh(wa), wup_ref[...]))) - 0.5
    a = _sigmoid(a0_ref[...] + _dot_hi(wa, aup_ref[...]))
    g_s[...] = _dot_hi(_sigmoid(gl), gup_ref[...])
    kk = k * kk_ref[...]
    kkn = kk / jnp.maximum(jnp.sqrt(_head_sum(kk * kk, ones_blk)), NORM_EPS)
    km = k * (1.0 + (a - 1.0) * ka_ref[...])
    b = kkn * a
    bonus_s[...] = _head_sum(r * km * rk_ref[...], ones_blk) * v
    v_s[...] = v
    r_s[...] = r

    log_d = -jnp.exp(w)
    c0 = lax.broadcasted_iota(jnp.int32, (CHUNK, CHUNK), 0)
    c1 = lax.broadcasted_iota(jnp.int32, (CHUNK, CHUNK), 1)
    upto = (c1 <= c0).astype(BF16)
    for j in range(nc):
        rows = slice(j * CHUNK, (j + 1) * CHUNK)
        ld = log_d[rows]
        c = _dot_sel_lhs(upto, ld)
        c_end = c[CHUNK - 1:CHUNK]
        grow = jnp.exp(-c)
        to_end = jnp.exp(c_end - c)
        a_s[rows] = kkn[rows] * jnp.exp(c - ld)
        bt_s[rows] = b[rows] * grow
        kt_s[rows] = km[rows] * grow
        bp_s[rows] = b[rows] * to_end
        kp_s[rows] = km[rows] * to_end
        gam_s[j * SUBLANES:(j + 1) * SUBLANES] = jnp.broadcast_to(jnp.exp(c_end), (SUBLANES, RWKV_W))

    lo = lax.broadcasted_iota(jnp.int32, (CHUNK, LANES), 1) < RWKV_DH
    strict_lower = jnp.logical_and(same_head, i1 < i0)

    def by_pair(x):
        return jnp.stack([x[:, p * LANES:(p + 1) * LANES] for p in range(RW_PAIRS)], axis=0)

    def stacked(ref, rows):
        x = by_pair(ref[rows, :])
        return jnp.concatenate([jnp.where(lo, x, 0.0), jnp.where(lo, 0.0, x)], axis=1)

    def chunk_local(j, carry):
        rows = pl.ds(pl.multiple_of(j * CHUNK, CHUNK), CHUNK)
        a2, bp2 = stacked(a_s, rows), stacked(bp_s, rows)
        m_all = _bmm_nt(a2.astype(BF16),
                        jnp.concatenate([stacked(bt_s, rows), stacked(kt_s, rows)], axis=1).astype(BF16))
        m_b = jnp.where(strict_lower, m_all[:, :, :LANES], 0.0)
        m_k = jnp.where(strict_lower, m_all[:, :, LANES:], 0.0)
        t_mi = _unit_lower_inverse_minus_eye(m_b, i0, i1)
        z2 = bp2 + _bmm3_tn(t_mi, bp2)
        yk = _bmm3_tn(jnp.concatenate([a2, m_k], axis=2), z2)
        kpp = stacked(kp_s, rows) - yk[:, LANES:]
        uu = _bmm_tn(stacked(v_s, rows).astype(BF16), kpp.astype(BF16))
        y_s[j] = yk[:, :LANES].astype(BF16)
        u_s[j] = uu[:, :RWKV_DH] + uu[:, RWKV_DH:]
        return carry

    lax.fori_loop(0, nc, chunk_local, 0)

    def chunk_advance(j, carry):
        rows = pl.ds(pl.multiple_of(j * CHUNK, CHUNK), CHUNK)
        gam = by_pair(gam_s[pl.ds(pl.multiple_of(j * SUBLANES, SUBLANES), SUBLANES)])[:, 0:1]
        s = state_ref[...]
        s_hi, s_lo = _split(s)
        y = y_s[j]
        s = s * gam - _bmm(jnp.concatenate([s_hi, s_lo], axis=2), jnp.concatenate([y, y], axis=1)) + u_s[j]
        state_ref[...] = s
        o2 = _bmm_nt(stacked(r_s, rows).astype(BF16), s.astype(BF16))
        o_pair = jnp.concatenate([o2[:, :CHUNK], o2[:, CHUNK:]], axis=2)
        o_s[rows, :] = jnp.concatenate([o_pair[p] for p in range(RW_PAIRS)], axis=1)
        return carry

    lax.fori_loop(0, nc, chunk_advance, 0)

    o = o_s[...]
    mu = _head_sum(o, ones_blk) * (1.0 / RWKV_DH)
    oc = o - mu
    var = _head_sum(oc * oc, ones_blk) * (1.0 / RWKV_DH)
    y = oc * lax.rsqrt(var + RWKV_GN_EPS) * gng_ref[...] + gnb_ref[...]
    o_ref[...] = ((y + bonus_s[...]) * g_s[...]).astype(o_ref.dtype)


def _rwkv(proj, small, mix, w_up, w0, a_up, a0, g_up, k_k, k_a, r_k, gn_g, gn_b):
    length = proj.shape[0]
    nc = _chunks_per_step(length, (5, 4, 2, 1))
    t = nc * CHUNK
    w = RWKV_W
    mix_s = mix[3 * w:][None]
    wup_pad = jnp.zeros((LANES, w), F32).at[:RWKV_RANK_W].set(w_up)
    aup_pad = jnp.zeros((LANES, w), F32).at[RWKV_RANK_W:].set(a_up)
    col = lambda c0: pl.BlockSpec((t, w), lambda c: (c, c0 // w))
    vec = pl.BlockSpec((1, w), lambda c: (0, 0))
    mat = pl.BlockSpec((LANES, w), lambda c: (0, 0))
    wide = lambda: pltpu.VMEM((t, w), F32)
    carry = lambda n: pltpu.VMEM((SUBLANES, n), F32)
    return pl.pallas_call(
        functools.partial(_rwkv_kernel, nc=nc),
        grid=(length // t,),
        in_specs=[col(RW_R0), col(RW_K0), col(RW_V0), pl.BlockSpec((t, 2 * LANES), lambda c: (c, 0)),
                  vec, vec, vec, pl.BlockSpec((1, 2 * LANES), lambda c: (0, 0)),
                  mat, vec, mat, vec, mat, vec, vec, vec, vec, vec],
        out_specs=pl.BlockSpec((t, w), lambda c: (c, 0)),
        out_shape=jax.ShapeDtypeStruct((length, w), BF16),
        scratch_shapes=[pltpu.VMEM((RW_PAIRS, RWKV_DH, LANES), F32),
                        carry(w), carry(w), carry(w), carry(2 * LANES),
                        wide(), wide(), wide(), wide(), wide(), wide(), wide(),
                        pltpu.VMEM((nc * SUBLANES, w), F32), wide(), wide(), wide(),
                        pltpu.VMEM((nc, RW_PAIRS, LANES, LANES), BF16),
                        pltpu.VMEM((nc, RW_PAIRS, RWKV_DH, LANES), F32)],
        compiler_params=_cparams(("arbitrary",)),
        name="rwkv7",
    )(proj, proj, proj, small, mix[None, :w], mix[None, w:2 * w], mix[None, 2 * w:3 * w], mix_s,
      wup_pad, w0[None], aup_pad, a0[None], g_up, k_k[None], k_a[None], r_k.reshape(1, w),
      gn_g[None], gn_b[None])


def _merge_kernel(oret_ref, ogla_ref, orw_ref, wret_ref, wgla_ref, wrw_ref, gret_ref, ggla_ref, grw_ref, o_ref):
    acc = _sigmoid(gret_ref[...].astype(F32)) * _dot(oret_ref[...], wret_ref[...])
    acc = acc + _sigmoid(ggla_ref[...].astype(F32)) * _dot(ogla_ref[...], wgla_ref[...])
    acc = acc + _sigmoid(grw_ref[...].astype(F32)) * _dot(orw_ref[...], wrw_ref[...])
    o_ref[...] = acc.astype(o_ref.dtype)


def _merge(o_ret, o_gla, o_rw, w_ret, w_gla, w_rw, layer, gates):
    m = o_ret.shape[0]
    tm = _pick(m, (640, 256, 128, 64))
    tn = 512
    act = lambda n: pl.BlockSpec((tm, n), lambda i, j: (i, 0))
    wgt = lambda n: pl.BlockSpec((None, n, tn), lambda i, j: (layer, 0, j))
    gate = lambda b: pl.BlockSpec((tm, tn), lambda i, j: (i, b * (D_MODEL // tn) + j))
    return pl.pallas_call(
        _merge_kernel,
        grid=(m // tm, D_MODEL // tn),
        in_specs=[act(RET_W), act(GLA_WV), act(RWKV_W), wgt(RET_W), wgt(GLA_WV), wgt(RWKV_W),
                  gate(0), gate(1), gate(2)],
        out_specs=pl.BlockSpec((tm, tn), lambda i, j: (i, j)),
        out_shape=jax.ShapeDtypeStruct((m, D_MODEL), BF16),
        compiler_params=_cparams(("parallel", "parallel")),
        name="branch_merge",
    )(o_ret, o_gla, o_rw, w_ret, w_gla, w_rw, gates, gates, gates)


def _router_kernel(h_ref, rw_ref, bias_ref, eidx_ref, ewts_ref):
    tm = h_ref.shape[0]
    (hh, hl), (wh, wl) = _split(h_ref[...]), _split(rw_ref[...])
    logits = (_dot(hh, wl) + _dot(hl, wh)) + _dot(hh, wh)
    s_all = _sigmoid(logits.T[0:N_EXPERTS])
    sel_all = s_all + bias_ref[...]
    ids = lax.broadcasted_iota(jnp.int32, (EXPERTS_PER_GROUP, tm), 0)
    neg = jnp.float32(-jnp.inf)
    best_score = best_e1 = best_e2 = best_w1 = best_w2 = None
    for gidx in range(N_GROUPS):
        rows = slice(gidx * EXPERTS_PER_GROUP, (gidx + 1) * EXPERTS_PER_GROUP)
        sel, s = sel_all[rows], s_all[rows]
        m1 = jnp.max(sel, axis=0, keepdims=True)
        i1 = jnp.min(jnp.where(sel == m1, ids, EXPERTS_PER_GROUP), axis=0, keepdims=True)
        rest = jnp.where(ids == i1, neg, sel)
        m2 = jnp.max(rest, axis=0, keepdims=True)
        i2 = jnp.min(jnp.where(rest == m2, ids, EXPERTS_PER_GROUP), axis=0, keepdims=True)
        score = m1 + m2
        w1 = jnp.sum(jnp.where(ids == i1, s, 0.0), axis=0, keepdims=True)
        w2 = jnp.sum(jnp.where(ids == i2, s, 0.0), axis=0, keepdims=True)
        e1 = i1 + gidx * EXPERTS_PER_GROUP
        e2 = i2 + gidx * EXPERTS_PER_GROUP
        if gidx == 0:
            best_score, best_e1, best_e2, best_w1, best_w2 = score, e1, e2, w1, w2
        else:
            take = score > best_score
            best_score = jnp.where(take, score, best_score)
            best_e1, best_e2 = jnp.where(take, e1, best_e1), jnp.where(take, e2, best_e2)
            best_w1, best_w2 = jnp.where(take, w1, best_w1), jnp.where(take, w2, best_w2)
    total = best_w1 + best_w2
    slot = lax.broadcasted_iota(jnp.int32, (SUBLANES, tm), 0)
    eidx_ref[...] = jnp.where(slot == 0, best_e1, jnp.where(slot == 1, best_e2, 0))
    ewts_ref[...] = jnp.where(slot == 0, best_w1 / total, jnp.where(slot == 1, best_w2 / total, 0.0))


def _router(h, router_w, router_bias):
    m = h.shape[0]
    tm = _pick(m, (256, 128))
    rw_pad = jnp.zeros((D_MODEL, LANES), F32).at[:, :N_EXPERTS].set(router_w)
    return pl.pallas_call(
        _router_kernel,
        grid=(m // tm,),
        in_specs=[pl.BlockSpec((tm, D_MODEL), lambda i: (i, 0)),
                  pl.BlockSpec((D_MODEL, LANES), lambda i: (0, 0)),
                  pl.BlockSpec((N_EXPERTS, 1), lambda i: (0, 0))],
        out_specs=(pl.BlockSpec((SUBLANES, tm), lambda i: (0, i)), pl.BlockSpec((SUBLANES, tm), lambda i: (0, i))),
        out_shape=(jax.ShapeDtypeStruct((SUBLANES, m), jnp.int32), jax.ShapeDtypeStruct((SUBLANES, m), F32)),
        compiler_params=_cparams(("parallel",)),
        name="router",
    )(h, rw_pad, router_bias[:, None])


def _row_copy(src_hbm, row, dst_buf, sem):
    return pltpu.make_async_copy(src_hbm.at[pl.ds(row, 1)], dst_buf, sem)


def _ffn_kernel(te_ref, rt_ref, na_ref, h_hbm, w1_ref, w3_ref, w2_ref, y_ref, xbuf, sem, *, tm):
    del te_ref
    i = pl.program_id(0)
    n = pl.num_programs(0)
    n_active = na_ref[0]
    ahead = FFN_SLOTS - 1
    slot = i % FFN_SLOTS

    def start_tile(tile, dst_slot):
        for r in range(tm):
            _row_copy(h_hbm, rt_ref[tile * tm + r], xbuf.at[dst_slot, pl.ds(r, 1)],
                      sem.at[dst_slot]).start(priority=r % DMA_THREADS)

    def wait_tile(dst_slot):
        def body(r, c):
            _row_copy(h_hbm, 0, xbuf.at[dst_slot, pl.ds(r, 1)], sem.at[dst_slot]).wait()
            return c
        lax.fori_loop(0, tm, body, 0, unroll=DMA_UNROLL)

    @pl.when(i == 0)
    def _():
        for t in range(ahead):
            start_tile(jnp.minimum(t, n - 1), t)

    wait_tile(slot)
    nxt = jnp.minimum(i + ahead, n - 1)
    nxt_slot = (i + ahead) % FFN_SLOTS

    @pl.when(i < n_active)
    def _():
        start_tile(nxt, nxt_slot)
        x = _unpack_halves(xbuf[slot]).astype(BF16)
        h1 = _dot(x, w1_ref[...])
        h3 = _dot(x, w3_ref[...])
        he = (h1 * _sigmoid(h1)) * h3
        y_ref[...] = _pack_halves(_dot(he.astype(BF16), w2_ref[...]))

    @pl.when(i >= n_active)
    def _():
        start_tile(nxt, nxt_slot)
        y_ref[...] = jnp.zeros_like(y_ref)

    @pl.when(i == n - 1)
    def _():
        for t in range(1, FFN_SLOTS):
            wait_tile((i + t) % FFN_SLOTS)


def _expert_ffn(hp, tile_expert, row_token, n_active, layer, w1, w3, w2):
    half = hp.shape[1]
    d = 2 * half
    tm = MOE_TM
    n_tiles = tile_expert.shape[0]
    wspec = lambda shape: pl.BlockSpec((None, None) + shape, lambda i, te, rt, na: (layer, te[i], 0, 0))
    return pl.pallas_call(
        functools.partial(_ffn_kernel, tm=tm),
        grid_spec=pltpu.PrefetchScalarGridSpec(
            num_scalar_prefetch=3,
            grid=(n_tiles,),
            in_specs=[pl.BlockSpec(memory_space=pl.ANY), wspec((d, D_FF_PAD)), wspec((d, D_FF_PAD)),
                      wspec((D_FF_PAD, d))],
            out_specs=pl.BlockSpec((tm, half), lambda i, te, rt, na: (i, 0)),
            scratch_shapes=[pltpu.VMEM((FFN_SLOTS, tm, half), jnp.uint32), pltpu.SemaphoreType.DMA((FFN_SLOTS,))]),
        out_shape=jax.ShapeDtypeStruct((n_tiles * tm, half), jnp.uint32),
        compiler_params=_cparams(("arbitrary",)),
        name="expert_ffn",
    )(tile_expert, row_token, n_active, hp, w1, w3, w2)


def _combine_ln_kernel(p1_ref, p2_ref, h_ref, y_hbm, w_ref, g_ref, b_ref, o_ref, ob_ref, ybuf, sem, *, tm, n_mask):
    i = pl.program_id(0)
    n = pl.num_programs(0)
    slot = i % 2

    def start_tile(tile, dst_slot):
        for r in range(tm):
            _row_copy(y_hbm, p1_ref[tile * tm + r], ybuf.at[dst_slot, 0, pl.ds(r, 1)], sem.at[dst_slot]).start(priority=0)
            _row_copy(y_hbm, p2_ref[tile * tm + r], ybuf.at[dst_slot, 1, pl.ds(r, 1)],
                      sem.at[dst_slot]).start(priority=DMA_THREADS - 1)

    def wait_tile(dst_slot):
        def body(r, c):
            _row_copy(y_hbm, 0, ybuf.at[dst_slot, 0, pl.ds(r, 1)], sem.at[dst_slot]).wait()
            _row_copy(y_hbm, 0, ybuf.at[dst_slot, 1, pl.ds(r, 1)], sem.at[dst_slot]).wait()
            return c
        lax.fori_loop(0, tm, body, 0, unroll=DMA_UNROLL)

    @pl.when(i == 0)
    def _():
        start_tile(0, 0)

    wait_tile(slot)
    start_tile(jnp.minimum(i + 1, n - 1), 1 - slot)
    w = w_ref[...]
    ffn = w[:, 0:1] * _unpack_halves(ybuf[slot, 0]) + w[:, 1:2] * _unpack_halves(ybuf[slot, 1])
    x = h_ref[...] * DN_ALPHA + ffn
    _emit_ln(_layernorm_rows(x, g_ref[...], b_ref[...]), o_ref, ob_ref, n_mask, tm)

    @pl.when(i == n - 1)
    def _():
        wait_tile(1 - slot)


def _combine_ln(h, y, pos1, pos2, wts, g, b, n_mask, frames_only):
    m, d = h.shape
    tm = FRONT if frames_only else _pick(m, (256, 128, 64))
    row = pl.BlockSpec((tm, d), lambda i, p1, p2: (i, 0))
    vec = pl.BlockSpec((1, d), lambda i, p1, p2: (0, 0))
    if frames_only:
        out_rows, out_row = m - tm, pl.BlockSpec((tm, d), lambda i, p1, p2: (jnp.maximum(i - 1, 0), 0))
    else:
        out_rows, out_row = m, row
    return pl.pallas_call(
        functools.partial(_combine_ln_kernel, tm=tm, n_mask=n_mask),
        grid_spec=pltpu.PrefetchScalarGridSpec(
            num_scalar_prefetch=2,
            grid=(m // tm,),
            in_specs=[row, pl.BlockSpec(memory_space=pl.ANY),
                      pl.BlockSpec((tm, 2), lambda i, p1, p2: (i, 0)), vec, vec],
            out_specs=(out_row, row),
            scratch_shapes=[pltpu.VMEM((2, 2, tm, d // 2), jnp.uint32), pltpu.SemaphoreType.DMA((2,))]),
        out_shape=(jax.ShapeDtypeStruct((out_rows, d), F32), jax.ShapeDtypeStruct((m, d), BF16)),
        compiler_params=_cparams(("arbitrary",)),
        name="moe_combine_layernorm",
    )(pos1, pos2, h, y, wts, g[None], b[None])


def _dispatch_plan(eidx, tm):
    m = eidx.shape[1]
    flat_e = jnp.concatenate([eidx[0], eidx[1]])
    tok = jnp.arange(m, dtype=jnp.int32)
    flat_t = jnp.concatenate([tok, tok])
    onehot = (flat_e[:, None] == jnp.arange(N_EXPERTS, dtype=jnp.int32)[None, :]).astype(jnp.int32)
    rank = jnp.take_along_axis(jnp.cumsum(onehot, axis=0) - onehot, flat_e[:, None], axis=1)[:, 0]
    counts = jnp.sum(onehot, axis=0)
    padded = ((counts + tm - 1) // tm) * tm
    ends = jnp.cumsum(padded)
    offs = ends - padded
    pos = (offs[flat_e] + rank).astype(jnp.int32)
    n_tiles = (2 * m) // tm + N_EXPERTS
    row_token = jnp.zeros((n_tiles * tm,), jnp.int32).at[pos].set(flat_t)
    n_active = (ends[-1] // tm).astype(jnp.int32)
    tile_id = jnp.arange(n_tiles, dtype=jnp.int32)
    tile_id = jnp.minimum(tile_id, n_active - 1)
    tile_expert = (jnp.sum(tile_id[:, None] >= (offs // tm)[None, :], axis=1) - 1).astype(jnp.int32)
    return tile_expert, row_token, n_active.reshape(1), pos[:m], pos[m:]


def _moe_ln(h, hp, router_w, router_bias, layer, w1, w3, w2, g, b, n_mask, frames_only=False):
    eidx, ewts = _router(h, router_w, router_bias)
    tile_expert, row_token, n_active, pos1, pos2 = _dispatch_plan(eidx, MOE_TM)
    y = _expert_ffn(hp, tile_expert, row_token, n_active, layer, w1, w3, w2)
    return _combine_ln(h, y, pos1, pos2, ewts[0:2].T, g, b, n_mask, frames_only)


def _cast_pad_kernel(x_ref, o_ref):
    rows, cols = x_ref.shape
    o_ref[...] = jnp.zeros(o_ref.shape, o_ref.dtype)
    o_ref[0:rows, 0:cols] = x_ref[...].astype(o_ref.dtype)


def _cast_pad(w, rows_out, cols_out, block_rows, block_cols):
    n_l, n_e, r, c = w.shape
    in_rows = r if rows_out != r else block_rows
    in_cols = c if cols_out != c else block_cols
    out_rows = rows_out if rows_out != r else block_rows
    out_cols = cols_out if cols_out != c else block_cols
    grid = (n_l * n_e, rows_out // out_rows, cols_out // out_cols)
    idx = lambda e, i, j: (e // n_e, e % n_e, i, j)
    return pl.pallas_call(
        _cast_pad_kernel,
        grid=grid,
        in_specs=[pl.BlockSpec((None, None, in_rows, in_cols), idx)],
        out_specs=pl.BlockSpec((None, None, out_rows, out_cols), idx),
        out_shape=jax.ShapeDtypeStruct((n_l, n_e, rows_out, cols_out), BF16),
        compiler_params=_cparams(("parallel", "parallel", "parallel")),
        name="cast_pad_weights",
    )(w)


def _cast_pad_transposed_kernel(x_ref, o_ref):
    rows, cols = x_ref.shape
    x = jnp.concatenate([x_ref[...], jnp.zeros((o_ref.shape[1] - rows, cols), x_ref.dtype)], axis=0)
    o_ref[...] = x.T.astype(o_ref.dtype)


def _cast_pad_transposed(wt, cols_out, block):
    n_l, n_e, c, r = wt.shape
    return pl.pallas_call(
        _cast_pad_transposed_kernel,
        grid=(n_l * n_e, r // block),
        in_specs=[pl.BlockSpec((None, None, c, block), lambda e, i: (e // n_e, e % n_e, 0, i))],
        out_specs=pl.BlockSpec((None, None, block, cols_out), lambda e, i: (e // n_e, e % n_e, i, 0)),
        out_shape=jax.ShapeDtypeStruct((n_l, n_e, r, cols_out), BF16),
        compiler_params=_cparams(("parallel", "parallel")),
        name="cast_pad_weights_t",
    )(wt)


def _prep_experts(w1, w3, w2):
    d = w1.shape[2]
    w1 = _cast_pad_transposed(jnp.swapaxes(w1, 2, 3), D_FF_PAD, 1024)
    w3 = _cast_pad_transposed(jnp.swapaxes(w3, 2, 3), D_FF_PAD, 1024)
    w2 = _cast_pad(w2, D_FF_PAD, d, D_FF_PAD, 2048)
    return w1, w3, w2


def _rotary_tables(length):
    half = RET_DK // 2
    pos = jnp.arange(length, dtype=jnp.int32) - N_MASK
    inv = ROPE_BASE ** (-jnp.arange(half, dtype=F32) / half)
    ang = pos.astype(F32)[:, None] * inv[None, :]
    cos, sin = jnp.cos(ang), jnp.sin(ang)
    return jnp.concatenate([cos, cos], axis=1), jnp.concatenate([-sin, sin], axis=1)


def _mixer(hb, cos, sin, layer, w_in, ret_gn_g, ret_gn_b, gla_a_up, gla_a_bias, gla_gn_g, gla_gn_b, rw_mix, rw_w_up,
           rw_w0, rw_a_up, rw_a0, rw_g_up, rw_k_k, rw_k_a, rw_r_k, rw_gn_g, rw_gn_b, w_br_ret, w_br_gla,
           w_br_rw, w_out):
    rows = lambda a, b: lax.slice(w_in, (layer, a, 0), (layer + 1, b, D_MODEL))
    w_small = jnp.concatenate([rows(W_IN_SM0, W_IN_GATE0), rows(W_IN_GA0, W_IN_RW0),
                               jnp.zeros((1, LANES - GLA_RANK, D_MODEL), F32)], axis=1)
    proj = _matmul_nt(hb, w_in, layer, 0, N_HEAD, BF16)
    proj_rw = _matmul_nt(hb, w_in, layer, W_IN_RW0, W_IN_SM0 - W_IN_RW0, BF16)
    gates = _matmul_nt(hb, w_in, layer, W_IN_GATE0, w_in.shape[1] - W_IN_GATE0, BF16)
    small = _matmul_nt(hb, w_small, 0, 0, N_SMALL, F32, tn=N_SMALL)
    o_ret = _retention(proj, cos, sin, ret_gn_g, ret_gn_b)
    o_gla = _gla(proj, small, gla_a_up, gla_a_bias, gla_gn_g, gla_gn_b)
    o_rw = _rwkv(proj_rw, small, rw_mix, rw_w_up, rw_w0, rw_a_up, rw_a0, rw_g_up, rw_k_k, rw_k_a, rw_r_k,
                 rw_gn_g, rw_gn_b)
    merged = _merge(o_ret, o_gla, o_rw, w_br_ret, w_br_gla, w_br_rw, layer, gates)
    return _matmul(merged, w_out, F32, 512, layer=layer)


def kernel(x, meta, ln_in_g, ln_in_b, w_in, ret_gn_g, ret_gn_b, gla_a_up, gla_a_bias, gla_gn_g, gla_gn_b, rw_mix, rw_w_up, rw_w0, rw_a_up, rw_a0, rw_g_up, rw_k_k, rw_k_a, rw_r_k, rw_gn_g, rw_gn_b, w_br_ret, w_br_gla, w_br_rw, w_out, ln1_g, ln1_b, router_w, router_bias, exp_w1, exp_w3, exp_w2, ln2_g, ln2_b):
    batch, seq, d = x.shape
    assert batch == 1 and d == D_MODEL
    assert meta.shape == (N_META, d)
    cos, sin = _rotary_tables(FRONT + seq)
    h, hb = _embed_layer_norm(x[0], meta.astype(x.dtype), ln_in_g, ln_in_b)
    w_in = jnp.swapaxes(w_in, 1, 2)
    w_out = w_out.astype(BF16)
    w_br_ret, w_br_gla, w_br_rw = w_br_ret.astype(BF16), w_br_gla.astype(BF16), w_br_rw.astype(BF16)
    w1, w3, w2 = _prep_experts(exp_w1, exp_w3, exp_w2)
    for l in range(DEPTH):
        mix = _mixer(hb, cos, sin, l, w_in, ret_gn_g[l], ret_gn_b[l], gla_a_up[l], gla_a_bias[l], gla_gn_g[l],
                     gla_gn_b[l], rw_mix[l], rw_w_up[l], rw_w0[l], rw_a_up[l], rw_a0[l], rw_g_up[l], rw_k_k[l],
                     rw_k_a[l], rw_r_k[l], rw_gn_g[l], rw_gn_b[l], w_br_ret, w_br_gla, w_br_rw, w_out)
        h, hp = _layer_norm(h, mix, ln1_g[l], ln1_b[l], 0, packed=True)
        h, hb = _moe_ln(h, hp, router_w, router_bias, l, w1, w3, w2, ln2_g[l], ln2_b[l], N_MASK,
                        frames_only=(l == DEPTH - 1))
    return h[None]
```

```python
import functools

import jax
import jax.numpy as jnp
from jax import lax
from jax.experimental import pallas as pl
from jax.experimental.pallas import tpu as pltpu

F32 = jnp.float32
BF16 = jnp.bfloat16

D_MODEL = 4096
DEPTH = 2
CHUNK = 64
N_META = 16

RET_HEADS, RET_DK, RET_DV = 12, 128, 128
RET_W = RET_HEADS * RET_DV
ROPE_BASE = 10000.0

GLA_HEADS, GLA_DK, GLA_DV = 6, 128, 256
GLA_WK = GLA_HEADS * GLA_DK
GLA_WV = GLA_HEADS * GLA_DV
GLA_RANK = 16
GLA_TAU = 16.0

RWKV_HEADS, RWKV_DH = 16, 64
RWKV_W = RWKV_HEADS * RWKV_DH
RWKV_RANK_W, RWKV_RANK_A, RWKV_RANK_G = 64, 64, 128

N_EXPERTS = 32
N_GROUPS = 4
EXPERTS_PER_GROUP = N_EXPERTS // N_GROUPS
D_FF = 704

DN_ALPHA = (2 * DEPTH) ** 0.25
LN_EPS = 1e-5
GN_EPS = 1e-5
RWKV_GN_EPS = 64e-5
NORM_EPS = 1e-12

LANES = 128
SUBLANES = 8
VMEM_LIMIT_BYTES = 56 * 1024 * 1024

RET_Q0, RET_K0, RET_V0, RET_G0 = 0, RET_W, 2 * RET_W, 3 * RET_W
GLA_Q0 = 4 * RET_W
GLA_K0 = GLA_Q0 + GLA_WK
GLA_V0 = GLA_K0 + GLA_WK
GLA_G0 = GLA_V0 + GLA_WV
N_HEAD = GLA_G0 + GLA_WV
W_IN_GA0 = N_HEAD
W_IN_RW0 = W_IN_GA0 + GLA_RANK
W_IN_SM0 = W_IN_RW0 + 3 * RWKV_W
W_IN_GATE0 = W_IN_SM0 + RWKV_RANK_W + RWKV_RANK_A + RWKV_RANK_G
RW_R0, RW_K0, RW_V0 = 0, RWKV_W, 2 * RWKV_W
N_SMALL = 3 * LANES
D_FF_PAD = 768

ROW_TILE = 256
FRONT = ROW_TILE
N_MASK = FRONT - N_META
MOE_TM = 256
DMA_UNROLL = 8
DMA_THREADS = 2
FFN_SLOTS = 3


def _cparams(sem):
    return pltpu.CompilerParams(dimension_semantics=sem, vmem_limit_bytes=VMEM_LIMIT_BYTES)


def _pick(n, cands):
    for c in cands:
        if n % c == 0:
            return c
    raise ValueError(f"no tile in {cands} divides {n}")


def _dot(a, b):
    return jnp.dot(a, b, preferred_element_type=F32)


def _split(x):
    hi = x.astype(BF16)
    return hi, (x - hi.astype(F32)).astype(BF16)


def _split3(x):
    hi = x.astype(BF16)
    rest = x - hi.astype(F32)
    mid = rest.astype(BF16)
    return hi, mid, (rest - mid.astype(F32)).astype(BF16)


def _dot_hi(a, b):
    (ah, al), (bh, bl) = _split(a), _split(b)
    return _dot(jnp.concatenate([ah, ah, al], axis=1), jnp.concatenate([bh, bl, bh], axis=0))


def _dot_sel_lhs(sel, b):
    return _dot(jnp.concatenate([sel, sel, sel], axis=1), jnp.concatenate(_split3(b), axis=0))


def _dot_sel_rhs16(a, sel):
    return _dot(jnp.concatenate(_split(a), axis=1), jnp.concatenate([sel, sel], axis=0))


def _dot_nt(a, b):
    return lax.dot_general(a, b, (((1,), (1,)), ((), ())), preferred_element_type=F32)


def _sigmoid(x):
    return 1.0 / (1.0 + jnp.exp(-x))


def _softplus(x):
    return jnp.maximum(x, 0.0) + jnp.log(1.0 + jnp.exp(-jnp.abs(x)))


def _mm_kernel(x_ref, w_ref, o_ref):
    o_ref[...] = _dot(x_ref[...], w_ref[...]).astype(o_ref.dtype)


def _matmul(x, w, out_dtype, tn, layer=None, n=None):
    m, k = x.shape
    n = w.shape[-1] if n is None else n
    assert n % tn == 0
    tm = _pick(m, (1280, 640, 256, 128, 64))
    if w.ndim == 3:
        w_spec = pl.BlockSpec((None, k, tn), lambda i, j: (layer, 0, j))
    else:
        w_spec = pl.BlockSpec((k, tn), lambda i, j: (0, j))
    return pl.pallas_call(
        _mm_kernel,
        grid=(m // tm, n // tn),
        in_specs=[pl.BlockSpec((tm, k), lambda i, j: (i, 0)), w_spec],
        out_specs=pl.BlockSpec((tm, tn), lambda i, j: (i, j)),
        out_shape=jax.ShapeDtypeStruct((m, n), out_dtype),
        compiler_params=_cparams(("parallel", "parallel")),
        name="dense_matmul",
    )(x, w)


def _mm_nt_kernel(x_ref, wt_ref, o_ref, wb_ref):
    @pl.when(pl.program_id(1) == 0)
    def _():
        wb_ref[...] = wt_ref[0].astype(BF16)

    o_ref[...] = _dot_nt(x_ref[...], wb_ref[...]).astype(o_ref.dtype)


def _matmul_nt(x, wt, layer, row0, n, out_dtype, tn=512):
    m, k = x.shape
    assert n % tn == 0 and row0 % 16 == 0 and wt.shape[2] == k
    tm = _pick(m, (1280, 640, 256, 128, 64))
    wt_spec = pl.BlockSpec((pl.Element(1), pl.Element(tn), pl.Element(k)),
                           lambda j, i: (layer, pl.multiple_of(row0 + j * tn, 16), 0))
    return pl.pallas_call(
        _mm_nt_kernel,
        grid=(n // tn, m // tm),
        in_specs=[pl.BlockSpec((tm, k), lambda j, i: (i, 0)), wt_spec],
        out_specs=pl.BlockSpec((tm, tn), lambda j, i: (i, j)),
        out_shape=jax.ShapeDtypeStruct((m, n), out_dtype),
        scratch_shapes=[pltpu.VMEM((tn, k), BF16)],
        compiler_params=_cparams(("arbitrary", "arbitrary")),
        name="dense_matmul_nt",
    )(x, wt)


def _layernorm_rows(x, g, b):
    mu = jnp.mean(x, axis=-1, keepdims=True)
    xc = x - mu
    var = jnp.mean(xc * xc, axis=-1, keepdims=True)
    return xc * lax.rsqrt(var + LN_EPS) * g + b


HIGH_HALF = 0xFFFF0000


def _pack_halves(y):
    half = y.shape[1] // 2
    lo = pltpu.bitcast(y[:, :half].astype(BF16).astype(F32), jnp.uint32) >> 16
    hi = pltpu.bitcast(y[:, half:].astype(BF16).astype(F32), jnp.uint32) & jnp.uint32(HIGH_HALF)
    return lo | hi


def _unpack_halves(w):
    lo = pltpu.bitcast(w << 16, F32)
    hi = pltpu.bitcast(w & jnp.uint32(HIGH_HALF), F32)
    return jnp.concatenate([lo, hi], axis=1)


def _emit_ln(y, o_ref, ob_ref, n_mask, tm):
    o_ref[...] = y
    if ob_ref.dtype == jnp.uint32:
        ob_ref[...] = _pack_halves(y)
        return
    yb = y.astype(BF16)
    if n_mask:
        row = pl.program_id(0) * tm + lax.broadcasted_iota(jnp.int32, (tm, 1), 0)
        yb = jnp.where(row >= n_mask, yb, jnp.zeros_like(yb))
    ob_ref[...] = yb


def _embed_ln_kernel(x_ref, meta_ref, g_ref, b_ref, o_ref, ob_ref, *, n_mask, tm):
    front = jnp.concatenate([jnp.zeros((n_mask, x_ref.shape[1]), F32), meta_ref[...]], axis=0)
    x = jnp.where(pl.program_id(0) == 0, front, x_ref[...])
    _emit_ln(_layernorm_rows(x, g_ref[...], b_ref[...]), o_ref, ob_ref, n_mask, tm)


def _embed_layer_norm(x, meta, g, b):
    seq, d = x.shape
    tm = FRONT
    assert seq % tm == 0 and N_MASK + meta.shape[0] == tm
    m = tm + seq
    row = pl.BlockSpec((tm, d), lambda i: (i, 0))
    vec = pl.BlockSpec((1, d), lambda i: (0, 0))
    return pl.pallas_call(
        functools.partial(_embed_ln_kernel, n_mask=N_MASK, tm=tm),
        grid=(m // tm,),
        in_specs=[pl.BlockSpec((tm, d), lambda i: (jnp.maximum(i - 1, 0), 0)),
                  pl.BlockSpec(meta.shape, lambda i: (0, 0)), vec, vec],
        out_specs=(row, row),
        out_shape=(jax.ShapeDtypeStruct((m, d), F32), jax.ShapeDtypeStruct((m, d), BF16)),
        compiler_params=_cparams(("parallel",)), name="embed_layernorm",
    )(x, meta, g[None], b[None])


def _groupnorm_lanes(o, g, b, eps):
    mu = jnp.mean(o, axis=-1, keepdims=True)
    oc = o - mu
    var = jnp.mean(oc * oc, axis=-1, keepdims=True)
    return oc * lax.rsqrt(var + eps) * g + b


def _chunks_per_step(length, cands):
    return _pick(length // CHUNK, cands)


def _by_head(x, n_heads, width):
    return jnp.stack([x[:, h * width:(h + 1) * width] for h in range(n_heads)], axis=0)


def _heads_to_lanes(x):
    return jnp.concatenate([x[h] for h in range(x.shape[0])], axis=1)


def _ret_kernel(q_ref, k_ref, v_ref, g_ref, cos_ref, sin_ref, intra_ref, qdec_ref, kdec_ref, cdec_ref,
                gng_ref, gnb_ref, o_ref, state_ref, *, nc):
    @pl.when(pl.program_id(0) == 0)
    def _():
        state_ref[...] = jnp.zeros_like(state_ref)

    cos, sin = cos_ref[...], sin_ref[...]
    half = RET_DK // 2

    def rotated(ref):
        x = ref[...].astype(F32)
        return jnp.stack([xh * cos + pltpu.roll(xh, half, 1) * sin
                          for xh in (x[:, h * RET_DK:(h + 1) * RET_DK] for h in range(RET_HEADS))], axis=0)

    q = rotated(q_ref) * (RET_DK ** -0.5)
    k = rotated(k_ref)
    v = _by_head(v_ref[...], RET_HEADS, RET_DV)
    intra, qdec, kdec = intra_ref[...], qdec_ref[...], kdec_ref[...]
    cdec = cdec_ref[...][:, 0:1]
    state = state_ref[...]
    outs = []
    for j in range(nc):
        sl = slice(j * CHUNK, (j + 1) * CHUNK)
        qc, kc, vc = q[:, sl], k[:, sl], v[:, sl]
        s = _bmm_nt(qc.astype(BF16), kc.astype(BF16)) * intra
        outs.append(_bmm(s.astype(BF16), vc) + _bmm((qc * qdec).astype(BF16), state.astype(BF16)))
        state = state * cdec + _bmm_tn((kc * kdec).astype(BF16), vc)
    state_ref[...] = state
    y = _groupnorm_lanes(jnp.concatenate(outs, axis=1), gng_ref[...], gnb_ref[...], GN_EPS)
    gate = _by_head(g_ref[...].astype(F32), RET_HEADS, RET_DV)
    o_ref[...] = _heads_to_lanes(y * (gate * _sigmoid(gate))).astype(o_ref.dtype)


def _retention(proj, cos, sin, gn_g, gn_b):
    length = proj.shape[0]
    nc = _chunks_per_step(length, (5, 4, 2, 1))
    t = nc * CHUNK
    lg = jnp.log1p(-jnp.exp2(-5.0 - jnp.arange(RET_HEADS, dtype=F32)))
    i = jnp.arange(CHUNK, dtype=F32)
    intra = jnp.exp(lg[:, None, None] * jnp.abs(i[:, None] - i[None, :]))
    qdec = jnp.broadcast_to(jnp.exp(lg[:, None] * (i[None, :] + 1.0))[:, :, None], (RET_HEADS, CHUNK, RET_DK))
    kdec = jnp.broadcast_to(jnp.exp(lg[:, None] * (CHUNK - 1.0 - i[None, :]))[:, :, None], (RET_HEADS, CHUNK, RET_DK))
    cdec = jnp.broadcast_to(jnp.exp(lg * CHUNK)[:, None, None], (RET_HEADS, SUBLANES, RET_DV))
    col = lambda c0: pl.BlockSpec((t, RET_W), lambda c: (c, c0 // RET_W))
    tab = pl.BlockSpec((t, RET_DK), lambda c: (c, 0))
    whole = lambda a: pl.BlockSpec(a.shape, lambda c: (0,) * a.ndim)
    gng, gnb = gn_g.reshape(RET_HEADS, 1, RET_DV), gn_b.reshape(RET_HEADS, 1, RET_DV)
    consts = (intra, qdec, kdec, cdec, gng, gnb)
    return pl.pallas_call(
        functools.partial(_ret_kernel, nc=nc),
        grid=(length // t,),
        in_specs=[col(RET_Q0), col(RET_K0), col(RET_V0), col(RET_G0), tab, tab] + [whole(a) for a in consts],
        out_specs=pl.BlockSpec((t, RET_W), lambda c: (c, 0)),
        out_shape=jax.ShapeDtypeStruct((length, RET_W), BF16),
        scratch_shapes=[pltpu.VMEM((RET_HEADS, RET_DK, RET_DV), F32)],
        compiler_params=_cparams(("arbitrary",)),
        name="retention",
    )(proj, proj, proj, proj, cos, sin, *consts)


def _gla_kernel(q_ref, k_ref, v_ref, g_ref, ga_ref, aup_ref, ab_ref, gng_ref, gnb_ref, o_ref, state_ref, *, nc):
    @pl.when(pl.program_id(0) == 0)
    def _():
        state_ref[...] = jnp.zeros_like(state_ref)

    z = _dot_hi(ga_ref[...], aup_ref[...]) + ab_ref[...]
    log_a = -_softplus(-z) / GLA_TAU
    q = _by_head((q_ref[...].astype(F32) * (GLA_DK ** -0.5)).astype(BF16), GLA_HEADS, GLA_DK)
    k = k_ref[...].astype(F32)
    v = _by_head(v_ref[...], GLA_HEADS, GLA_DV)
    row = lax.broadcasted_iota(jnp.int32, (CHUNK, CHUNK), 0)
    colm = lax.broadcasted_iota(jnp.int32, (CHUNK, CHUNK), 1)
    later = (colm > row).astype(BF16)
    state = state_ref[...]
    outs = []
    for j in range(nc):
        sl = slice(j * CHUNK, (j + 1) * CHUNK)
        la = log_a[sl]
        to_end = _dot_sel_lhs(later, la)
        dec = _by_head(jnp.exp(to_end[0:1] + la[0:1]), GLA_HEADS, GLA_DK)
        kc = _by_head((k[sl] * jnp.exp(to_end)).astype(BF16), GLA_HEADS, GLA_DK)
        state = state * dec + _bmm_tn(v[:, sl], kc)
        outs.append(_bmm_nt(q[:, sl], state.astype(BF16)))
    state_ref[...] = state
    y = _groupnorm_lanes(jnp.concatenate(outs, axis=1), gng_ref[...], gnb_ref[...], GN_EPS)
    gate = _by_head(g_ref[...].astype(F32), GLA_HEADS, GLA_DV)
    o_ref[...] = _heads_to_lanes(y * (gate * _sigmoid(gate))).astype(o_ref.dtype)


def _gla(proj, small, a_up, a_bias, gn_g, gn_b):
    length = proj.shape[0]
    nc = _chunks_per_step(length, (5, 4, 2, 1))
    t = nc * CHUNK
    a_up_pad = jnp.zeros((LANES, GLA_WK), F32).at[:GLA_RANK].set(a_up)
    kcol = lambda c0: pl.BlockSpec((t, GLA_WK), lambda c: (c, c0 // GLA_WK))
    vcol = lambda c0: pl.BlockSpec((t, GLA_WV), lambda c: (c, c0 // GLA_WV))
    whole = lambda a: pl.BlockSpec(a.shape, lambda c: (0,) * a.ndim)
    consts = (a_up_pad, a_bias[None], gn_g.reshape(GLA_HEADS, 1, GLA_DV), gn_b.reshape(GLA_HEADS, 1, GLA_DV))
    return pl.pallas_call(
        functools.partial(_gla_kernel, nc=nc),
        grid=(length // t,),
        in_specs=[kcol(GLA_Q0), kcol(GLA_K0), vcol(GLA_V0), vcol(GLA_G0),
                  pl.BlockSpec((t, LANES), lambda c: (c, 2))] + [whole(a) for a in consts],
        out_specs=pl.BlockSpec((t, GLA_WV), lambda c: (c, 0)),
        out_shape=jax.ShapeDtypeStruct((length, GLA_WV), BF16),
        scratch_shapes=[pltpu.VMEM((GLA_HEADS, GLA_DV, GLA_DK), F32)],
        compiler_params=_cparams(("arbitrary",)),
        name="gla",
    )(proj, proj, proj, proj, small, *consts)


RW_PAIRS = RWKV_HEADS // 2


def _head_sum(x, ones_blk):
    parts = [_dot_sel_rhs16(x[:, p * LANES:(p + 1) * LANES], ones_blk) for p in range(RW_PAIRS)]
    return jnp.concatenate(parts, axis=1)


def _bmm(a, b):
    return jnp.einsum('pik,pkj->pij', a, b, preferred_element_type=F32)


def _bmm_nt(a, b):
    return jnp.einsum('pik,pjk->pij', a, b, preferred_element_type=F32)


def _bmm_tn(a, b):
    return jnp.einsum('pki,pkj->pij', a, b, preferred_element_type=F32)


def _bmm3(a, b):
    (ah, al), (bh, bl) = _split(a), _split(b)
    return _bmm(jnp.concatenate([ah, ah, al], axis=2), jnp.concatenate([bh, bl, bh], axis=1))


def _bmm3_tn(a, b):
    (ah, al), (bh, bl) = _split(a), _split(b)
    return _bmm_tn(jnp.concatenate([ah, ah, al], axis=1), jnp.concatenate([bh, bl, bh], axis=1))


def _unit_lower_inverse_minus_eye(m, i0, i1):
    def joins(s):
        return jnp.logical_and(i0 // (2 * s) == i1 // (2 * s),
                               jnp.logical_and((i0 // s) % 2 == 1, (i1 // s) % 2 == 0))

    xp = -jnp.where(joins(1), m, 0.0)
    s = 2
    while s < CHUNK:
        c = jnp.where(joins(s), m, 0.0)
        y = c + _bmm3(xp, c)
        xp = xp - (y + _bmm3(y, xp))
        s *= 2
    return xp


def _rwkv_kernel(r_ref, k_ref, v_ref, sm_ref, mixr_ref, mixk_ref, mixv_ref, mixs_ref, wup_ref, w0_ref,
                 aup_ref, a0_ref, gup_ref, kk_ref, ka_ref, rk_ref, gng_ref, gnb_ref, o_ref,
                 state_ref, cr_ref, ck_ref, cv_ref, cs_ref, a_s, bt_s, kt_s, bp_s, kp_s, v_s, r_s, gam_s,
                 o_s, g_s, bonus_s, y_s, u_s, *, nc):
    t = nc * CHUNK

    @pl.when(pl.program_id(0) == 0)
    def _():
        state_ref[...] = jnp.zeros_like(state_ref)
        cr_ref[...] = jnp.zeros_like(cr_ref)
        ck_ref[...] = jnp.zeros_like(ck_ref)
        cv_ref[...] = jnp.zeros_like(cv_ref)
        cs_ref[...] = jnp.zeros_like(cs_ref)

    rowi = lax.broadcasted_iota(jnp.int32, (t, 1), 0)

    def shifted(x, carry_ref, mix):
        prev = jnp.where(rowi == 0, carry_ref[SUBLANES - 1:SUBLANES], pltpu.roll(x, 1, 0))
        carry_ref[...] = x[t - SUBLANES:t]
        return x + (prev - x) * mix

    r = shifted(r_ref[...].astype(F32), cr_ref, mixr_ref[...])
    k = shifted(k_ref[...].astype(F32), ck_ref, mixk_ref[...])
    v = shifted(v_ref[...].astype(F32), cv_ref, mixv_ref[...])
    sm = shifted(sm_ref[...], cs_ref, mixs_ref[...])
    wa, gl = sm[:, :LANES], sm[:, LANES:]

    i0 = lax.broadcasted_iota(jnp.int32, (LANES, LANES), 0)
    i1 = lax.broadcasted_iota(jnp.int32, (LANES, LANES), 1)
    same_head = (i0 // RWKV_DH) == (i1 // RWKV_DH)
    ones_blk = same_head.astype(BF16)

    w = -_softplus(-(w0_ref[...] + _dot_hi(jnp.tanh(wa), wup_ref[...]))) - 0.5
    a = _sigmoid(a0_ref[...] + _dot_hi(wa, aup_ref[...]))
    g_s[...] = _dot_hi(_sigmoid(gl), gup_ref[...])
    kk = k * kk_ref[...]
    kkn = kk / jnp.maximum(jnp.sqrt(_head_sum(kk * kk, ones_blk)), NORM_EPS)
    km = k * (1.0 + (a - 1.0) * ka_ref[...])
    b = kkn * a
    bonus_s[...] = _head_sum(r * km * rk_ref[...], ones_blk) * v
    v_s[...] = v
    r_s[...] = r

    log_d = -jnp.exp(w)
    c0 = lax.broadcasted_iota(jnp.int32, (CHUNK, CHUNK), 0)
    c1 = lax.broadcasted_iota(jnp.int32, (CHUNK, CHUNK), 1)
    upto = (c1 <= c0).astype(BF16)
    for j in range(nc):
        rows = slice(j * CHUNK, (j + 1) * CHUNK)
        ld = log_d[rows]
        c = _dot_sel_lhs(upto, ld)
        c_end = c[CHUNK - 1:CHUNK]
        grow = jnp.exp(-c)
        to_end = jnp.exp(c_end - c)
        a_s[rows] = kkn[rows] * jnp.exp(c - ld)
        bt_s[rows] = b[rows] * grow
        kt_s[rows] = km[rows] * grow
        bp_s[rows] = b[rows] * to_end
        kp_s[rows] = km[rows] * to_end
        gam_s[j * SUBLANES:(j + 1) * SUBLANES] = jnp.broadcast_to(jnp.exp(c_end), (SUBLANES, RWKV_W))

    lo = lax.broadcasted_iota(jnp.int32, (CHUNK, LANES), 1) < RWKV_DH
    strict_lower = jnp.logical_and(same_head, i1 < i0)

    def by_pair(x):
        return jnp.stack([x[:, p * LANES:(p + 1) * LANES] for p in range(RW_PAIRS)], axis=0)

    def stacked(ref, rows):
        x = by_pair(ref[rows, :])
        return jnp.concatenate([jnp.where(lo, x, 0.0), jnp.where(lo, 0.0, x)], axis=1)

    def chunk_local(j, carry):
        rows = pl.ds(pl.multiple_of(j * CHUNK, CHUNK), CHUNK)
        a2, bp2 = stacked(a_s, rows), stacked(bp_s, rows)
        m_all = _bmm_nt(a2.astype(BF16),
                        jnp.concatenate([stacked(bt_s, rows), stacked(kt_s, rows)], axis=1).astype(BF16))
        m_b = jnp.where(strict_lower, m_all[:, :, :LANES], 0.0)
        m_k = jnp.where(strict_lower, m_all[:, :, LANES:], 0.0)
        t_mi = _unit_lower_inverse_minus_eye(m_b, i0, i1)
        z2 = bp2 + _bmm3_tn(t_mi, bp2)
        yk = _bmm3_tn(jnp.concatenate([a2, m_k], axis=2), z2)
        kpp = stacked(kp_s, rows) - yk[:, LANES:]
        uu = _bmm_tn(stacked(v_s, rows).astype(BF16), kpp.astype(BF16))
        y_s[j] = yk[:, :LANES].astype(BF16)
        u_s[j] = uu[:, :RWKV_DH] + uu[:, RWKV_DH:]
        return carry

    lax.fori_loop(0, nc, chunk_local, 0)

    def chunk_advance(j, carry):
        rows = pl.ds(pl.multiple_of(j * CHUNK, CHUNK), CHUNK)
        gam = by_pair(gam_s[pl.ds(pl.multiple_of(j * SUBLANES, SUBLANES), SUBLANES)])[:, 0:1]
        s = state_ref[...]
        s_hi, s_lo = _split(s)
        y = y_s[j]
        s = s * gam - _bmm(jnp.concatenate([s_hi, s_lo], axis=2), jnp.concatenate([y, y], axis=1)) + u_s[j]
        state_ref[...] = s
        o2 = _bmm_nt(stacked(r_s, rows).astype(BF16), s.astype(BF16))
        o_pair = jnp.concatenate([o2[:, :CHUNK], o2[:, CHUNK:]], axis=2)
        o_s[rows, :] = jnp.concatenate([o_pair[p] for p in range(RW_PAIRS)], axis=1)
        return carry

    lax.fori_loop(0, nc, chunk_advance, 0)

    o = o_s[...]
    mu = _head_sum(o, ones_blk) * (1.0 / RWKV_DH)
    oc = o - mu
    var = _head_sum(oc * oc, ones_blk) * (1.0 / RWKV_DH)
    y = oc * lax.rsqrt(var + RWKV_GN_EPS) * gng_ref[...] + gnb_ref[...]
    o_ref[...] = ((y + bonus_s[...]) * g_s[...]).astype(o_ref.dtype)


def _rwkv(proj, small, mix, w_up, w0, a_up, a0, g_up, k_k, k_a, r_k, gn_g, gn_b):
    length = proj.shape[0]
    nc = _chunks_per_step(length, (5, 4, 2, 1))
    t = nc * CHUNK
    w = RWKV_W
    mix_s = mix[3 * w:][None]
    wup_pad = jnp.zeros((LANES, w), F32).at[:RWKV_RANK_W].set(w_up)
    aup_pad = jnp.zeros((LANES, w), F32).at[RWKV_RANK_W:].set(a_up)
    col = lambda c0: pl.BlockSpec((t, w), lambda c: (c, c0 // w))
    vec = pl.BlockSpec((1, w), lambda c: (0, 0))
    mat = pl.BlockSpec((LANES, w), lambda c: (0, 0))
    wide = lambda: pltpu.VMEM((t, w), F32)
    carry = lambda n: pltpu.VMEM((SUBLANES, n), F32)
    return pl.pallas_call(
        functools.partial(_rwkv_kernel, nc=nc),
        grid=(length // t,),
        in_specs=[col(RW_R0), col(RW_K0), col(RW_V0), pl.BlockSpec((t, 2 * LANES), lambda c: (c, 0)),
                  vec, vec, vec, pl.BlockSpec((1, 2 * LANES), lambda c: (0, 0)),
                  mat, vec, mat, vec, mat, vec, vec, vec, vec, vec],
        out_specs=pl.BlockSpec((t, w), lambda c: (c, 0)),
        out_shape=jax.ShapeDtypeStruct((length, w), BF16),
        scratch_shapes=[pltpu.VMEM((RW_PAIRS, RWKV_DH, LANES), F32),
                        carry(w), carry(w), carry(w), carry(2 * LANES),
                        wide(), wide(), wide(), wide(), wide(), wide(), wide(),
                        pltpu.VMEM((nc * SUBLANES, w), F32), wide(), wide(), wide(),
                        pltpu.VMEM((nc, RW_PAIRS, LANES, LANES), BF16),
                        pltpu.VMEM((nc, RW_PAIRS, RWKV_DH, LANES), F32)],
        compiler_params=_cparams(("arbitrary",)),
        name="rwkv7",
    )(proj, proj, proj, small, mix[None, :w], mix[None, w:2 * w], mix[None, 2 * w:3 * w], mix_s,
      wup_pad, w0[None], aup_pad, a0[None], g_up, k_k[None], k_a[None], r_k.reshape(1, w),
      gn_g[None], gn_b[None])


def _merge_kernel(oret_ref, ogla_ref, orw_ref, wret_ref, wgla_ref, wrw_ref, gret_ref, ggla_ref, grw_ref, o_ref):
    acc = _sigmoid(gret_ref[...].astype(F32)) * _dot(oret_ref[...], wret_ref[...])
    acc = acc + _sigmoid(ggla_ref[...].astype(F32)) * _dot(ogla_ref[...], wgla_ref[...])
    acc = acc + _sigmoid(grw_ref[...].astype(F32)) * _dot(orw_ref[...], wrw_ref[...])
    o_ref[...] = acc.astype(o_ref.dtype)


def _merge(o_ret, o_gla, o_rw, w_ret, w_gla, w_rw, layer, gates):
    m = o_ret.shape[0]
    tm = _pick(m, (640, 256, 128, 64))
    tn = 512
    act = lambda n: pl.BlockSpec((tm, n), lambda i, j: (i, 0))
    wgt = lambda n: pl.BlockSpec((None, n, tn), lambda i, j: (layer, 0, j))
    gate = lambda b: pl.BlockSpec((tm, tn), lambda i, j: (i, b * (D_MODEL // tn) + j))
    return pl.pallas_call(
        _merge_kernel,
        grid=(m // tm, D_MODEL // tn),
        in_specs=[act(RET_W), act(GLA_WV), act(RWKV_W), wgt(RET_W), wgt(GLA_WV), wgt(RWKV_W),
                  gate(0), gate(1), gate(2)],
        out_specs=pl.BlockSpec((tm, tn), lambda i, j: (i, j)),
        out_shape=jax.ShapeDtypeStruct((m, D_MODEL), BF16),
        compiler_params=_cparams(("parallel", "parallel")),
        name="branch_merge",
    )(o_ret, o_gla, o_rw, w_ret, w_gla, w_rw, gates, gates, gates)


def _route(h, rw, bias, eidx_ref, ewts_ref):
    tm = h.shape[0]
    (hh, hl), (wh, wl) = _split(h), _split(rw)
    logits = (_dot(hh, wl) + _dot(hl, wh)) + _dot(hh, wh)
    s_all = _sigmoid(logits.T[0:N_EXPERTS])
    sel_all = s_all + bias
    ids = lax.broadcasted_iota(jnp.int32, (EXPERTS_PER_GROUP, tm), 0)
    neg = jnp.float32(-jnp.inf)
    best_score = best_e1 = best_e2 = best_w1 = best_w2 = None
    for gidx in range(N_GROUPS):
        rows = slice(gidx * EXPERTS_PER_GROUP, (gidx + 1) * EXPERTS_PER_GROUP)
        sel, s = sel_all[rows], s_all[rows]
        m1 = jnp.max(sel, axis=0, keepdims=True)
        i1 = jnp.min(jnp.where(sel == m1, ids, EXPERTS_PER_GROUP), axis=0, keepdims=True)
        rest = jnp.where(ids == i1, neg, sel)
        m2 = jnp.max(rest, axis=0, keepdims=True)
        i2 = jnp.min(jnp.where(rest == m2, ids, EXPERTS_PER_GROUP), axis=0, keepdims=True)
        score = m1 + m2
        w1 = jnp.sum(jnp.where(ids == i1, s, 0.0), axis=0, keepdims=True)
        w2 = jnp.sum(jnp.where(ids == i2, s, 0.0), axis=0, keepdims=True)
        e1 = i1 + gidx * EXPERTS_PER_GROUP
        e2 = i2 + gidx * EXPERTS_PER_GROUP
        if gidx == 0:
            best_score, best_e1, best_e2, best_w1, best_w2 = score, e1, e2, w1, w2
        else:
            take = score > best_score
            best_score = jnp.where(take, score, best_score)
            best_e1, best_e2 = jnp.where(take, e1, best_e1), jnp.where(take, e2, best_e2)
            best_w1, best_w2 = jnp.where(take, w1, best_w1), jnp.where(take, w2, best_w2)
    total = best_w1 + best_w2
    slot = lax.broadcasted_iota(jnp.int32, (SUBLANES, tm), 0)
    eidx_ref[...] = jnp.where(slot == 0, best_e1, jnp.where(slot == 1, best_e2, 0))
    ewts_ref[...] = jnp.where(slot == 0, best_w1 / total, jnp.where(slot == 1, best_w2 / total, 0.0))


def _res_ln_route_kernel(h_ref, f_ref, g_ref, b_ref, rw_ref, bias_ref, o_ref, op_ref, eidx_ref, ewts_ref, *, alpha, tm):
    y = _layernorm_rows(h_ref[...] * alpha + f_ref[...], g_ref[...], b_ref[...])
    _emit_ln(y, o_ref, op_ref, 0, tm)
    _route(y, rw_ref[...], bias_ref[...], eidx_ref, ewts_ref)


def _layer_norm_route(h, f, g, b, router_w, router_bias):
    m, d = h.shape
    tm = _pick(m, (256, 128))
    rw_pad = jnp.zeros((d, LANES), F32).at[:, :N_EXPERTS].set(router_w)
    row = pl.BlockSpec((tm, d), lambda i: (i, 0))
    vec = pl.BlockSpec((1, d), lambda i: (0, 0))
    lanes = pl.BlockSpec((SUBLANES, tm), lambda i: (0, i))
    return pl.pallas_call(
        functools.partial(_res_ln_route_kernel, alpha=DN_ALPHA, tm=tm),
        grid=(m // tm,),
        in_specs=[row, row, vec, vec, pl.BlockSpec((d, LANES), lambda i: (0, 0)),
                  pl.BlockSpec((N_EXPERTS, 1), lambda i: (0, 0))],
        out_specs=(row, pl.BlockSpec((tm, d // 2), lambda i: (i, 0)), lanes, lanes),
        out_shape=(jax.ShapeDtypeStruct((m, d), F32), jax.ShapeDtypeStruct((m, d // 2), jnp.uint32),
                   jax.ShapeDtypeStruct((SUBLANES, m), jnp.int32), jax.ShapeDtypeStruct((SUBLANES, m), F32)),
        compiler_params=_cparams(("parallel",)),
        name="res_layernorm_router",
    )(h, f, g[None], b[None], rw_pad, router_bias[:, None])


def _row_copy(src_hbm, row, dst_buf, sem):
    return pltpu.make_async_copy(src_hbm.at[pl.ds(row, 1)], dst_buf, sem)


def _ffn_kernel(te_ref, rt_ref, na_ref, h_hbm, w1_ref, w3_ref, w2_ref, y_ref, xbuf, sem, *, tm):
    del te_ref
    i = pl.program_id(0)
    n = pl.num_programs(0)
    n_active = na_ref[0]
    ahead = FFN_SLOTS - 1
    slot = i % FFN_SLOTS

    def start_tile(tile, dst_slot):
        for r in range(tm):
            _row_copy(h_hbm, rt_ref[tile * tm + r], xbuf.at[dst_slot, pl.ds(r, 1)],
                      sem.at[dst_slot]).start(priority=r % DMA_THREADS)

    def wait_tile(dst_slot):
        def body(r, c):
            _row_copy(h_hbm, 0, xbuf.at[dst_slot, pl.ds(r, 1)], sem.at[dst_slot]).wait()
            return c
        lax.fori_loop(0, tm, body, 0, unroll=DMA_UNROLL)

    @pl.when(i == 0)
    def _():
        for t in range(ahead):
            start_tile(jnp.minimum(t, n - 1), t)

    wait_tile(slot)
    nxt = jnp.minimum(i + ahead, n - 1)
    nxt_slot = (i + ahead) % FFN_SLOTS

    @pl.when(i < n_active)
    def _():
        start_tile(nxt, nxt_slot)
        x = _unpack_halves(xbuf[slot]).astype(BF16)
        h1 = _dot(x, w1_ref[...])
        h3 = _dot(x, w3_ref[...])
        he = (h1 * _sigmoid(h1)) * h3
        y_ref[...] = _pack_halves(_dot(he.astype(BF16), w2_ref[...]))

    @pl.when(i >= n_active)
    def _():
        start_tile(nxt, nxt_slot)
        y_ref[...] = jnp.zeros_like(y_ref)

    @pl.when(i == n - 1)
    def _():
        for t in range(1, FFN_SLOTS):
            wait_tile((i + t) % FFN_SLOTS)


def _expert_ffn(hp, tile_expert, row_token, n_active, layer, w1, w3, w2):
    half = hp.shape[1]
    d = 2 * half
    tm = MOE_TM
    n_tiles = tile_expert.shape[0]
    wspec = lambda shape: pl.BlockSpec((None, None) + shape, lambda i, te, rt, na: (layer, te[i], 0, 0))
    return pl.pallas_call(
        functools.partial(_ffn_kernel, tm=tm),
        grid_spec=pltpu.PrefetchScalarGridSpec(
            num_scalar_prefetch=3,
            grid=(n_tiles,),
            in_specs=[pl.BlockSpec(memory_space=pl.ANY), wspec((d, D_FF_PAD)), wspec((d, D_FF_PAD)),
                      wspec((D_FF_PAD, d))],
            out_specs=pl.BlockSpec((tm, half), lambda i, te, rt, na: (i, 0)),
            scratch_shapes=[pltpu.VMEM((FFN_SLOTS, tm, half), jnp.uint32), pltpu.SemaphoreType.DMA((FFN_SLOTS,))]),
        out_shape=jax.ShapeDtypeStruct((n_tiles * tm, half), jnp.uint32),
        compiler_params=_cparams(("arbitrary",)),
        name="expert_ffn",
    )(tile_expert, row_token, n_active, hp, w1, w3, w2)


def _combine_ln_kernel(p1_ref, p2_ref, h_ref, y_hbm, w_ref, g_ref, b_ref, o_ref, ob_ref, ybuf, sem, *, tm, n_mask):
    i = pl.program_id(0)
    n = pl.num_programs(0)
    slot = i % 2

    def start_tile(tile, dst_slot):
        for r in range(tm):
            _row_copy(y_hbm, p1_ref[tile * tm + r], ybuf.at[dst_slot, 0, pl.ds(r, 1)], sem.at[dst_slot]).start(priority=0)
            _row_copy(y_hbm, p2_ref[tile * tm + r], ybuf.at[dst_slot, 1, pl.ds(r, 1)],
                      sem.at[dst_slot]).start(priority=DMA_THREADS - 1)

    def wait_tile(dst_slot):
        def body(r, c):
            _row_copy(y_hbm, 0, ybuf.at[dst_slot, 0, pl.ds(r, 1)], sem.at[dst_slot]).wait()
            _row_copy(y_hbm, 0, ybuf.at[dst_slot, 1, pl.ds(r, 1)], sem.at[dst_slot]).wait()
            return c
        lax.fori_loop(0, tm, body, 0, unroll=DMA_UNROLL)

    @pl.when(i == 0)
    def _():
        start_tile(0, 0)

    wait_tile(slot)
    start_tile(jnp.minimum(i + 1, n - 1), 1 - slot)
    w = w_ref[...]
    ffn = w[:, 0:1] * _unpack_halves(ybuf[slot, 0]) + w[:, 1:2] * _unpack_halves(ybuf[slot, 1])
    x = h_ref[...] * DN_ALPHA + ffn
    _emit_ln(_layernorm_rows(x, g_ref[...], b_ref[...]), o_ref, ob_ref, n_mask, tm)

    @pl.when(i == n - 1)
    def _():
        wait_tile(1 - slot)


def _combine_ln(h, y, pos1, pos2, wts, g, b, n_mask, frames_only):
    m, d = h.shape
    tm = FRONT if frames_only else _pick(m, (256, 128, 64))
    row = pl.BlockSpec((tm, d), lambda i, p1, p2: (i, 0))
    vec = pl.BlockSpec((1, d), lambda i, p1, p2: (0, 0))
    if frames_only:
        out_rows, out_row = m - tm, pl.BlockSpec((tm, d), lambda i, p1, p2: (jnp.maximum(i - 1, 0), 0))
    else:
        out_rows, out_row = m, row
    return pl.pallas_call(
        functools.partial(_combine_ln_kernel, tm=tm, n_mask=n_mask),
        grid_spec=pltpu.PrefetchScalarGridSpec(
            num_scalar_prefetch=2,
            grid=(m // tm,),
            in_specs=[row, pl.BlockSpec(memory_space=pl.ANY),
                      pl.BlockSpec((tm, 2), lambda i, p1, p2: (i, 0)), vec, vec],
            out_specs=(out_row, row),
            scratch_shapes=[pltpu.VMEM((2, 2, tm, d // 2), jnp.uint32), pltpu.SemaphoreType.DMA((2,))]),
        out_shape=(jax.ShapeDtypeStruct((out_rows, d), F32), jax.ShapeDtypeStruct((m, d), BF16)),
        compiler_params=_cparams(("arbitrary",)),
        name="moe_combine_layernorm",
    )(pos1, pos2, h, y, wts, g[None], b[None])


def _dispatch_plan(eidx, tm):
    m = eidx.shape[1]
    flat_e = jnp.concatenate([eidx[0], eidx[1]])
    tok = jnp.arange(m, dtype=jnp.int32)
    flat_t = jnp.concatenate([tok, tok])
    onehot = (flat_e[:, None] == jnp.arange(N_EXPERTS, dtype=jnp.int32)[None, :]).astype(jnp.int32)
    rank = jnp.take_along_axis(jnp.cumsum(onehot, axis=0) - onehot, flat_e[:, None], axis=1)[:, 0]
    counts = jnp.sum(onehot, axis=0)
    padded = ((counts + tm - 1) // tm) * tm
    ends = jnp.cumsum(padded)
    offs = ends - padded
    pos = (offs[flat_e] + rank).astype(jnp.int32)
    n_tiles = (2 * m) // tm + N_EXPERTS
    row_token = jnp.zeros((n_tiles * tm,), jnp.int32).at[pos].set(flat_t)
    n_active = (ends[-1] // tm).astype(jnp.int32)
    tile_id = jnp.arange(n_tiles, dtype=jnp.int32)
    tile_id = jnp.minimum(tile_id, n_active - 1)
    tile_expert = (jnp.sum(tile_id[:, None] >= (offs // tm)[None, :], axis=1) - 1).astype(jnp.int32)
    return tile_expert, row_token, n_active.reshape(1), pos[:m], pos[m:]


def _moe_ln(h, hp, eidx, ewts, layer, w1, w3, w2, g, b, n_mask, frames_only=False):
    tile_expert, row_token, n_active, pos1, pos2 = _dispatch_plan(eidx, MOE_TM)
    y = _expert_ffn(hp, tile_expert, row_token, n_active, layer, w1, w3, w2)
    return _combine_ln(h, y, pos1, pos2, ewts[0:2].T, g, b, n_mask, frames_only)


def _cast_pad_kernel(x_ref, o_ref):
    rows, cols = x_ref.shape
    o_ref[...] = jnp.zeros(o_ref.shape, o_ref.dtype)
    o_ref[0:rows, 0:cols] = x_ref[...].astype(o_ref.dtype)


def _cast_pad(w, rows_out, cols_out, block_rows, block_cols):
    n_l, n_e, r, c = w.shape
    in_rows = r if rows_out != r else block_rows
    in_cols = c if cols_out != c else block_cols
    out_rows = rows_out if rows_out != r else block_rows
    out_cols = cols_out if cols_out != c else block_cols
    grid = (n_l * n_e, rows_out // out_rows, cols_out // out_cols)
    idx = lambda e, i, j: (e // n_e, e % n_e, i, j)
    return pl.pallas_call(
        _cast_pad_kernel,
        grid=grid,
        in_specs=[pl.BlockSpec((None, None, in_rows, in_cols), idx)],
        out_specs=pl.BlockSpec((None, None, out_rows, out_cols), idx),
        out_shape=jax.ShapeDtypeStruct((n_l, n_e, rows_out, cols_out), BF16),
        compiler_params=_cparams(("parallel", "parallel", "parallel")),
        name="cast_pad_weights",
    )(w)


def _cast_pad_transposed_kernel(x_ref, o_ref):
    rows, cols = x_ref.shape
    x = jnp.concatenate([x_ref[...], jnp.zeros((o_ref.shape[1] - rows, cols), x_ref.dtype)], axis=0)
    o_ref[...] = x.T.astype(o_ref.dtype)


def _cast_pad_transposed(wt, cols_out, block):
    n_l, n_e, c, r = wt.shape
    return pl.pallas_call(
        _cast_pad_transposed_kernel,
        grid=(n_l * n_e, r // block),
        in_specs=[pl.BlockSpec((None, None, c, block), lambda e, i: (e // n_e, e % n_e, 0, i))],
        out_specs=pl.BlockSpec((None, None, block, cols_out), lambda e, i: (e // n_e, e % n_e, i, 0)),
        out_shape=jax.ShapeDtypeStruct((n_l, n_e, r, cols_out), BF16),
        compiler_params=_cparams(("parallel", "parallel")),
        name="cast_pad_weights_t",
    )(wt)


def _prep_experts(w1, w3, w2):
    d = w1.shape[2]
    w1 = _cast_pad_transposed(jnp.swapaxes(w1, 2, 3), D_FF_PAD, 1024)
    w3 = _cast_pad_transposed(jnp.swapaxes(w3, 2, 3), D_FF_PAD, 1024)
    w2 = _cast_pad(w2, D_FF_PAD, d, D_FF_PAD, 2048)
    return w1, w3, w2


def _rotary_tables(length):
    half = RET_DK // 2
    pos = jnp.arange(length, dtype=jnp.int32) - N_MASK
    inv = ROPE_BASE ** (-jnp.arange(half, dtype=F32) / half)
    ang = pos.astype(F32)[:, None] * inv[None, :]
    cos, sin = jnp.cos(ang), jnp.sin(ang)
    return jnp.concatenate([cos, cos], axis=1), jnp.concatenate([-sin, sin], axis=1)


def _mixer(hb, cos, sin, layer, w_in, ret_gn_g, ret_gn_b, gla_a_up, gla_a_bias, gla_gn_g, gla_gn_b, rw_mix, rw_w_up,
           rw_w0, rw_a_up, rw_a0, rw_g_up, rw_k_k, rw_k_a, rw_r_k, rw_gn_g, rw_gn_b, w_br_ret, w_br_gla,
           w_br_rw, w_out):
    rows = lambda a, b: lax.slice(w_in, (layer, a, 0), (layer + 1, b, D_MODEL))
    w_small = jnp.concatenate([rows(W_IN_SM0, W_IN_GATE0), rows(W_IN_GA0, W_IN_RW0),
                               jnp.zeros((1, LANES - GLA_RANK, D_MODEL), F32)], axis=1)
    proj = _matmul_nt(hb, w_in, layer, 0, N_HEAD, BF16)
    proj_rw = _matmul_nt(hb, w_in, layer, W_IN_RW0, W_IN_SM0 - W_IN_RW0, BF16)
    gates = _matmul_nt(hb, w_in, layer, W_IN_GATE0, w_in.shape[1] - W_IN_GATE0, BF16)
    small = _matmul_nt(hb, w_small, 0, 0, N_SMALL, F32, tn=N_SMALL)
    o_ret = _retention(proj, cos, sin, ret_gn_g, ret_gn_b)
    o_gla = _gla(proj, small, gla_a_up, gla_a_bias, gla_gn_g, gla_gn_b)
    o_rw = _rwkv(proj_rw, small, rw_mix, rw_w_up, rw_w0, rw_a_up, rw_a0, rw_g_up, rw_k_k, rw_k_a, rw_r_k,
                 rw_gn_g, rw_gn_b)
    merged = _merge(o_ret, o_gla, o_rw, w_br_ret, w_br_gla, w_br_rw, layer, gates)
    return _matmul(merged, w_out, F32, 512, layer=layer)


def kernel(x, meta, ln_in_g, ln_in_b, w_in, ret_gn_g, ret_gn_b, gla_a_up, gla_a_bias, gla_gn_g, gla_gn_b, rw_mix, rw_w_up, rw_w0, rw_a_up, rw_a0, rw_g_up, rw_k_k, rw_k_a, rw_r_k, rw_gn_g, rw_gn_b, w_br_ret, w_br_gla, w_br_rw, w_out, ln1_g, ln1_b, router_w, router_bias, exp_w1, exp_w3, exp_w2, ln2_g, ln2_b):
    batch, seq, d = x.shape
    assert batch == 1 and d == D_MODEL
    assert meta.shape == (N_META, d)
    cos, sin = _rotary_tables(FRONT + seq)
    h, hb = _embed_layer_norm(x[0], meta.astype(x.dtype), ln_in_g, ln_in_b)
    w_in = jnp.swapaxes(w_in, 1, 2)
    w_out = w_out.astype(BF16)
    w_br_ret, w_br_gla, w_br_rw = w_br_ret.astype(BF16), w_br_gla.astype(BF16), w_br_rw.astype(BF16)
    w1, w3, w2 = _prep_experts(exp_w1, exp_w3, exp_w2)
    for l in range(DEPTH):
        mix = _mixer(hb, cos, sin, l, w_in, ret_gn_g[l], ret_gn_b[l], gla_a_up[l], gla_a_bias[l], gla_gn_g[l],
                     gla_gn_b[l], rw_mix[l], rw_w_up[l], rw_w0[l], rw_a_up[l], rw_a0[l], rw_g_up[l], rw_k_k[l],
                     rw_k_a[l], rw_r_k[l], rw_gn_g[l], rw_gn_b[l], w_br_ret, w_br_gla, w_br_rw, w_out)
        h, hp, eidx, ewts = _layer_norm_route(h, mix, ln1_g[l], ln1_b[l], router_w, router_bias)
        h, hb = _moe_ln(h, hp, eidx, ewts, l, w1, w3, w2, ln2_g[l], ln2_b[l], N_MASK,
                        frames_only=(l == DEPTH - 1))
    return h[None]
```
